```python
import math
import jax, jax.numpy as jnp
from jax import lax
import numpy as np

D_MODEL = 1024
BATCH = 8
SEQ = 2048
DEPTH = 2

GRID_W = 64
CTX_LEN = 256
N_MIXERS = 2
N_ATTN_LAYERS = (DEPTH + N_MIXERS - 1) // N_MIXERS
N_HGRN_LAYERS = DEPTH // N_MIXERS
DA_HEADS = 8
DA_QK_DIM = D_MODEL // (2 * DA_HEADS)
DA_V_DIM = 2 * DA_QK_DIM
ROPE_BASE = 10000.0
Q_BLOCK = 128
HG_HEADS = 8
HG_DIM = D_MODEL // HG_HEADS
CHUNK = 64
N_EXPERTS = 16
N_GROUPS = 4
EXPERTS_PER_GROUP = N_EXPERTS // N_GROUPS
TOP_K = 2
D_FF = D_MODEL // 2
EPS = 1e-6

kernel_name = "hybrid_diffattn_hgrn2_groupmoe_dit"


def rms_norm(x, g):
    xf = x.astype(jnp.float32)
    y = xf * lax.rsqrt(jnp.mean(xf * xf, axis=-1, keepdims=True) + EPS)
    return (y * g.astype(jnp.float32)).astype(x.dtype)


def axial_rope_tables(n_tokens):
    rows = n_tokens // GRID_W
    r, col = jnp.meshgrid(jnp.arange(rows), jnp.arange(GRID_W), indexing="ij")
    r = r.reshape(-1).astype(jnp.float32)
    col = col.reshape(-1).astype(jnp.float32)
    n_pairs = DA_QK_DIM // 4
    inv = ROPE_BASE ** (-jnp.arange(n_pairs, dtype=jnp.float32) / n_pairs)
    ang = jnp.concatenate([r[:, None] * inv, col[:, None] * inv], axis=-1)
    return jnp.cos(ang), jnp.sin(ang)


def apply_rope(x, cos, sin):
    xf = x.astype(jnp.float32)
    half = xf.shape[-1] // 2
    x1, x2 = xf[..., :half], xf[..., half:]
    cs, sn = cos[None, :, None, None, :], sin[None, :, None, None, :]
    return jnp.concatenate([x1 * cs - x2 * sn, x2 * cs + x1 * sn], axis=-1).astype(x.dtype)


def diff_attention_mixer(h, hc, w_qkv, w_o, q_norm, k_norm, sub_norm, lam_p, layer_idx, need_ctx_out):
    B, T, _ = h.shape
    lam_init = 0.8 - 0.6 * math.exp(-0.3 * layer_idx)
    lp = lam_p.astype(jnp.float32)
    lam = jnp.exp(jnp.sum(lp[0] * lp[1])) - jnp.exp(jnp.sum(lp[2] * lp[3])) + lam_init
    scale = 1.0 / math.sqrt(DA_QK_DIM)

    def project(t):
        b_, t_, _ = t.shape
        q, k, v = jnp.split(t @ w_qkv, 3, axis=-1)
        q = rms_norm(q.reshape(b_, t_, DA_HEADS, 2, DA_QK_DIM), q_norm)
        k = rms_norm(k.reshape(b_, t_, DA_HEADS, 2, DA_QK_DIM), k_norm)
        return q, k, v.reshape(b_, t_, DA_HEADS, DA_V_DIM)

    def attend(qb, keys, vals):
        s = jnp.einsum("bqhmd,bkhmd->bhmqk", qb.astype(jnp.float32), keys.astype(jnp.float32)) * scale
        p = jax.nn.softmax(s, axis=-1)
        a = p[:, :, 0] - lam * p[:, :, 1]
        return jnp.einsum("bhqk,bkhe->bqhe", a, vals.astype(jnp.float32))

    def head_out(o):
        o = rms_norm(o, sub_norm) * (1.0 - lam_init)
        return o.reshape(o.shape[0], o.shape[1], DA_HEADS * DA_V_DIM).astype(h.dtype) @ w_o

    q, k, v = project(h)
    cos, sin = axial_rope_tables(T)
    q, k = apply_rope(q, cos, sin), apply_rope(k, cos, sin)
    qc, kc, vc = project(hc)

    k_all = jnp.concatenate([k, kc], axis=1)
    v_all = jnp.concatenate([v, vc], axis=1)
    nb = T // Q_BLOCK
    q_blocks = q.reshape(B, nb, Q_BLOCK, DA_HEADS, 2, DA_QK_DIM).swapaxes(0, 1)
    o = lax.map(lambda qb: attend(qb, k_all, v_all), q_blocks)
    out = head_out(o.swapaxes(0, 1).reshape(B, T, DA_HEADS, DA_V_DIM))
    out_c = head_out(attend(qc, kc, vc)) if need_ctx_out else None
    return out, out_c


def gla_chunk_scan(q, k, v, logf, s0):
    B, T, H, K = q.shape
    V = v.shape[-1]
    n = T // CHUNK

    def to_chunks(t):
        return t.reshape(B, n, CHUNK, H, t.shape[-1]).swapaxes(0, 1)

    mask = jnp.tril(jnp.ones((CHUNK, CHUNK), dtype=bool))[None, :, :, None, None]

    def step(S, xs):
        qc, kc, vc, lf = xs
        b = jnp.cumsum(lf, axis=1)
        o_inter = jnp.einsum("bchk,bhkv->bchv", qc * jnp.exp(b), S)
        rel = b[:, :, None] - b[:, None, :]
        decay = jnp.where(mask, jnp.exp(jnp.where(mask, rel, 0.0)), 0.0)
        scores = jnp.einsum("bthk,bshk,btshk->bhts", qc, kc, decay)
        o_intra = jnp.einsum("bhts,bshv->bthv", scores, vc)
        b_last = b[:, -1]
        S_new = jnp.exp(b_last)[..., None] * S + jnp.einsum(
            "bshk,bshv->bhkv", kc * jnp.exp(b_last[:, None] - b), vc)
        return S_new, o_inter + o_intra

    S_fin, o = lax.scan(step, s0, (to_chunks(q), to_chunks(k), to_chunks(v), to_chunks(logf)))
    return o.swapaxes(0, 1).reshape(B, T, H, V), S_fin


def hgrn2_mixer(h, hc, w_in, w_o, out_norm, lb, need_ctx_out):
    lb = lb.astype(jnp.float32)

    def project(t):
        b_, t_, _ = t.shape
        q, v, zf, zb, g = jnp.split(t @ w_in, 5, axis=-1)
        shp = (b_, t_, HG_HEADS, HG_DIM)
        q = jax.nn.silu(q.astype(jnp.float32)).reshape(shp)
        v = v.astype(jnp.float32).reshape(shp)
        zf = zf.astype(jnp.float32).reshape(shp)
        zb = zb.astype(jnp.float32).reshape(shp)
        return q, v, zf, zb, g.astype(jnp.float32).reshape(shp)

    def forget(z, lbd):
        logf = jnp.logaddexp(jnp.log(lbd), jnp.log1p(-lbd) + jax.nn.log_sigmoid(z))
        return (1.0 - lbd) * jax.nn.sigmoid(-z), logf

    def flip(t):
        return jnp.flip(t, axis=1)

    def bidir(q, v, zf, zb, s_f, s_b):
        kf, lff = forget(zf, lb[0])
        kb, lfb = forget(zb, lb[1])
        o_f, S_f = gla_chunk_scan(q, kf, v, lff, s_f)
        o_b, S_b = gla_chunk_scan(flip(q), flip(kb), flip(v), flip(lfb), s_b)
        return o_f + flip(o_b), S_f, S_b

    def readout(o, g):
        o = rms_norm(o, out_norm) * jax.nn.silu(g)
        return o.reshape(o.shape[0], o.shape[1], HG_HEADS * HG_DIM).astype(h.dtype) @ w_o

    qc, vc, zfc, zbc, gc = project(hc)
    s0 = jnp.zeros((hc.shape[0], HG_HEADS, HG_DIM, HG_DIM), jnp.float32)
    oc, Sc_f, Sc_b = bidir(qc, vc, zfc, zbc, s0, s0)
    q, v, zf, zb, g = project(h)
    o, _, _ = bidir(q, v, zf, zb, Sc_f, Sc_b)
    out_c = readout(oc, gc) if need_ctx_out else None
    return readout(o, g), out_c


def moe_ffn(t, router_w, router_bias, w_gate, w_up, w_down):
    n = t.shape[0]
    aff = jax.nn.sigmoid(t.astype(jnp.float32) @ router_w.astype(jnp.float32))
    biased = aff + router_bias.astype(jnp.float32)
    grp_score = lax.top_k(biased.reshape(n, N_GROUPS, EXPERTS_PER_GROUP), TOP_K)[0].sum(-1)
    sel_group = jnp.argmax(grp_score, axis=-1)
    in_group = (jnp.arange(N_EXPERTS) // EXPERTS_PER_GROUP)[None, :] == sel_group[:, None]
    _, idx = lax.top_k(jnp.where(in_group, biased, -jnp.inf), TOP_K)
    w = jnp.take_along_axis(aff, idx, axis=-1)
    w = w / jnp.sum(w, axis=-1, keepdims=True)
    gates = jnp.sum(jax.nn.one_hot(idx, N_EXPERTS, dtype=jnp.float32) * w[..., None], axis=1)
    y = jnp.zeros((n, t.shape[1]), jnp.float32)
    for e in range(N_EXPERTS):
        hid = jax.nn.silu(t @ w_gate[e]) * (t @ w_up[e])
        y = y + gates[:, e:e + 1] * (hid @ w_down[e]).astype(jnp.float32)
    return y.astype(t.dtype)


def setup_inputs(seed: int = 0) -> dict:
    key = jax.random.key(seed)
    ks = jax.random.split(key, 24)
    D, HK = D_MODEL, HG_HEADS * HG_DIM
    nrm = jax.random.normal
    s = D ** -0.5
    return {
        "x": nrm(ks[0], (BATCH, SEQ, D), jnp.float32),
        "c": nrm(ks[1], (BATCH, D), jnp.float32),
        "ctx": nrm(ks[2], (BATCH, CTX_LEN, D), jnp.float32),
        "c_ctx": nrm(ks[3], (D,), jnp.float32),
        "ada_w": nrm(ks[4], (DEPTH, D, 6 * D), jnp.float32) * (0.5 * s),
        "ada_b": nrm(ks[5], (DEPTH, 6 * D), jnp.float32) * 0.02,
        "norm_mix": 1.0 + 0.1 * nrm(ks[6], (DEPTH, D), jnp.float32),
        "norm_ffn": 1.0 + 0.1 * nrm(ks[7], (DEPTH, D), jnp.float32),
        "attn_w_qkv": nrm(ks[8], (N_ATTN_LAYERS, D, 3 * D), jnp.float32) * s,
        "attn_w_o": nrm(ks[9], (N_ATTN_LAYERS, DA_HEADS * DA_V_DIM, D), jnp.float32) * s,
        "attn_q_norm": 1.0 + 0.1 * nrm(ks[10], (N_ATTN_LAYERS, DA_QK_DIM), jnp.float32),
        "attn_k_norm": 1.0 + 0.1 * nrm(ks[11], (N_ATTN_LAYERS, DA_QK_DIM), jnp.float32),
        "attn_sub_norm": 1.0 + 0.1 * nrm(ks[12], (N_ATTN_LAYERS, DA_V_DIM), jnp.float32),
        "attn_lambda": 0.1 * nrm(ks[13], (N_ATTN_LAYERS, 4, DA_QK_DIM), jnp.float32),
        "hgrn_w_in": nrm(ks[14], (N_HGRN_LAYERS, D, 5 * HK), jnp.float32) * s,
        "hgrn_w_o": nrm(ks[15], (N_HGRN_LAYERS, HK, D), jnp.float32) * s,
        "hgrn_out_norm": 1.0 + 0.1 * nrm(ks[16], (N_HGRN_LAYERS, HG_DIM), jnp.float32),
        "hgrn_lb_gamma": nrm(ks[17], (2, DEPTH, HK), jnp.float32),
        "router_w": nrm(ks[18], (D, N_EXPERTS), jnp.float32) * s,
        "router_bias": 0.01 * nrm(ks[19], (N_EXPERTS,), jnp.float32),
        "moe_w_gate": nrm(ks[20], (DEPTH, N_EXPERTS, D, D_FF), jnp.float32) * s,
        "moe_w_up": nrm(ks[21], (DEPTH, N_EXPERTS, D, D_FF), jnp.float32) * s,
        "moe_w_down": nrm(ks[22], (DEPTH, N_EXPERTS, D_FF, D), jnp.float32) * (D_FF ** -0.5),
    }


def reference(x, c, ctx, c_ctx, ada_w, ada_b, norm_mix, norm_ffn, attn_w_qkv, attn_w_o,
              attn_q_norm, attn_k_norm, attn_sub_norm, attn_lambda, hgrn_w_in, hgrn_w_o,
              hgrn_out_norm, hgrn_lb_gamma, router_w, router_bias, moe_w_gate, moe_w_up,
              moe_w_down):
    B, T, D = x.shape
    P = jax.nn.softmax(hgrn_lb_gamma.astype(jnp.float32), axis=1)
    cum = jnp.cumsum(P, axis=1)
    lb_all = (cum - cum[:, :1]).reshape(2, DEPTH, HG_HEADS, HG_DIM)

    xc = ctx
    for i in range(DEPTH):
        need_ctx = i < DEPTH - 1
        mod = jax.nn.silu(c) @ ada_w[i] + ada_b[i]
        sh_m, sc_m, g_m, sh_f, sc_f, g_f = jnp.split(mod[:, None, :], 6, axis=-1)
        modc = jax.nn.silu(c_ctx) @ ada_w[i] + ada_b[i]
        shc_m, scc_m, gc_m, shc_f, scc_f, gc_f = jnp.split(modc, 6, axis=-1)

        h = rms_norm(x, norm_mix[i]) * (1.0 + sc_m) + sh_m
        hc = rms_norm(xc, norm_mix[i]) * (1.0 + scc_m) + shc_m
        j = i // N_MIXERS
        if i % N_MIXERS == 0:
            out, out_c = diff_attention_mixer(h, hc, attn_w_qkv[j], attn_w_o[j], attn_q_norm[j],
                                              attn_k_norm[j], attn_sub_norm[j], attn_lambda[j],
                                              i, need_ctx)
        else:
            out, out_c = hgrn2_mixer(h, hc, hgrn_w_in[j], hgrn_w_o[j], hgrn_out_norm[j],
                                     lb_all[:, i], need_ctx)
        x = x + g_m * out

        f = rms_norm(x, norm_ffn[i]) * (1.0 + sc_f) + sh_f
        if need_ctx:
            xc = xc + gc_m * out_c
            fc = rms_norm(xc, norm_ffn[i]) * (1.0 + scc_f) + shc_f
            tokens = jnp.concatenate([f.reshape(-1, D), fc.reshape(-1, D)], axis=0)
            y = moe_ffn(tokens, router_w, router_bias, moe_w_gate[i], moe_w_up[i], moe_w_down[i])
            x = x + g_f * y[:B * T].reshape(B, T, D)
            xc = xc + gc_f * y[B * T:].reshape(xc.shape)
        else:
            y = moe_ffn(f.reshape(-1, D), router_w, router_bias, moe_w_gate[i], moe_w_up[i],
                        moe_w_down[i])
            x = x + g_f * y.reshape(B, T, D)
    return x
```

```python
import functools
import math

import jax
import jax.numpy as jnp
from jax import lax
from jax.experimental import pallas as pl
from jax.experimental.pallas import tpu as pltpu

F32 = jnp.float32
BF16 = jnp.bfloat16
HIGHEST = lax.Precision.HIGHEST

D_MODEL = 1024
DEPTH = 2
GRID_W = 64
DA_HEADS = 8
DA_QK_DIM = 64
DA_V_DIM = 128
ROPE_BASE = 10000.0
HG_HEADS = 8
HG_DIM = 128
N_EXPERTS = 16
N_GROUPS = 4
EXPERTS_PER_GROUP = 4
D_FF = 512
EPS = 1e-6

LANES = 128
TOKEN_BLOCK = 256
MOE_TILE = 256
PAIRS = [(i, j) for i in range(EXPERTS_PER_GROUP) for j in range(i + 1, EXPERTS_PER_GROUP)]
N_BUCKETS = N_GROUPS * len(PAIRS)
SCAN_CHUNK = 64
SCAN_SUB = 16
EXP_CLAMP = 80.0
VMEM_LIMIT = 56 * 1024 * 1024


def _cparams(sem):
    return pltpu.CompilerParams(dimension_semantics=sem, vmem_limit_bytes=VMEM_LIMIT)


def _silu(x):
    return x * jax.nn.sigmoid(x)


def _norm_mod(x, g, sc, sh):
    y = x * lax.rsqrt(jnp.mean(x * x, axis=-1, keepdims=True) + EPS)
    return (y * g) * (1.0 + sc) + sh


def _mod_kernel(c_ref, w_ref, b_ref, o_ref):
    o_ref[0] = jnp.dot(_silu(c_ref[...]), w_ref[0], preferred_element_type=F32,
                       precision=HIGHEST) + b_ref[0]


def _modulation(cvec, ada_w, ada_b):
    R, D = cvec.shape
    depth, _, n6 = ada_w.shape
    tn = 1024
    return pl.pallas_call(
        _mod_kernel,
        grid=(depth, n6 // tn),
        in_specs=[
            pl.BlockSpec((R, D), lambda i, j: (0, 0)),
            pl.BlockSpec((1, D, tn), lambda i, j: (i, 0, j)),
            pl.BlockSpec((1, 1, tn), lambda i, j: (i, 0, j)),
        ],
        out_specs=pl.BlockSpec((1, R, tn), lambda i, j: (i, 0, j)),
        out_shape=jax.ShapeDtypeStruct((depth, R, n6), F32),
        compiler_params=_cparams(("arbitrary", "arbitrary")),
        name="adaln_mod",
    )(cvec, ada_w, ada_b.reshape(depth, 1, n6))


def _mod_spec(col, n_lat, ctx_row):
    return pl.BlockSpec((1, 1, D_MODEL), lambda b, t: (jnp.where(t < n_lat, b, ctx_row), 0, col))


def _attn_proj_kernel(x_ref, g_ref, sc_ref, sh_ref, w_ref, qg_ref, kg_ref, cos_ref, sin_ref,
                      bd_ref, q_ref, k_ref, v_ref):
    D = D_MODEL
    h = _norm_mod(x_ref[0], g_ref[...], sc_ref[0], sh_ref[0])
    qkv = jnp.dot(h.astype(BF16), w_ref[...], preferred_element_type=F32)
    cos, sin, bd = cos_ref[...], sin_ref[...], bd_ref[...]
    lane = lax.broadcasted_iota(jnp.int32, cos.shape, 1)
    upper = (lane & (DA_QK_DIM // 2)) != 0

    def norm_rope(t, gain, scale):
        sq = t * t
        sq_hi = sq.astype(BF16)
        sq_lo = (sq - sq_hi.astype(F32)).astype(BF16)
        ss = (jnp.dot(sq_hi, bd, preferred_element_type=F32)
              + jnp.dot(sq_lo, bd, preferred_element_type=F32))
        tn = t * lax.rsqrt(ss * (1.0 / DA_QK_DIM) + EPS) * gain
        partner = jnp.where(upper, pltpu.roll(tn, DA_QK_DIM // 2, 1),
                            pltpu.roll(tn, LANES - DA_QK_DIM // 2, 1))
        return (tn * cos + partner * sin) * scale

    for j in range(D // LANES):
        sl = slice(j * LANES, (j + 1) * LANES)
        q_ref[0, :, sl] = norm_rope(qkv[:, j * LANES:(j + 1) * LANES], qg_ref[...],
                                    1.0 / math.sqrt(DA_QK_DIM)).astype(BF16)
        k_ref[0, :, sl] = norm_rope(qkv[:, D + j * LANES:D + (j + 1) * LANES], kg_ref[...],
                                    1.0).astype(BF16)
    v_ref[0] = qkv[:, 2 * D:].astype(BF16)


def _attn_project(xa, g, mod3, w_qkv, qg, kg, cos_t, sin_t, bd, n_lat, ctx_row):
    B, NT, D = xa.shape
    TB = TOKEN_BLOCK
    tok = pl.BlockSpec((1, TB, D), lambda b, t: (b, t, 0))
    const2 = lambda shape: pl.BlockSpec(shape, lambda b, t: (0, 0))
    out = jax.ShapeDtypeStruct((B, NT, D), BF16)
    return pl.pallas_call(
        _attn_proj_kernel,
        grid=(B, NT // TB),
        in_specs=[
            tok, const2((1, D)), _mod_spec(1, n_lat, ctx_row), _mod_spec(0, n_lat, ctx_row),
            const2((D, 3 * D)), const2((1, LANES)), const2((1, LANES)),
            pl.BlockSpec((TB, LANES), lambda b, t: (t, 0)),
            pl.BlockSpec((TB, LANES), lambda b, t: (t, 0)),
            const2((LANES, LANES)),
        ],
        out_specs=[tok, tok, tok],
        out_shape=[out, out, out],
        compiler_params=_cparams(("parallel", "arbitrary")),
        name="attn_qkv_proj",
    )(xa, g, mod3, mod3, w_qkv, qg, kg, cos_t, sin_t, bd)


def _attn_kernel(lam_ref, q_ref, k_ref, v_ref, sn_ref, o_ref, *, n_lat, t_lat, lam_init):
    tb = pl.program_id(2)
    lp = lam_ref[...]
    lam = (jnp.exp(jnp.sum(lp[0:1] * lp[1:2], keepdims=True))
           - jnp.exp(jnp.sum(lp[2:3] * lp[3:4], keepdims=True)) + lam_init)
    q = q_ref[0]
    lane = lax.broadcasted_iota(jnp.int32, q.shape, 1)
    zero = jnp.zeros_like(q)
    q0 = jnp.where(lane < DA_QK_DIM, q, zero)
    q1 = jnp.where(lane >= DA_QK_DIM, q, zero)
    nt = (((1,), (1,)), ((), ()))

    def attend(k, v):
        def softmax_parts(qm):
            s = lax.dot_general(qm, k, nt, preferred_element_type=F32)
            e = jnp.exp(s - jnp.max(s, axis=-1, keepdims=True))
            return e, jnp.sum(e, axis=-1, keepdims=True)

        e0, l0 = softmax_parts(q0)
        e1, l1 = softmax_parts(q1)
        a = e0 * (1.0 / l0) - e1 * (lam / l1)
        o = jnp.dot(a.astype(BF16), v, preferred_element_type=F32)
        o = o * lax.rsqrt(jnp.mean(o * o, axis=-1, keepdims=True) + EPS) * sn_ref[...]
        o_ref[0] = (o * (1.0 - lam_init)).astype(BF16)

    @pl.when(tb < n_lat)
    def _():
        attend(k_ref[0], v_ref[0])

    @pl.when(tb >= n_lat)
    def _():
        attend(k_ref[0, t_lat:, :], v_ref[0, t_lat:, :])


def _attention(lam_p, q, k, v, sub_norm, n_lat, t_lat, lam_init):
    B, NT, D = q.shape
    TB = TOKEN_BLOCK
    kv = pl.BlockSpec((1, NT, LANES), lambda b, h, t: (b, 0, h))
    qo = pl.BlockSpec((1, TB, LANES), lambda b, h, t: (b, t, h))
    return pl.pallas_call(
        functools.partial(_attn_kernel, n_lat=n_lat, t_lat=t_lat, lam_init=lam_init),
        grid=(B, DA_HEADS, NT // TB),
        in_specs=[
            pl.BlockSpec(lam_p.shape, lambda b, h, t: (0, 0)),
            qo, kv, kv,
            pl.BlockSpec((1, LANES), lambda b, h, t: (0, 0)),
        ],
        out_specs=qo,
        out_shape=jax.ShapeDtypeStruct((B, NT, D), BF16),
        compiler_params=_cparams(("parallel", "parallel", "arbitrary")),
        name="diff_attention",
    )(lam_p, q, k, v, sub_norm)


def _out_router_kernel(o_ref, x_ref, w_ref, gm_ref, g_ref, sc_ref, sh_ref, rwt_ref, rb_ref,
                       xo_ref, f_ref, bk_ref, wlo_ref, whi_ref):
    out = jnp.dot(o_ref[0], w_ref[...], preferred_element_type=F32)
    x = x_ref[0] + gm_ref[0] * out
    xo_ref[0] = x
    f = _norm_mod(x, g_ref[...], sc_ref[0], sh_ref[0])
    f_ref[0] = f
    logits = lax.dot_general(rwt_ref[...], f, (((1,), (1,)), ((), ())),
                             preferred_element_type=F32, precision=HIGHEST)
    aff = jax.nn.sigmoid(logits)
    biased = aff + rb_ref[...]
    G = EXPERTS_PER_GROUP
    a = [aff[e:e + 1] for e in range(N_EXPERTS)]
    s = [biased[e:e + 1] for e in range(N_EXPERTS)]
    gscore = []
    for g in range(N_GROUPS):
        best = None
        for i, j in PAIRS:
            ps = s[g * G + i] + s[g * G + j]
            best = ps if best is None else jnp.maximum(best, ps)
        gscore.append(best)
    sel = jnp.zeros_like(gscore[0], dtype=jnp.int32)
    best = gscore[0]
    for g in range(1, N_GROUPS):
        better = gscore[g] > best
        sel = jnp.where(better, g, sel)
        best = jnp.where(better, gscore[g], best)
    bucket = jnp.zeros_like(sel)
    a_lo = jnp.zeros_like(best)
    a_hi = jnp.zeros_like(best)
    for g in range(N_GROUPS):
        chosen = []
        for j in range(G):
            rank = jnp.zeros_like(sel)
            for i in range(G):
                if i == j:
                    continue
                si, sj = s[g * G + i], s[g * G + j]
                ahead = (si > sj) | ((si == sj) & (i < j))
                rank = rank + ahead.astype(jnp.int32)
            chosen.append(rank < 2)
        in_g = sel == g
        for p, (i, j) in enumerate(PAIRS):
            pm = chosen[i] & chosen[j] & in_g
            bucket = jnp.where(pm, g * len(PAIRS) + p, bucket)
            a_lo = jnp.where(pm, a[g * G + i], a_lo)
            a_hi = jnp.where(pm, a[g * G + j], a_hi)
    tot = a_lo + a_hi
    bk_ref[0] = bucket
    wlo_ref[0] = a_lo / tot
    whi_ref[0] = a_hi / tot


def _out_router(o, xa, w_o, mod3, g_ffn, rwt, rb, n_lat, ctx_row):
    B, NT, D = xa.shape
    TB = TOKEN_BLOCK
    nb = NT // TB
    tok = pl.BlockSpec((1, TB, D), lambda b, t: (b, t, 0))
    row = pl.BlockSpec((1, 1, TB), lambda b, t: (b * nb + t, 0, 0))
    const2 = lambda shape: pl.BlockSpec(shape, lambda b, t: (0, 0))
    rows = lambda dt: jax.ShapeDtypeStruct((B * nb, 1, TB), dt)
    return pl.pallas_call(
        _out_router_kernel,
        grid=(B, nb),
        in_specs=[
            tok, tok, const2((D, D)), _mod_spec(2, n_lat, ctx_row), const2((1, D)),
            _mod_spec(4, n_lat, ctx_row), _mod_spec(3, n_lat, ctx_row),
            const2((N_EXPERTS, D)), const2((N_EXPERTS, 1)),
        ],
        out_specs=[tok, tok, row, row, row],
        out_shape=[jax.ShapeDtypeStruct((B, NT, D), F32), jax.ShapeDtypeStruct((B, NT, D), F32),
                   rows(jnp.int32), rows(F32), rows(F32)],
        compiler_params=_cparams(("parallel", "arbitrary")),
        name="out_proj_router",
    )(o, xa, w_o, mod3, g_ffn, mod3, mod3, rwt, rb)


def _routing_tables(bucket, n_tokens):
    tm = MOE_TILE
    max_tiles = n_tokens // tm + N_BUCKETS
    onehot = (bucket[:, None] == jnp.arange(N_BUCKETS, dtype=jnp.int32)[None, :]).astype(jnp.int32)
    count = jnp.sum(onehot, axis=0)
    rank = jnp.sum(jnp.cumsum(onehot, axis=0) * onehot, axis=1) - 1
    btiles = (count + tm - 1) // tm
    tile_end = jnp.cumsum(btiles)
    tile_start = tile_end - btiles
    dest = tile_start[bucket] * tm + rank
    n_used = tile_end[-1]
    src = jnp.zeros((max_tiles * tm,), jnp.int32).at[dest].set(jnp.arange(n_tokens, dtype=jnp.int32))
    tile_row = jnp.minimum(jnp.arange(max_tiles, dtype=jnp.int32), n_used - 1)
    tile_bucket = jnp.minimum(jnp.sum((tile_row[:, None] >= tile_end[None, :]).astype(jnp.int32), axis=1),
                              N_BUCKETS - 1)
    grp = tile_bucket // len(PAIRS)
    pair = tile_bucket % len(PAIRS)
    lo_tab = jnp.array([p[0] for p in PAIRS], jnp.int32)
    hi_tab = jnp.array([p[1] for p in PAIRS], jnp.int32)
    tile_lo = grp * EXPERTS_PER_GROUP + lo_tab[pair]
    tile_hi = grp * EXPERTS_PER_GROUP + hi_tab[pair]
    return dest, src, tile_row, tile_lo, tile_hi, n_used.reshape(1).astype(jnp.int32)


def _row_copy(src_hbm, row, dst_vmem, r, sem):
    return pltpu.make_async_copy(src_hbm.at[pl.ds(row, 1), :], dst_vmem.at[pl.ds(r, 1), :], sem)


def _gather_rows(idx_ref, base, src_hbm, dst_vmem, sem, n_rows):
    def issue(r, carry):
        _row_copy(src_hbm, idx_ref[base + r], dst_vmem, r, sem).start()
        return carry

    lax.fori_loop(0, n_rows, issue, 0)

    def drain(r, carry):
        _row_copy(src_hbm, 0, dst_vmem, r, sem).wait()
        return carry

    lax.fori_loop(0, n_rows, drain, 0)


def _moe_gather_kernel(src_ref, f_hbm, o_ref, sem):
    _gather_rows(src_ref, pl.program_id(0) * MOE_TILE, f_hbm, o_ref, sem, MOE_TILE)


def _moe_gather(src, f2d):
    n_rows = src.shape[0]
    D = f2d.shape[1]
    return pl.pallas_call(
        _moe_gather_kernel,
        grid_spec=pltpu.PrefetchScalarGridSpec(
            num_scalar_prefetch=1,
            grid=(n_rows // MOE_TILE,),
            in_specs=[pl.BlockSpec(memory_space=pl.ANY)],
            out_specs=pl.BlockSpec((MOE_TILE, D), lambda i, src: (i, 0)),
            scratch_shapes=[pltpu.SemaphoreType.DMA(())],
        ),
        out_shape=jax.ShapeDtypeStruct((n_rows, D), F32),
        compiler_params=_cparams(("arbitrary",)),
        name="moe_gather",
    )(src, f2d)


def _expert_kernel(trow_ref, tlo_ref, thi_ref, nused_ref, x_ref, wl_ref, wh_ref,
                   gu_lo_ref, dn_lo_ref, gu_hi_ref, dn_hi_ref, o_ref):
    @pl.when(pl.program_id(0) < nused_ref[0])
    def _():
        x = x_ref[...].astype(BF16)

        def ffn(gu_ref, dn_ref):
            gu = jnp.dot(x, gu_ref[0], preferred_element_type=F32)
            hid = _silu(gu[:, :D_FF]) * gu[:, D_FF:]
            return jnp.dot(hid.astype(BF16), dn_ref[0], preferred_element_type=F32)

        o_ref[...] = wl_ref[...] * ffn(gu_lo_ref, dn_lo_ref) + wh_ref[...] * ffn(gu_hi_ref, dn_hi_ref)

    @pl.when(pl.program_id(0) >= nused_ref[0])
    def _():
        o_ref[...] = jnp.zeros_like(o_ref)


def _expert_ffn(tables, xs, wl_s, wh_s, w_gu, w_dn):
    _, _, tile_row, tile_lo, tile_hi, n_used = tables
    P, D = xs.shape
    tm = MOE_TILE
    rows = lambda i, trow, tlo, thi, nu: (trow[i], 0)
    lo3 = lambda i, trow, tlo, thi, nu: (tlo[i], 0, 0)
    hi3 = lambda i, trow, tlo, thi, nu: (thi[i], 0, 0)
    return pl.pallas_call(
        _expert_kernel,
        grid_spec=pltpu.PrefetchScalarGridSpec(
            num_scalar_prefetch=4,
            grid=(P // tm,),
            in_specs=[
                pl.BlockSpec((tm, D), rows), pl.BlockSpec((tm, 1), rows), pl.BlockSpec((tm, 1), rows),
                pl.BlockSpec((1, D, 2 * D_FF), lo3), pl.BlockSpec((1, D_FF, D), lo3),
                pl.BlockSpec((1, D, 2 * D_FF), hi3), pl.BlockSpec((1, D_FF, D), hi3),
            ],
            out_specs=pl.BlockSpec((tm, D), lambda i, trow, tlo, thi, nu: (i, 0)),
        ),
        out_shape=jax.ShapeDtypeStruct((P, D), F32),
        compiler_params=_cparams(("arbitrary",)),
        name="moe_expert_ffn",
    )(tile_row, tile_lo, tile_hi, n_used, xs, wl_s, wh_s, w_gu, w_dn, w_gu, w_dn)


def _combine_kernel(dest_ref, x_ref, gf_ref, y_hbm, o_ref, ybuf, sem, *, blocks_per_row):
    base = (pl.program_id(0) * blocks_per_row + pl.program_id(1)) * TOKEN_BLOCK
    _gather_rows(dest_ref, base, y_hbm, ybuf, sem, TOKEN_BLOCK)
    o_ref[0] = x_ref[0] + gf_ref[0] * ybuf[...]


def _moe_combine(dest, xa, mod3, ys, n_lat, ctx_row, n_blocks_out):
    B, NT, D = xa.shape
    TB = TOKEN_BLOCK
    tok = pl.BlockSpec((1, TB, D), lambda b, t, dest: (b, t, 0))
    gf = pl.BlockSpec((1, 1, D), lambda b, t, dest: (jnp.where(t < n_lat, b, ctx_row), 0, 5))
    return pl.pallas_call(
        functools.partial(_combine_kernel, blocks_per_row=NT // TB),
        grid_spec=pltpu.PrefetchScalarGridSpec(
            num_scalar_prefetch=1,
            grid=(B, n_blocks_out),
            in_specs=[tok, gf, pl.BlockSpec(memory_space=pl.ANY)],
            out_specs=tok,
            scratch_shapes=[pltpu.VMEM((TB, D), F32), pltpu.SemaphoreType.DMA(())],
        ),
        out_shape=jax.ShapeDtypeStruct((B, n_blocks_out * TB, D), F32),
        compiler_params=_cparams(("arbitrary", "arbitrary")),
        name="moe_combine",
    )(dest, xa, mod3, ys)


def _moe_layer(f, bucket, w_lo, w_hi, w_gu, w_dn):
    n_tokens = bucket.shape[0]
    tables = _routing_tables(bucket, n_tokens)
    dest, src = tables[0], tables[1]
    xs = _moe_gather(src, f.reshape(n_tokens, D_MODEL))
    ys = _expert_ffn(tables, xs, w_lo[src][:, None], w_hi[src][:, None], w_gu, w_dn)
    return dest, ys


def _hgrn_proj_kernel(x_ref, g_ref, sc_ref, sh_ref, w_ref, lb_ref,
                      q_ref, v_ref, kf_ref, lff_ref, kb_ref, lfb_ref, gate_ref):
    HK = HG_HEADS * HG_DIM
    h = _norm_mod(x_ref[0], g_ref[...], sc_ref[0], sh_ref[0])
    y = jnp.dot(h.astype(BF16), w_ref[...], preferred_element_type=F32)
    q_ref[0] = _silu(y[:, :HK])
    v_ref[0] = y[:, HK:2 * HK]
    gate_ref[0] = y[:, 4 * HK:]

    def forget(z, lbd, k_ref, lf_ref):
        k_ref[0] = (1.0 - lbd) * jax.nn.sigmoid(-z)
        lf_ref[0] = jnp.logaddexp(jnp.log(lbd), jnp.log1p(-lbd) + jax.nn.log_sigmoid(z))

    forget(y[:, 2 * HK:3 * HK], lb_ref[0:1], kf_ref, lff_ref)
    forget(y[:, 3 * HK:4 * HK], lb_ref[1:2], kb_ref, lfb_ref)


def _hgrn_project(xa, g, mod3, w_in, lb, n_lat, ctx_row):
    B, NT, D = xa.shape
    TB = TOKEN_BLOCK
    HK = HG_HEADS * HG_DIM
    tok = pl.BlockSpec((1, TB, D), lambda b, t: (b, t, 0))
    tok_o = pl.BlockSpec((1, TB, HK), lambda b, t: (b, t, 0))
    const2 = lambda shape: pl.BlockSpec(shape, lambda b, t: (0, 0))
    out = jax.ShapeDtypeStruct((B, NT, HK), F32)
    return pl.pallas_call(
        _hgrn_proj_kernel,
        grid=(B, NT // TB),
        in_specs=[
            tok, const2((1, D)), _mod_spec(1, n_lat, ctx_row), _mod_spec(0, n_lat, ctx_row),
            pl.BlockSpec((D, 5 * HK), lambda b, t: (0, 0), pipeline_mode=pl.Buffered(1)),
            const2((2, HK)),
        ],
        out_specs=[tok_o] * 7,
        out_shape=[out] * 7,
        compiler_params=_cparams(("parallel", "arbitrary")),
        name="hgrn_proj",
    )(xa, g, mod3, mod3, w_in, lb)


def _scan_chunk(st, q, k, v, lf, forward):
    C, SB = SCAN_CHUNK, SCAN_SUB
    r_i = lax.broadcasted_iota(jnp.int32, (C, C), 0)
    c_i = lax.broadcasted_iota(jnp.int32, (C, C), 1)
    tri = ((c_i <= r_i) if forward else (c_i >= r_i)).astype(F32)
    a = jnp.dot(tri, lf, preferred_element_type=F32, precision=HIGHEST)
    nt = (((1,), (1,)), ((), ()))
    o = lax.dot_general((q * jnp.exp(a)).astype(BF16), st.astype(BF16), nt,
                        preferred_element_type=F32)
    parts = []
    for i in range(C // SB):
        lo, hi = i * SB, (i + 1) * SB
        if forward:
            ref = a[lo - 1:lo] if i > 0 else jnp.zeros_like(a[0:1])
            ks = slice(0, hi)
        else:
            ref = a[hi:hi + 1] if hi < C else jnp.zeros_like(a[0:1])
            ks = slice(lo, C)
        qt = q[lo:hi] * jnp.exp(a[lo:hi] - ref)
        kt = k[ks] * jnp.exp(jnp.minimum(ref - a[ks], EXP_CLAMP))
        sc = lax.dot_general(qt.astype(BF16), kt.astype(BF16), nt, preferred_element_type=F32)
        t_idx = lo + lax.broadcasted_iota(jnp.int32, sc.shape, 0)
        s_idx = ks.start + lax.broadcasted_iota(jnp.int32, sc.shape, 1)
        allowed = (s_idx <= t_idx) if forward else (s_idx >= t_idx)
        sc = jnp.where(allowed, sc, 0.0)
        parts.append(jnp.dot(sc.astype(BF16), v[ks].astype(BF16), preferred_element_type=F32))
    o = o + jnp.concatenate(parts, axis=0)
    a_out = a[C - 1:C] if forward else a[0:1]
    kd = k * jnp.exp(a_out - a)
    st_new = st * jnp.exp(a_out) + lax.dot_general(
        v.astype(BF16), kd.astype(BF16), (((0,), (0,)), ((), ())), preferred_element_type=F32)
    return o, st_new


def _hgrn_scan_kernel(q_ref, v_ref, kf_ref, lff_ref, kb_ref, lfb_ref, gate_ref, gain_ref,
                      o_ref, acc_ref, *, n_chunks, n_lat_chunks):
    C = SCAN_CHUNK
    acc_ref[...] = jnp.zeros_like(acc_ref)

    def body(j, carry):
        st_f, st_b = carry
        cf = pl.multiple_of(lax.rem(j + n_lat_chunks, n_chunks) * C, C)
        cb = pl.multiple_of((n_chunks - 1 - j) * C, C)
        rf, rb = pl.ds(cf, C), pl.ds(cb, C)
        o_f, st_f = _scan_chunk(st_f, q_ref[0, rf, :], kf_ref[0, rf, :], v_ref[0, rf, :],
                                lff_ref[0, rf, :], True)
        acc_ref[rf, :] += o_f
        o_b, st_b = _scan_chunk(st_b, q_ref[0, rb, :], kb_ref[0, rb, :], v_ref[0, rb, :],
                                lfb_ref[0, rb, :], False)
        acc_ref[rb, :] += o_b
        return st_f, st_b

    zero = jnp.zeros((HG_DIM, HG_DIM), F32)
    lax.fori_loop(0, n_chunks, body, (zero, zero))
    o = acc_ref[...]
    o = o * lax.rsqrt(jnp.mean(o * o, axis=-1, keepdims=True) + EPS) * gain_ref[...]
    o_ref[0] = (o * _silu(gate_ref[0])).astype(BF16)


def _hgrn_scan(q, v, kf, lff, kb, lfb, gate, gain, t_lat):
    B, NT, HK = q.shape
    blk = pl.BlockSpec((1, NT, HG_DIM), lambda b, h: (b, 0, h))
    return pl.pallas_call(
        functools.partial(_hgrn_scan_kernel, n_chunks=NT // SCAN_CHUNK,
                          n_lat_chunks=t_lat // SCAN_CHUNK),
        grid=(B, HG_HEADS),
        in_specs=[blk] * 7 + [pl.BlockSpec((1, HG_DIM), lambda b, h: (0, 0))],
        out_specs=blk,
        out_shape=jax.ShapeDtypeStruct((B, NT, HK), BF16),
        scratch_shapes=[pltpu.VMEM((NT, HG_DIM), F32)],
        compiler_params=_cparams(("parallel", "arbitrary")),
        name="hgrn_scan",
    )(q, v, kf, lff, kb, lfb, gate, gain)


def _rope_tables(t_lat, t_ctx):
    rows = t_lat // GRID_W
    r = jnp.repeat(jnp.arange(rows, dtype=F32), GRID_W)
    col = jnp.tile(jnp.arange(GRID_W, dtype=F32), rows)
    n_pairs = DA_QK_DIM // 4
    inv = ROPE_BASE ** (-jnp.arange(n_pairs, dtype=F32) / n_pairs)
    ang = jnp.concatenate([r[:, None] * inv, col[:, None] * inv], axis=-1)
    cos, sin = jnp.cos(ang), jnp.sin(ang)
    cos_l = jnp.tile(cos, (1, LANES // cos.shape[1]))
    sin_l = jnp.tile(jnp.concatenate([-sin, sin], axis=-1), (1, LANES // (2 * sin.shape[1])))
    cos_l = jnp.concatenate([cos_l, jnp.ones((t_ctx, LANES), F32)], axis=0)
    sin_l = jnp.concatenate([sin_l, jnp.zeros((t_ctx, LANES), F32)], axis=0)
    return cos_l, sin_l


def kernel(x, c, ctx, c_ctx, ada_w, ada_b, norm_mix, norm_ffn, attn_w_qkv, attn_w_o, attn_q_norm,
           attn_k_norm, attn_sub_norm, attn_lambda, hgrn_w_in, hgrn_w_o, hgrn_out_norm,
           hgrn_lb_gamma, router_w, router_bias, moe_w_gate, moe_w_up, moe_w_down):
    B, T, D = x.shape
    Tc = ctx.shape[1]
    TB = TOKEN_BLOCK
    assert D == D_MODEL and T % TB == 0 and Tc % TB == 0 and T % GRID_W == 0
    assert ada_w.shape[0] == DEPTH == 2
    NT = T + Tc
    n_lat = T // TB
    n_tokens = B * NT
    ctx_row = B

    n_rows = -(-(B + 1) // 8) * 8
    cvec = jnp.concatenate([c, c_ctx[None, :], jnp.zeros((n_rows - B - 1, D), F32)], axis=0)
    mod = _modulation(cvec, ada_w, ada_b)

    xa = jnp.concatenate([x, ctx], axis=1)
    rwt = router_w.T
    rb = router_bias.reshape(N_EXPERTS, 1)

    p = jax.nn.softmax(hgrn_lb_gamma.astype(F32), axis=1)
    cum = jnp.cumsum(p, axis=1)
    lb_all = cum - cum[:, :1]

    cos_t, sin_t = _rope_tables(T, Tc)
    lane = jnp.arange(LANES)
    bd = (lane[:, None] // DA_QK_DIM == lane[None, :] // DA_QK_DIM).astype(BF16)

    for i in range(DEPTH):
        mod3 = mod[i].reshape(n_rows, 1, 6 * D)
        last = i == DEPTH - 1
        j = i // 2
        g_mix = norm_mix[i].reshape(1, D)
        if i % 2 == 0:
            lam_init = 0.8 - 0.6 * math.exp(-0.3 * i)
            q, k, v = _attn_project(
                xa, g_mix, mod3, attn_w_qkv[j].astype(BF16),
                jnp.tile(attn_q_norm[j], LANES // DA_QK_DIM).reshape(1, LANES),
                jnp.tile(attn_k_norm[j], LANES // DA_QK_DIM).reshape(1, LANES),
                cos_t, sin_t, bd, n_lat, ctx_row)
            o = _attention(attn_lambda[j], q, k, v, attn_sub_norm[j].reshape(1, DA_V_DIM),
                           n_lat, T, lam_init)
            w_o = attn_w_o[j]
        else:
            parts = _hgrn_project(xa, g_mix, mod3, hgrn_w_in[j].astype(BF16), lb_all[:, i],
                                  n_lat, ctx_row)
            o = _hgrn_scan(*parts, hgrn_out_norm[j].reshape(1, HG_DIM), T)
            w_o = hgrn_w_o[j]
        xa, f, bucket, w_lo, w_hi = _out_router(
            o, xa, w_o.astype(BF16), mod3, norm_ffn[i].reshape(1, D), rwt, rb, n_lat, ctx_row)
        w_gu = jnp.concatenate([moe_w_gate[i], moe_w_up[i]], axis=-1).astype(BF16)
        dest, ys = _moe_layer(f, bucket.reshape(n_tokens), w_lo.reshape(n_tokens),
                              w_hi.reshape(n_tokens), w_gu, moe_w_down[i].astype(BF16))
        xa = _moe_combine(dest, xa, mod3, ys, n_lat, ctx_row, n_lat if last else NT // TB)
    return xa
```

```python
import functools
import math

import jax
import jax.numpy as jnp
from jax import lax
from jax.experimental import pallas as pl
from jax.experimental.pallas import tpu as pltpu

F32 = jnp.float32
BF16 = jnp.bfloat16
HIGHEST = lax.Precision.HIGHEST

D_MODEL = 1024
DEPTH = 2
GRID_W = 64
DA_HEADS = 8
DA_QK_DIM = 64
DA_V_DIM = 128
ROPE_BASE = 10000.0
HG_HEADS = 8
HG_DIM = 128
N_EXPERTS = 16
N_GROUPS = 4
EXPERTS_PER_GROUP = 4
D_FF = 512
EPS = 1e-6

LANES = 128
TOKEN_BLOCK = 256
MOE_TILE = 256
PAIRS = [(i, j) for i in range(EXPERTS_PER_GROUP) for j in range(i + 1, EXPERTS_PER_GROUP)]
N_BUCKETS = N_GROUPS * len(PAIRS)
SCAN_CHUNK = 64
SCAN_SUB = SCAN_CHUNK // 2
SCAN_GROUP = 4
EXP_CLAMP = 80.0
VMEM_LIMIT = 56 * 1024 * 1024


def _cparams(sem):
    return pltpu.CompilerParams(dimension_semantics=sem, vmem_limit_bytes=VMEM_LIMIT)


def _silu(x):
    return x * jax.nn.sigmoid(x)


def _norm_mod(x, g, sc, sh):
    y = x * lax.rsqrt(jnp.mean(x * x, axis=-1, keepdims=True) + EPS)
    return (y * g) * (1.0 + sc) + sh


def _mod_kernel(c_ref, w_ref, b_ref, o_ref):
    o_ref[0] = jnp.dot(_silu(c_ref[...]), w_ref[0], preferred_element_type=F32,
                       precision=HIGHEST) + b_ref[0]


def _modulation(cvec, ada_w, ada_b):
    R, D = cvec.shape
    depth, _, n6 = ada_w.shape
    tn = 1024
    return pl.pallas_call(
        _mod_kernel,
        grid=(depth, n6 // tn),
        in_specs=[
            pl.BlockSpec((R, D), lambda i, j: (0, 0)),
            pl.BlockSpec((1, D, tn), lambda i, j: (i, 0, j)),
            pl.BlockSpec((1, 1, tn), lambda i, j: (i, 0, j)),
        ],
        out_specs=pl.BlockSpec((1, R, tn), lambda i, j: (i, 0, j)),
        out_shape=jax.ShapeDtypeStruct((depth, R, n6), F32),
        compiler_params=_cparams(("arbitrary", "arbitrary")),
        name="adaln_mod",
    )(cvec, ada_w, ada_b.reshape(depth, 1, n6))


def _mod_spec(col, n_lat, ctx_row):
    return pl.BlockSpec((1, 1, D_MODEL), lambda b, t: (jnp.where(t < n_lat, b, ctx_row), 0, col))


def _attn_proj_kernel(x_ref, g_ref, sc_ref, sh_ref, w_ref, qg_ref, kg_ref, cos_ref, sin_ref,
                      bd_ref, q_ref, k_ref, v_ref):
    D = D_MODEL
    h = _norm_mod(x_ref[0], g_ref[...], sc_ref[0], sh_ref[0])
    qkv = jnp.dot(h.astype(BF16), w_ref[...], preferred_element_type=F32)
    cos, sin, bd = cos_ref[...], sin_ref[...], bd_ref[...]
    lane = lax.broadcasted_iota(jnp.int32, cos.shape, 1)
    upper = (lane & (DA_QK_DIM // 2)) != 0

    def norm_rope(t, gain, scale):
        sq = t * t
        sq_hi = sq.astype(BF16)
        sq_lo = (sq - sq_hi.astype(F32)).astype(BF16)
        ss = (jnp.dot(sq_hi, bd, preferred_element_type=F32)
              + jnp.dot(sq_lo, bd, preferred_element_type=F32))
        tn = t * lax.rsqrt(ss * (1.0 / DA_QK_DIM) + EPS) * gain
        partner = jnp.where(upper, pltpu.roll(tn, DA_QK_DIM // 2, 1),
                            pltpu.roll(tn, LANES - DA_QK_DIM // 2, 1))
        return (tn * cos + partner * sin) * scale

    for j in range(D // LANES):
        sl = slice(j * LANES, (j + 1) * LANES)
        q_ref[0, :, sl] = norm_rope(qkv[:, j * LANES:(j + 1) * LANES], qg_ref[...],
                                    1.0 / math.sqrt(DA_QK_DIM)).astype(BF16)
        k_ref[0, :, sl] = norm_rope(qkv[:, D + j * LANES:D + (j + 1) * LANES], kg_ref[...],
                                    1.0).astype(BF16)
    v_ref[0] = qkv[:, 2 * D:].astype(BF16)


def _attn_project(xa, g, mod3, w_qkv, qg, kg, cos_t, sin_t, bd, n_lat, ctx_row):
    B, NT, D = xa.shape
    TB = TOKEN_BLOCK
    tok = pl.BlockSpec((1, TB, D), lambda b, t: (b, t, 0))
    const2 = lambda shape: pl.BlockSpec(shape, lambda b, t: (0, 0))
    out = jax.ShapeDtypeStruct((B, NT, D), BF16)
    return pl.pallas_call(
        _attn_proj_kernel,
        grid=(B, NT // TB),
        in_specs=[
            tok, const2((1, D)), _mod_spec(1, n_lat, ctx_row), _mod_spec(0, n_lat, ctx_row),
            const2((D, 3 * D)), const2((1, LANES)), const2((1, LANES)),
            pl.BlockSpec((TB, LANES), lambda b, t: (t, 0)),
            pl.BlockSpec((TB, LANES), lambda b, t: (t, 0)),
            const2((LANES, LANES)),
        ],
        out_specs=[tok, tok, tok],
        out_shape=[out, out, out],
        compiler_params=_cparams(("parallel", "arbitrary")),
        name="attn_qkv_proj",
    )(xa, g, mod3, mod3, w_qkv, qg, kg, cos_t, sin_t, bd)


def _attn_kernel(lam_ref, q_ref, k_ref, v_ref, sn_ref, o_ref, *, n_lat, t_lat, lam_init):
    tb = pl.program_id(2)
    lp = lam_ref[...]
    lam = (jnp.exp(jnp.sum(lp[0:1] * lp[1:2], keepdims=True))
           - jnp.exp(jnp.sum(lp[2:3] * lp[3:4], keepdims=True)) + lam_init)
    q = q_ref[0]
    lane = lax.broadcasted_iota(jnp.int32, q.shape, 1)
    zero = jnp.zeros_like(q)
    q0 = jnp.where(lane < DA_QK_DIM, q, zero)
    q1 = jnp.where(lane >= DA_QK_DIM, q, zero)
    nt = (((1,), (1,)), ((), ()))

    def attend(k, v):
        def softmax_parts(qm):
            s = lax.dot_general(qm, k, nt, preferred_element_type=F32)
            e = jnp.exp(s - jnp.max(s, axis=-1, keepdims=True))
            return e, jnp.sum(e, axis=-1, keepdims=True)

        e0, l0 = softmax_parts(q0)
        e1, l1 = softmax_parts(q1)
        a = e0 * (1.0 / l0) - e1 * (lam / l1)
        o = jnp.dot(a.astype(BF16), v, preferred_element_type=F32)
        o = o * lax.rsqrt(jnp.mean(o * o, axis=-1, keepdims=True) + EPS) * sn_ref[...]
        o_ref[0] = (o * (1.0 - lam_init)).astype(BF16)

    @pl.when(tb < n_lat)
    def _():
        attend(k_ref[0], v_ref[0])

    @pl.when(tb >= n_lat)
    def _():
        attend(k_ref[0, t_lat:, :], v_ref[0, t_lat:, :])


def _attention(lam_p, q, k, v, sub_norm, n_lat, t_lat, lam_init):
    B, NT, D = q.shape
    TB = TOKEN_BLOCK
    kv = pl.BlockSpec((1, NT, LANES), lambda b, h, t: (b, 0, h))
    qo = pl.BlockSpec((1, TB, LANES), lambda b, h, t: (b, t, h))
    return pl.pallas_call(
        functools.partial(_attn_kernel, n_lat=n_lat, t_lat=t_lat, lam_init=lam_init),
        grid=(B, DA_HEADS, NT // TB),
        in_specs=[
            pl.BlockSpec(lam_p.shape, lambda b, h, t: (0, 0)),
            qo, kv, kv,
            pl.BlockSpec((1, LANES), lambda b, h, t: (0, 0)),
        ],
        out_specs=qo,
        out_shape=jax.ShapeDtypeStruct((B, NT, D), BF16),
        compiler_params=_cparams(("parallel", "parallel", "arbitrary")),
        name="diff_attention",
    )(lam_p, q, k, v, sub_norm)


def _out_router_kernel(o_ref, x_ref, w_ref, gm_ref, g_ref, sc_ref, sh_ref, rwt_ref, rb_ref,
                       xo_ref, f_ref, bk_ref):
    D = D_MODEL
    out = jnp.dot(o_ref[0], w_ref[...], preferred_element_type=F32)
    x = x_ref[0] + gm_ref[0] * out
    xo_ref[0] = x
    f = _norm_mod(x, g_ref[...], sc_ref[0], sh_ref[0])
    f_ref[0, :, :D] = f
    logits = lax.dot_general(rwt_ref[...], f, (((1,), (1,)), ((), ())),
                             preferred_element_type=F32, precision=HIGHEST)
    aff = jax.nn.sigmoid(logits)
    biased = aff + rb_ref[...]
    G = EXPERTS_PER_GROUP
    a = [aff[e:e + 1] for e in range(N_EXPERTS)]
    s = [biased[e:e + 1] for e in range(N_EXPERTS)]
    gscore = []
    for g in range(N_GROUPS):
        best = None
        for i, j in PAIRS:
            ps = s[g * G + i] + s[g * G + j]
            best = ps if best is None else jnp.maximum(best, ps)
        gscore.append(best)
    sel = jnp.zeros_like(gscore[0], dtype=jnp.int32)
    best = gscore[0]
    for g in range(1, N_GROUPS):
        better = gscore[g] > best
        sel = jnp.where(better, g, sel)
        best = jnp.where(better, gscore[g], best)
    bucket = jnp.zeros_like(sel)
    a_lo = jnp.zeros_like(best)
    a_hi = jnp.zeros_like(best)
    for g in range(N_GROUPS):
        chosen = []
        for j in range(G):
            rank = jnp.zeros_like(sel)
            for i in range(G):
                if i == j:
                    continue
                si, sj = s[g * G + i], s[g * G + j]
                ahead = (si > sj) | ((si == sj) & (i < j))
                rank = rank + ahead.astype(jnp.int32)
            chosen.append(rank < 2)
        in_g = sel == g
        for p, (i, j) in enumerate(PAIRS):
            pm = chosen[i] & chosen[j] & in_g
            bucket = jnp.where(pm, g * len(PAIRS) + p, bucket)
            a_lo = jnp.where(pm, a[g * G + i], a_lo)
            a_hi = jnp.where(pm, a[g * G + j], a_hi)
    tot = a_lo + a_hi
    bk_ref[0] = bucket

    tb = x.shape[0]
    eye = (lax.broadcasted_iota(jnp.int32, (tb, tb), 0)
           == lax.broadcasted_iota(jnp.int32, (tb, tb), 1))

    def column(row):
        return jnp.sum(jnp.where(eye, row, 0.0), axis=1, keepdims=True)

    lane = lax.broadcasted_iota(jnp.int32, (tb, LANES), 1)
    f_ref[0, :, D:] = jnp.where(lane < LANES // 2, column(a_lo / tot), column(a_hi / tot))


def _out_router(o, xa, w_o, mod3, g_ffn, rwt, rb, n_lat, ctx_row):
    B, NT, D = xa.shape
    TB = TOKEN_BLOCK
    nb = NT // TB
    tok = pl.BlockSpec((1, TB, D), lambda b, t: (b, t, 0))
    row = pl.BlockSpec((1, 1, TB), lambda b, t: (b * nb + t, 0, 0))
    const2 = lambda shape: pl.BlockSpec(shape, lambda b, t: (0, 0))
    rows = lambda dt: jax.ShapeDtypeStruct((B * nb, 1, TB), dt)
    return pl.pallas_call(
        _out_router_kernel,
        grid=(B, nb),
        in_specs=[
            tok, tok, const2((D, D)), _mod_spec(2, n_lat, ctx_row), const2((1, D)),
            _mod_spec(4, n_lat, ctx_row), _mod_spec(3, n_lat, ctx_row),
            const2((N_EXPERTS, D)), const2((N_EXPERTS, 1)),
        ],
        out_specs=[tok, pl.BlockSpec((1, TB, D + LANES), lambda b, t: (b, t, 0)), row],
        out_shape=[jax.ShapeDtypeStruct((B, NT, D), F32),
                   jax.ShapeDtypeStruct((B, NT, D + LANES), F32), rows(jnp.int32)],
        compiler_params=_cparams(("parallel", "arbitrary")),
        name="out_proj_router",
    )(o, xa, w_o, mod3, g_ffn, mod3, mod3, rwt, rb)


def _routing_tables(bucket, n_tokens):
    tm = MOE_TILE
    max_tiles = n_tokens // tm + N_BUCKETS
    n_rows = max_tiles * tm
    onehot = (bucket[:, None] == jnp.arange(N_BUCKETS, dtype=jnp.int32)[None, :]).astype(jnp.int32)
    count = jnp.sum(onehot, axis=0)
    rank = jnp.sum(jnp.cumsum(onehot, axis=0) * onehot, axis=1) - 1
    btiles = (count + tm - 1) // tm
    tile_end = jnp.cumsum(btiles)
    tile_start = tile_end - btiles
    dest = tile_start[bucket] * tm + rank
    token_at = jnp.full((n_rows,), -1, jnp.int32).at[dest].set(jnp.arange(n_tokens, dtype=jnp.int32))
    is_pad = token_at < 0
    spare = n_tokens + jnp.cumsum(is_pad.astype(jnp.int32)) - 1
    gather_idx = jnp.where(is_pad, 0, token_at)
    scatter_idx = jnp.where(is_pad, spare, token_at)
    tile = jnp.arange(max_tiles, dtype=jnp.int32)
    tile_bucket = jnp.minimum(jnp.sum((tile[:, None] >= tile_end[None, :]).astype(jnp.int32), axis=1),
                              N_BUCKETS - 1)
    grp = tile_bucket // len(PAIRS)
    pair = tile_bucket % len(PAIRS)
    lo_tab = jnp.array([p[0] for p in PAIRS], jnp.int32)
    hi_tab = jnp.array([p[1] for p in PAIRS], jnp.int32)
    tile_lo = grp * EXPERTS_PER_GROUP + lo_tab[pair]
    tile_hi = grp * EXPERTS_PER_GROUP + hi_tab[pair]
    return gather_idx, scatter_idx, tile_lo, tile_hi


def _expert_kernel(gidx_ref, sidx_ref, tlo_ref, thi_ref, f_hbm, gu_lo_ref, dn_lo_ref, gu_hi_ref,
                   dn_hi_ref, y_hbm, xbuf, ybuf, gsem, ssem):
    TM, D = MOE_TILE, D_MODEL
    i = pl.program_id(0)
    n = pl.num_programs(0)
    slot = lax.rem(i, 2)
    other = 1 - slot

    def start_gather(tile, s):
        for r in range(TM):
            pltpu.make_async_copy(f_hbm.at[pl.ds(gidx_ref[tile * TM + r], 1), :],
                                  xbuf.at[s, pl.ds(r, 1), :], gsem.at[s]).start()

    def start_scatter(tile, s):
        for r in range(TM):
            pltpu.make_async_copy(ybuf.at[s, pl.ds(r, 1), :],
                                  y_hbm.at[pl.ds(sidx_ref[tile * TM + r], 1), :], ssem.at[s]).start()

    def wait_gather(s):
        pltpu.make_async_copy(f_hbm.at[pl.ds(0, TM), :], xbuf.at[s], gsem.at[s]).wait()

    def wait_scatter(s):
        pltpu.make_async_copy(ybuf.at[s], y_hbm.at[pl.ds(0, TM), :], ssem.at[s]).wait()

    @pl.when(i == 0)
    def _():
        start_gather(0, 0)

    wait_gather(slot)

    @pl.when(i >= 2)
    def _():
        wait_scatter(slot)

    start_gather(jnp.minimum(i + 1, n - 1), other)

    xe = xbuf[slot]
    x = xe[:, :D].astype(BF16)

    def ffn(gu_ref, dn_ref):
        gu = jnp.dot(x, gu_ref[0], preferred_element_type=F32)
        hid = _silu(gu[:, :D_FF]) * gu[:, D_FF:]
        return jnp.dot(hid.astype(BF16), dn_ref[0], preferred_element_type=F32)

    ybuf[slot] = (xe[:, D:D + 1] * ffn(gu_lo_ref, dn_lo_ref)
                  + xe[:, D + LANES // 2:D + LANES // 2 + 1] * ffn(gu_hi_ref, dn_hi_ref))
    start_scatter(i, slot)

    @pl.when(i == n - 1)
    def _():
        wait_gather(other)
        wait_scatter(other)
        wait_scatter(slot)


def _expert_ffn(tables, f_ext, w_gu, w_dn):
    gather_idx, scatter_idx, tile_lo, tile_hi = tables
    P = gather_idx.shape[0]
    D = D_MODEL
    tm = MOE_TILE
    lo3 = lambda i, gi, si, tlo, thi: (tlo[i], 0, 0)
    hi3 = lambda i, gi, si, tlo, thi: (thi[i], 0, 0)
    return pl.pallas_call(
        _expert_kernel,
        grid_spec=pltpu.PrefetchScalarGridSpec(
            num_scalar_prefetch=4,
            grid=(P // tm,),
            in_specs=[
                pl.BlockSpec(memory_space=pl.ANY),
                pl.BlockSpec((1, D, 2 * D_FF), lo3), pl.BlockSpec((1, D_FF, D), lo3),
                pl.BlockSpec((1, D, 2 * D_FF), hi3), pl.BlockSpec((1, D_FF, D), hi3),
            ],
            out_specs=pl.BlockSpec(memory_space=pl.ANY),
            scratch_shapes=[pltpu.VMEM((2, tm, D + LANES), F32), pltpu.VMEM((2, tm, D), F32),
                            pltpu.SemaphoreType.DMA((2,)), pltpu.SemaphoreType.DMA((2,))],
        ),
        out_shape=jax.ShapeDtypeStruct((P, D), F32),
        compiler_params=_cparams(("arbitrary",)),
        name="moe_expert_ffn",
    )(gather_idx, scatter_idx, tile_lo, tile_hi, f_ext, w_gu, w_dn, w_gu, w_dn)


def _combine_kernel(x_ref, gf_ref, y_ref, o_ref):
    o_ref[0] = x_ref[0] + gf_ref[0] * y_ref[...]


def _moe_combine(xa, mod3, y, n_lat, ctx_row, n_blocks_out):
    B, NT, D = xa.shape
    TB = TOKEN_BLOCK
    nb = NT // TB
    tok = pl.BlockSpec((1, TB, D), lambda b, t: (b, t, 0))
    return pl.pallas_call(
        _combine_kernel,
        grid=(B, n_blocks_out),
        in_specs=[tok, _mod_spec(5, n_lat, ctx_row),
                  pl.BlockSpec((TB, D), lambda b, t: (b * nb + t, 0))],
        out_specs=tok,
        out_shape=jax.ShapeDtypeStruct((B, n_blocks_out * TB, D), F32),
        compiler_params=_cparams(("parallel", "arbitrary")),
        name="moe_combine",
    )(xa, mod3, y)


def _hgrn_proj_kernel(x_ref, g_ref, sc_ref, sh_ref, w_ref, lb_ref,
                      q_ref, v_ref, kf_ref, lff_ref, kb_ref, lfb_ref, gate_ref):
    HK = HG_HEADS * HG_DIM
    h = _norm_mod(x_ref[0], g_ref[...], sc_ref[0], sh_ref[0])
    y = jnp.dot(h.astype(BF16), w_ref[...], preferred_element_type=F32)
    q_ref[0] = _silu(y[:, :HK])
    v_ref[0] = y[:, HK:2 * HK]
    gate_ref[0] = y[:, 4 * HK:]

    def forget(z, lbd, k_ref, lf_ref):
        k_ref[0] = (1.0 - lbd) * jax.nn.sigmoid(-z)
        lf_ref[0] = jnp.logaddexp(jnp.log(lbd), jnp.log1p(-lbd) + jax.nn.log_sigmoid(z))

    forget(y[:, 2 * HK:3 * HK], lb_ref[0:1], kf_ref, lff_ref)
    forget(y[:, 3 * HK:4 * HK], lb_ref[1:2], kb_ref, lfb_ref)


def _hgrn_project(xa, g, mod3, w_in, lb, n_lat, ctx_row):
    B, NT, D = xa.shape
    TB = TOKEN_BLOCK
    HK = HG_HEADS * HG_DIM
    tok = pl.BlockSpec((1, TB, D), lambda b, t: (b, t, 0))
    tok_o = pl.BlockSpec((1, TB, HK), lambda b, t: (b, t, 0))
    const2 = lambda shape: pl.BlockSpec(shape, lambda b, t: (0, 0))
    out = jax.ShapeDtypeStruct((B, NT, HK), F32)
    return pl.pallas_call(
        _hgrn_proj_kernel,
        grid=(B, NT // TB),
        in_specs=[
            tok, const2((1, D)), _mod_spec(1, n_lat, ctx_row), _mod_spec(0, n_lat, ctx_row),
            pl.BlockSpec((D, 5 * HK), lambda b, t: (0, 0), pipeline_mode=pl.Buffered(1)),
            const2((2, HK)),
        ],
        out_specs=[tok_o] * 7,
        out_shape=[out] * 7,
        compiler_params=_cparams(("parallel", "arbitrary")),
        name="hgrn_proj",
    )(xa, g, mod3, mod3, w_in, lb)


def _split3(x):
    hi = x.astype(BF16)
    r = x - hi.astype(F32)
    mid = r.astype(BF16)
    return hi, mid, (r - mid.astype(F32)).astype(BF16)


def _scan_chunk_local(q, k, vb, a, forward):
    C, SB = SCAN_CHUNK, SCAN_SUB
    row = lax.broadcasted_iota(jnp.int32, a.shape, 0)
    first = row < SB
    mid_row = SB // 2
    m = jnp.where(first, a[mid_row:mid_row + 1], a[SB + mid_row:SB + mid_row + 1])
    qd = q * jnp.exp(jnp.minimum(a - m, EXP_CLAMP))
    kd = k * jnp.exp(jnp.minimum(m - a, EXP_CLAMP))
    edge = a[SB - 1:SB] if forward else a[SB:SB + 1]
    e_x = jnp.exp(-jnp.abs(a - edge))
    nt = (((1,), (1,)), ((), ()))
    s_d = lax.dot_general(qd.astype(BF16), kd.astype(BF16), nt, preferred_element_type=F32)
    s_x = lax.dot_general((q * e_x).astype(BF16), (k * e_x).astype(BF16), nt,
                          preferred_element_type=F32)
    t_i = lax.broadcasted_iota(jnp.int32, s_d.shape, 0)
    s_i = lax.broadcasted_iota(jnp.int32, s_d.shape, 1)
    same = (t_i < SB) == (s_i < SB)
    if forward:
        causal, cross = s_i <= t_i, (t_i >= SB) & (s_i < SB)
    else:
        causal, cross = s_i >= t_i, (t_i < SB) & (s_i >= SB)
    scores = jnp.where(same & causal, s_d, jnp.where(cross, s_x, 0.0))
    o_intra = jnp.dot(scores.astype(BF16), vb, preferred_element_type=F32)
    a_out = a[C - 1:C] if forward else a[0:1]
    k_out = k * jnp.exp(a_out - a)
    upd = lax.dot_general(vb, k_out.astype(BF16), (((0,), (0,)), ((), ())),
                          preferred_element_type=F32)
    return o_intra, (q * jnp.exp(a)).astype(BF16), upd, jnp.exp(a_out)


def _hgrn_scan_kernel(q_ref, v_ref, kf_ref, lff_ref, kb_ref, lfb_ref, gate_ref, gain_ref,
                      o_ref, acc_ref, qe_ref, upd_ref, dec_ref, *, n_chunks, n_lat_chunks):
    C, G = SCAN_CHUNK, SCAN_GROUP
    R = C * G
    r_i = lax.broadcasted_iota(jnp.int32, (R, R), 0)
    c_i = lax.broadcasted_iota(jnp.int32, (R, R), 1)
    same_chunk = (r_i // C) == (c_i // C)
    tri_f = (same_chunk & (c_i <= r_i)).astype(BF16)
    tri_b = (same_chunk & (c_i >= r_i)).astype(BF16)

    def cumsum(tri, lf):
        return sum(jnp.dot(tri, part, preferred_element_type=F32) for part in _split3(lf))

    def local(g, carry):
        rows = pl.ds(pl.multiple_of(g * R, R), R)
        q = q_ref[0, rows, :]
        vb = v_ref[0, rows, :].astype(BF16)
        kf, kb = kf_ref[0, rows, :], kb_ref[0, rows, :]
        a_f = cumsum(tri_f, lff_ref[0, rows, :])
        a_b = cumsum(tri_b, lfb_ref[0, rows, :])
        res = []
        for ci in range(G):
            sl = slice(ci * C, (ci + 1) * C)
            res.append((_scan_chunk_local(q[sl], kf[sl], vb[sl], a_f[sl], True),
                        _scan_chunk_local(q[sl], kb[sl], vb[sl], a_b[sl], False)))
        for ci, (fw, bw) in enumerate(res):
            c = g * G + ci
            crow = pl.ds(pl.multiple_of(c * C, C), C)
            acc_ref[crow, :] = fw[0] + bw[0]
            qe_ref[0, crow, :] = fw[1]
            qe_ref[1, crow, :] = bw[1]
            upd_ref[0, c] = fw[2]
            upd_ref[1, c] = bw[2]
            dec_ref[0, pl.ds(c, 1), :] = fw[3]
            dec_ref[1, pl.ds(c, 1), :] = bw[3]
        return carry

    lax.fori_loop(0, n_chunks // G, local, 0)

    nt = (((1,), (1,)), ((), ()))

    def carry_state(j, carry):
        st_f, st_b = carry
        cf = lax.rem(j + n_lat_chunks, n_chunks)
        cb = n_chunks - 1 - j
        rf = pl.ds(pl.multiple_of(cf * C, C), C)
        rb = pl.ds(pl.multiple_of(cb * C, C), C)
        acc_ref[rf, :] += lax.dot_general(qe_ref[0, rf, :], st_f.astype(BF16), nt,
                                          preferred_element_type=F32)
        acc_ref[rb, :] += lax.dot_general(qe_ref[1, rb, :], st_b.astype(BF16), nt,
                                          preferred_element_type=F32)
        st_f = st_f * dec_ref[0, pl.ds(cf, 1), :] + upd_ref[0, cf]
        st_b = st_b * dec_ref[1, pl.ds(cb, 1), :] + upd_ref[1, cb]
        return st_f, st_b

    zero = jnp.zeros((HG_DIM, HG_DIM), F32)
    lax.fori_loop(0, n_chunks, carry_state, (zero, zero), unroll=2)
    o = acc_ref[...]
    o = o * lax.rsqrt(jnp.mean(o * o, axis=-1, keepdims=True) + EPS) * gain_ref[...]
    o_ref[0] = (o * _silu(gate_ref[0])).astype(BF16)


def _hgrn_scan(q, v, kf, lff, kb, lfb, gate, gain, t_lat):
    B, NT, HK = q.shape
    blk = pl.BlockSpec((1, NT, HG_DIM), lambda b, h: (b, 0, h))
    return pl.pallas_call(
        functools.partial(_hgrn_scan_kernel, n_chunks=NT // SCAN_CHUNK,
                          n_lat_chunks=t_lat // SCAN_CHUNK),
        grid=(B, HG_HEADS),
        in_specs=[blk] * 7 + [pl.BlockSpec((1, HG_DIM), lambda b, h: (0, 0))],
        out_specs=blk,
        out_shape=jax.ShapeDtypeStruct((B, NT, HK), BF16),
        scratch_shapes=[pltpu.VMEM((NT, HG_DIM), F32),
                        pltpu.VMEM((2, NT, HG_DIM), BF16),
                        pltpu.VMEM((2, NT // SCAN_CHUNK, HG_DIM, HG_DIM), F32),
                        pltpu.VMEM((2, NT // SCAN_CHUNK, HG_DIM), F32)],
        compiler_params=_cparams(("parallel", "arbitrary")),
        name="hgrn_scan",
    )(q, v, kf, lff, kb, lfb, gate, gain)


def _rope_tables(t_lat, t_ctx):
    rows = t_lat // GRID_W
    r = jnp.repeat(jnp.arange(rows, dtype=F32), GRID_W)
    col = jnp.tile(jnp.arange(GRID_W, dtype=F32), rows)
    n_pairs = DA_QK_DIM // 4
    inv = ROPE_BASE ** (-jnp.arange(n_pairs, dtype=F32) / n_pairs)
    ang = jnp.concatenate([r[:, None] * inv, col[:, None] * inv], axis=-1)
    cos, sin = jnp.cos(ang), jnp.sin(ang)
    cos_l = jnp.tile(cos, (1, LANES // cos.shape[1]))
    sin_l = jnp.tile(jnp.concatenate([-sin, sin], axis=-1), (1, LANES // (2 * sin.shape[1])))
    cos_l = jnp.concatenate([cos_l, jnp.ones((t_ctx, LANES), F32)], axis=0)
    sin_l = jnp.concatenate([sin_l, jnp.zeros((t_ctx, LANES), F32)], axis=0)
    return cos_l, sin_l


def kernel(x, c, ctx, c_ctx, ada_w, ada_b, norm_mix, norm_ffn, attn_w_qkv, attn_w_o, attn_q_norm,
           attn_k_norm, attn_sub_norm, attn_lambda, hgrn_w_in, hgrn_w_o, hgrn_out_norm,
           hgrn_lb_gamma, router_w, router_bias, moe_w_gate, moe_w_up, moe_w_down):
    B, T, D = x.shape
    Tc = ctx.shape[1]
    TB = TOKEN_BLOCK
    assert D == D_MODEL and T % TB == 0 and Tc % TB == 0 and T % GRID_W == 0
    assert ada_w.shape[0] == DEPTH == 2
    assert (T + Tc) % (SCAN_CHUNK * SCAN_GROUP) == 0 and T % SCAN_CHUNK == 0
    NT = T + Tc
    n_lat = T // TB
    n_tokens = B * NT
    ctx_row = B

    n_rows = -(-(B + 1) // 8) * 8
    cvec = jnp.concatenate([c, c_ctx[None, :], jnp.zeros((n_rows - B - 1, D), F32)], axis=0)
    mod = _modulation(cvec, ada_w, ada_b)

    xa = jnp.concatenate([x, ctx], axis=1)
    rwt = router_w.T
    rb = router_bias.reshape(N_EXPERTS, 1)

    p = jax.nn.softmax(hgrn_lb_gamma.astype(F32), axis=1)
    cum = jnp.cumsum(p, axis=1)
    lb_all = cum - cum[:, :1]

    cos_t, sin_t = _rope_tables(T, Tc)
    lane = jnp.arange(LANES)
    bd = (lane[:, None] // DA_QK_DIM == lane[None, :] // DA_QK_DIM).astype(BF16)

    for i in range(DEPTH):
        mod3 = mod[i].reshape(n_rows, 1, 6 * D)
        last = i == DEPTH - 1
        j = i // 2
        g_mix = norm_mix[i].reshape(1, D)
        if i % 2 == 0:
            lam_init = 0.8 - 0.6 * math.exp(-0.3 * i)
            q, k, v = _attn_project(
                xa, g_mix, mod3, attn_w_qkv[j].astype(BF16),
                jnp.tile(attn_q_norm[j], LANES // DA_QK_DIM).reshape(1, LANES),
                jnp.tile(attn_k_norm[j], LANES // DA_QK_DIM).reshape(1, LANES),
                cos_t, sin_t, bd, n_lat, ctx_row)
            o = _attention(attn_lambda[j], q, k, v, attn_sub_norm[j].reshape(1, DA_V_DIM),
                           n_lat, T, lam_init)
            w_o = attn_w_o[j]
        else:
            parts = _hgrn_project(xa, g_mix, mod3, hgrn_w_in[j].astype(BF16), lb_all[:, i],
                                  n_lat, ctx_row)
            o = _hgrn_scan(*parts, hgrn_out_norm[j].reshape(1, HG_DIM), T)
            w_o = hgrn_w_o[j]
        xa, f_ext, bucket = _out_router(
            o, xa, w_o.astype(BF16), mod3, norm_ffn[i].reshape(1, D), rwt, rb, n_lat, ctx_row)
        w_gu = jnp.concatenate([moe_w_gate[i], moe_w_up[i]], axis=-1).astype(BF16)
        tables = _routing_tables(bucket.reshape(n_tokens), n_tokens)
        y = _expert_ffn(tables, f_ext.reshape(n_tokens, D + LANES), w_gu, moe_w_down[i].astype(BF16))
        xa = _moe_combine(xa, mod3, y, n_lat, ctx_row, n_lat if last else NT // TB)
    return xa
```

```python
import functools
import math

import jax
import jax.numpy as jnp
from jax import lax
from jax.experimental import pallas as pl
from jax.experimental.pallas import tpu as pltpu

F32 = jnp.float32
BF16 = jnp.bfloat16
HIGHEST = lax.Precision.HIGHEST

D_MODEL = 1024
DEPTH = 2
GRID_W = 64
DA_HEADS = 8
DA_QK_DIM = 64
DA_V_DIM = 128
ROPE_BASE = 10000.0
HG_HEADS = 8
HG_DIM = 128
N_EXPERTS = 16
N_GROUPS = 4
EXPERTS_PER_GROUP = 4
D_FF = 512
EPS = 1e-6

LANES = 128
TOKEN_BLOCK = 256
ATTN_Q_BLOCK = 256
ATTN_GROUP = 4
MOE_TILE = 256
PAIRS = [(i, j) for i in range(EXPERTS_PER_GROUP) for j in range(i + 1, EXPERTS_PER_GROUP)]
N_BUCKETS = N_GROUPS * len(PAIRS)
SCAN_CHUNK = 64
SCAN_SUB = SCAN_CHUNK // 2
SCAN_GROUP = 4
EXP_CLAMP = 80.0
VMEM_LIMIT = 56 * 1024 * 1024


def _cparams(sem):
    return pltpu.CompilerParams(dimension_semantics=sem, vmem_limit_bytes=VMEM_LIMIT)


def _silu(x):
    return x * jax.nn.sigmoid(x)


def _norm_mod(x, g, sc, sh):
    y = x * lax.rsqrt(jnp.mean(x * x, axis=-1, keepdims=True) + EPS)
    return (y * g) * (1.0 + sc) + sh


def _mod_kernel(c_ref, w_ref, b_ref, o_ref):
    o_ref[0] = jnp.dot(_silu(c_ref[...]), w_ref[0], preferred_element_type=F32,
                       precision=HIGHEST) + b_ref[0]


def _modulation(cvec, ada_w, ada_b):
    R, D = cvec.shape
    depth, _, n6 = ada_w.shape
    tn = 1024
    return pl.pallas_call(
        _mod_kernel,
        grid=(depth, n6 // tn),
        in_specs=[
            pl.BlockSpec((R, D), lambda i, j: (0, 0)),
            pl.BlockSpec((1, D, tn), lambda i, j: (i, 0, j)),
            pl.BlockSpec((1, 1, tn), lambda i, j: (i, 0, j)),
        ],
        out_specs=pl.BlockSpec((1, R, tn), lambda i, j: (i, 0, j)),
        out_shape=jax.ShapeDtypeStruct((depth, R, n6), F32),
        compiler_params=_cparams(("arbitrary", "arbitrary")),
        name="adaln_mod",
    )(cvec, ada_w, ada_b.reshape(depth, 1, n6))


def _mod_spec(col, n_lat, ctx_row):
    return pl.BlockSpec((1, 1, D_MODEL), lambda b, t: (jnp.where(t < n_lat, b, ctx_row), 0, col))


def _attn_proj_kernel(x_ref, g_ref, sc_ref, sh_ref, w_ref, qg_ref, kg_ref, cos_ref, sin_ref,
                      bd_ref, q_ref, k_ref, v_ref):
    D = D_MODEL
    h = _norm_mod(x_ref[0], g_ref[...], sc_ref[0], sh_ref[0])
    qkv = jnp.dot(h.astype(BF16), w_ref[...], preferred_element_type=F32)
    cos, sin, bd = cos_ref[...], sin_ref[...], bd_ref[...]
    lane = lax.broadcasted_iota(jnp.int32, cos.shape, 1)
    upper = (lane & (DA_QK_DIM // 2)) != 0

    def norm_rope(t, gain, scale):
        sq = t * t
        sq_hi = sq.astype(BF16)
        sq_lo = (sq - sq_hi.astype(F32)).astype(BF16)
        ss = (jnp.dot(sq_hi, bd, preferred_element_type=F32)
              + jnp.dot(sq_lo, bd, preferred_element_type=F32))
        tn = t * lax.rsqrt(ss * (1.0 / DA_QK_DIM) + EPS) * gain
        partner = jnp.where(upper, pltpu.roll(tn, DA_QK_DIM // 2, 1),
                            pltpu.roll(tn, LANES - DA_QK_DIM // 2, 1))
        return (tn * cos + partner * sin) * scale

    for j in range(D // LANES):
        sl = slice(j * LANES, (j + 1) * LANES)
        q_ref[0, :, sl] = norm_rope(qkv[:, j * LANES:(j + 1) * LANES], qg_ref[...],
                                    math.log2(math.e) / math.sqrt(DA_QK_DIM)).astype(BF16)
        k_ref[0, j] = norm_rope(qkv[:, D + j * LANES:D + (j + 1) * LANES], kg_ref[...],
                                1.0).T.astype(BF16)
    v_ref[0] = qkv[:, 2 * D:].astype(BF16)


def _attn_project(xa, g, mod3, w_qkv, qg, kg, cos_t, sin_t, bd, n_lat, ctx_row):
    B, NT, D = xa.shape
    TB = TOKEN_BLOCK
    tok = pl.BlockSpec((1, TB, D), lambda b, t: (b, t, 0))
    const2 = lambda shape: pl.BlockSpec(shape, lambda b, t: (0, 0))
    out = jax.ShapeDtypeStruct((B, NT, D), BF16)
    return pl.pallas_call(
        _attn_proj_kernel,
        grid=(B, NT // TB),
        in_specs=[
            tok, const2((1, D)), _mod_spec(1, n_lat, ctx_row), _mod_spec(0, n_lat, ctx_row),
            const2((D, 3 * D)), const2((1, LANES)), const2((1, LANES)),
            pl.BlockSpec((TB, LANES), lambda b, t: (t, 0)),
            pl.BlockSpec((TB, LANES), lambda b, t: (t, 0)),
            const2((LANES, LANES)),
        ],
        out_specs=[tok, pl.BlockSpec((1, DA_HEADS, LANES, TB), lambda b, t: (b, 0, 0, t)), tok],
        out_shape=[out, jax.ShapeDtypeStruct((B, DA_HEADS, LANES, NT), BF16), out],
        compiler_params=_cparams(("parallel", "arbitrary")),
        name="attn_qkv_proj",
    )(xa, g, mod3, mod3, w_qkv, qg, kg, cos_t, sin_t, bd)


def _attn_kernel(lam_ref, q_ref, kt_ref, v_ref, sn_ref, o_ref, *, t_lat, lam_init):
    QB = ATTN_Q_BLOCK
    lp = lam_ref[...]
    lam = (jnp.exp(jnp.sum(lp[0:1] * lp[1:2], keepdims=True))
           - jnp.exp(jnp.sum(lp[2:3] * lp[3:4], keepdims=True)) + lam_init)

    def scores(q, kt):
        lane = lax.broadcasted_iota(jnp.int32, q.shape, 1)
        zero = jnp.zeros_like(q)
        return (jnp.dot(jnp.where(lane < DA_QK_DIM, q, zero), kt, preferred_element_type=F32),
                jnp.dot(jnp.where(lane >= DA_QK_DIM, q, zero), kt, preferred_element_type=F32))

    def finish(s, v):
        def softmax_parts(sm):
            e = jnp.exp2(sm - jnp.max(sm, axis=-1, keepdims=True))
            return e, jnp.sum(e, axis=-1, keepdims=True)

        e0, l0 = softmax_parts(s[0])
        e1, l1 = softmax_parts(s[1])
        a = e0 - (lam * l0 / l1) * e1
        o = jnp.dot(a.astype(BF16), v, preferred_element_type=F32) * (1.0 / l0)
        o = o * lax.rsqrt(jnp.mean(o * o, axis=-1, keepdims=True) + EPS) * sn_ref[...]
        return (o * (1.0 - lam_init)).astype(BF16)

    G = math.gcd(ATTN_GROUP, t_lat // QB)

    def latent_group(i, carry):
        rows = [pl.ds(pl.multiple_of((G * i + j) * QB, QB), QB) for j in range(G)]
        s_next = scores(q_ref[0, rows[0], :], kt_ref[0, 0])
        for j in range(G):
            s_cur = s_next
            if j + 1 < G:
                s_next = scores(q_ref[0, rows[j + 1], :], kt_ref[0, 0])
            o_ref[0, rows[j], :] = finish(s_cur, v_ref[0])
        return carry

    lax.fori_loop(0, t_lat // (G * QB), latent_group, 0)
    o_ref[0, t_lat:, :] = finish(scores(q_ref[0, t_lat:, :], kt_ref[0, 0, :, t_lat:]),
                                 v_ref[0, t_lat:, :])


def _attention(lam_p, q, kt, v, sub_norm, t_lat, lam_init):
    B, NT, D = q.shape
    blk = pl.BlockSpec((1, NT, LANES), lambda b, h: (b, 0, h))
    return pl.pallas_call(
        functools.partial(_attn_kernel, t_lat=t_lat, lam_init=lam_init),
        grid=(B, DA_HEADS),
        in_specs=[
            pl.BlockSpec(lam_p.shape, lambda b, h: (0, 0)),
            blk, pl.BlockSpec((1, 1, LANES, NT), lambda b, h: (b, h, 0, 0)), blk,
            pl.BlockSpec((1, LANES), lambda b, h: (0, 0)),
        ],
        out_specs=blk,
        out_shape=jax.ShapeDtypeStruct((B, NT, D), BF16),
        compiler_params=_cparams(("parallel", "arbitrary")),
        name="diff_attention",
    )(lam_p, q, kt, v, sub_norm)


def _out_router_kernel(o_ref, x_ref, w_ref, gm_ref, g_ref, sc_ref, sh_ref, rwt_ref, rb_ref,
                       xo_ref, f_ref, bk_ref):
    D = D_MODEL
    out = jnp.dot(o_ref[0], w_ref[...], preferred_element_type=F32)
    x = x_ref[0] + gm_ref[0] * out
    xo_ref[0] = x
    f = _norm_mod(x, g_ref[...], sc_ref[0], sh_ref[0])
    f_ref[0, :, :D] = f
    logits = lax.dot_general(rwt_ref[...], f, (((1,), (1,)), ((), ())),
                             preferred_element_type=F32, precision=HIGHEST)
    aff = jax.nn.sigmoid(logits)
    biased = aff + rb_ref[...]
    G = EXPERTS_PER_GROUP
    a = [aff[e:e + 1] for e in range(N_EXPERTS)]
    s = [biased[e:e + 1] for e in range(N_EXPERTS)]
    gscore = []
    for g in range(N_GROUPS):
        best = None
        for i, j in PAIRS:
            ps = s[g * G + i] + s[g * G + j]
            best = ps if best is None else jnp.maximum(best, ps)
        gscore.append(best)
    sel = jnp.zeros_like(gscore[0], dtype=jnp.int32)
    best = gscore[0]
    for g in range(1, N_GROUPS):
        better = gscore[g] > best
        sel = jnp.where(better, g, sel)
        best = jnp.where(better, gscore[g], best)
    bucket = jnp.zeros_like(sel)
    a_lo = jnp.zeros_like(best)
    a_hi = jnp.zeros_like(best)
    for g in range(N_GROUPS):
        chosen = []
        for j in range(G):
            rank = jnp.zeros_like(sel)
            for i in range(G):
                if i == j:
                    continue
                si, sj = s[g * G + i], s[g * G + j]
                ahead = (si > sj) | ((si == sj) & (i < j))
                rank = rank + ahead.astype(jnp.int32)
            chosen.append(rank < 2)
        in_g = sel == g
        for p, (i, j) in enumerate(PAIRS):
            pm = chosen[i] & chosen[j] & in_g
            bucket = jnp.where(pm, g * len(PAIRS) + p, bucket)
            a_lo = jnp.where(pm, a[g * G + i], a_lo)
            a_hi = jnp.where(pm, a[g * G + j], a_hi)
    tot = a_lo + a_hi
    bk_ref[0] = bucket

    tb = x.shape[0]
    eye = (lax.broadcasted_iota(jnp.int32, (tb, tb), 0)
           == lax.broadcasted_iota(jnp.int32, (tb, tb), 1))

    def column(row):
        return jnp.sum(jnp.where(eye, row, 0.0), axis=1, keepdims=True)

    lane = lax.broadcasted_iota(jnp.int32, (tb, LANES), 1)
    f_ref[0, :, D:] = jnp.where(lane < LANES // 2, column(a_lo / tot), column(a_hi / tot))


def _out_router(o, xa, w_o, mod3, g_ffn, rwt, rb, n_lat, ctx_row):
    B, NT, D = xa.shape
    TB = TOKEN_BLOCK
    nb = NT // TB
    tok = pl.BlockSpec((1, TB, D), lambda b, t: (b, t, 0))
    row = pl.BlockSpec((1, 1, TB), lambda b, t: (b * nb + t, 0, 0))
    const2 = lambda shape: pl.BlockSpec(shape, lambda b, t: (0, 0))
    rows = lambda dt: jax.ShapeDtypeStruct((B * nb, 1, TB), dt)
    return pl.pallas_call(
        _out_router_kernel,
        grid=(B, nb),
        in_specs=[
            tok, tok, const2((D, D)), _mod_spec(2, n_lat, ctx_row), const2((1, D)),
            _mod_spec(4, n_lat, ctx_row), _mod_spec(3, n_lat, ctx_row),
            const2((N_EXPERTS, D)), const2((N_EXPERTS, 1)),
        ],
        out_specs=[tok, pl.BlockSpec((1, TB, D + LANES), lambda b, t: (b, t, 0)), row],
        out_shape=[jax.ShapeDtypeStruct((B, NT, D), F32),
                   jax.ShapeDtypeStruct((B, NT, D + LANES), F32), rows(jnp.int32)],
        compiler_params=_cparams(("parallel", "arbitrary")),
        name="out_proj_router",
    )(o, xa, w_o, mod3, g_ffn, mod3, mod3, rwt, rb)


def _routing_tables(bucket, n_tokens):
    tm = MOE_TILE
    max_tiles = n_tokens // tm + N_BUCKETS
    n_rows = max_tiles * tm
    onehot = (bucket[:, None] == jnp.arange(N_BUCKETS, dtype=jnp.int32)[None, :]).astype(jnp.int32)
    count = jnp.sum(onehot, axis=0)
    rank = jnp.sum(jnp.cumsum(onehot, axis=0) * onehot, axis=1) - 1
    btiles = (count + tm - 1) // tm
    tile_end = jnp.cumsum(btiles)
    tile_start = tile_end - btiles
    dest = tile_start[bucket] * tm + rank
    token_at = jnp.full((n_rows,), -1, jnp.int32).at[dest].set(jnp.arange(n_tokens, dtype=jnp.int32))
    is_pad = token_at < 0
    spare = n_tokens + jnp.cumsum(is_pad.astype(jnp.int32)) - 1
    gather_idx = jnp.where(is_pad, 0, token_at)
    scatter_idx = jnp.where(is_pad, spare, token_at)
    tile = jnp.arange(max_tiles, dtype=jnp.int32)
    tile_bucket = jnp.minimum(jnp.sum((tile[:, None] >= tile_end[None, :]).astype(jnp.int32), axis=1),
                              N_BUCKETS - 1)
    grp = tile_bucket // len(PAIRS)
    pair = tile_bucket % len(PAIRS)
    lo_tab = jnp.array([p[0] for p in PAIRS], jnp.int32)
    hi_tab = jnp.array([p[1] for p in PAIRS], jnp.int32)
    tile_lo = grp * EXPERTS_PER_GROUP + lo_tab[pair]
    tile_hi = grp * EXPERTS_PER_GROUP + hi_tab[pair]
    return gather_idx, scatter_idx, tile_lo, tile_hi


def _expert_kernel(gidx_ref, sidx_ref, tlo_ref, thi_ref, f_hbm, gu_lo_ref, dn_lo_ref, gu_hi_ref,
                   dn_hi_ref, y_hbm, xbuf, ybuf, gsem, ssem):
    TM, D = MOE_TILE, D_MODEL
    i = pl.program_id(0)
    n = pl.num_programs(0)
    slot = lax.rem(i, 2)
    other = 1 - slot

    def start_gather(tile, s):
        for r in range(TM):
            pltpu.make_async_copy(f_hbm.at[pl.ds(gidx_ref[tile * TM + r], 1), :],
                                  xbuf.at[s, pl.ds(r, 1), :], gsem.at[s]).start(priority=r % 2)

    def start_scatter(tile, s):
        for r in range(TM):
            pltpu.make_async_copy(ybuf.at[s, pl.ds(r, 1), :],
                                  y_hbm.at[pl.ds(sidx_ref[tile * TM + r], 1), :],
                                  ssem.at[s]).start(priority=r % 2)

    def wait_gather(s):
        pltpu.make_async_copy(f_hbm.at[pl.ds(0, TM), :], xbuf.at[s], gsem.at[s]).wait()

    def wait_scatter(s):
        pltpu.make_async_copy(ybuf.at[s], y_hbm.at[pl.ds(0, TM), :], ssem.at[s]).wait()

    @pl.when(i == 0)
    def _():
        start_gather(0, 0)

    wait_gather(slot)

    @pl.when(i >= 2)
    def _():
        wait_scatter(slot)

    start_gather(jnp.minimum(i + 1, n - 1), other)

    xe = xbuf[slot]
    x = xe[:, :D].astype(BF16)

    def ffn(gu_ref, dn_ref):
        gu = jnp.dot(x, gu_ref[0], preferred_element_type=F32)
        hid = _silu(gu[:, :D_FF]) * gu[:, D_FF:]
        return jnp.dot(hid.astype(BF16), dn_ref[0], preferred_element_type=F32)

    ybuf[slot] = (xe[:, D:D + 1] * ffn(gu_lo_ref, dn_lo_ref)
                  + xe[:, D + LANES // 2:D + LANES // 2 + 1] * ffn(gu_hi_ref, dn_hi_ref))
    start_scatter(i, slot)

    @pl.when(i == n - 1)
    def _():
        wait_gather(other)
        wait_scatter(other)
        wait_scatter(slot)


def _expert_ffn(tables, f_ext, w_gu, w_dn):
    gather_idx, scatter_idx, tile_lo, tile_hi = tables
    P = gather_idx.shape[0]
    D = D_MODEL
    tm = MOE_TILE
    lo3 = lambda i, gi, si, tlo, thi: (tlo[i], 0, 0)
    hi3 = lambda i, gi, si, tlo, thi: (thi[i], 0, 0)
    return pl.pallas_call(
        _expert_kernel,
        grid_spec=pltpu.PrefetchScalarGridSpec(
            num_scalar_prefetch=4,
            grid=(P // tm,),
            in_specs=[
                pl.BlockSpec(memory_space=pl.ANY),
                pl.BlockSpec((1, D, 2 * D_FF), lo3), pl.BlockSpec((1, D_FF, D), lo3),
                pl.BlockSpec((1, D, 2 * D_FF), hi3), pl.BlockSpec((1, D_FF, D), hi3),
            ],
            out_specs=pl.BlockSpec(memory_space=pl.ANY),
            scratch_shapes=[pltpu.VMEM((2, tm, D + LANES), F32), pltpu.VMEM((2, tm, D), F32),
                            pltpu.SemaphoreType.DMA((2,)), pltpu.SemaphoreType.DMA((2,))],
        ),
        out_shape=jax.ShapeDtypeStruct((P, D), F32),
        compiler_params=_cparams(("arbitrary",)),
        name="moe_expert_ffn",
    )(gather_idx, scatter_idx, tile_lo, tile_hi, f_ext, w_gu, w_dn, w_gu, w_dn)


def _combine_kernel(x_ref, gf_ref, y_ref, o_ref):
    o_ref[0] = x_ref[0] + gf_ref[0] * y_ref[...]


def _moe_combine(xa, mod3, y, n_lat, ctx_row, n_blocks_out):
    B, NT, D = xa.shape
    TB = TOKEN_BLOCK
    nb = NT // TB
    tok = pl.BlockSpec((1, TB, D), lambda b, t: (b, t, 0))
    return pl.pallas_call(
        _combine_kernel,
        grid=(B, n_blocks_out),
        in_specs=[tok, _mod_spec(5, n_lat, ctx_row),
                  pl.BlockSpec((TB, D), lambda b, t: (b * nb + t, 0))],
        out_specs=tok,
        out_shape=jax.ShapeDtypeStruct((B, n_blocks_out * TB, D), F32),
        compiler_params=_cparams(("parallel", "arbitrary")),
        name="moe_combine",
    )(xa, mod3, y)


def _hgrn_proj_kernel(x_ref, g_ref, sc_ref, sh_ref, w_ref, lb_ref,
                      q_ref, v_ref, kf_ref, lff_ref, kb_ref, lfb_ref, gate_ref):
    HK = HG_HEADS * HG_DIM
    h = _norm_mod(x_ref[0], g_ref[...], sc_ref[0], sh_ref[0])
    y = jnp.dot(h.astype(BF16), w_ref[...], preferred_element_type=F32)
    q_ref[0] = _silu(y[:, :HK])
    v_ref[0] = y[:, HK:2 * HK]
    gate_ref[0] = y[:, 4 * HK:]

    def forget(z, lbd, k_ref, lf_ref):
        e = jnp.exp(-jnp.abs(z))
        t = 1.0 + e
        k_ref[0] = (1.0 - lbd) * (jnp.where(z >= 0.0, e, 1.0) / t)
        a = jnp.log(lbd)
        b = jnp.log1p(-lbd) + (jnp.minimum(z, 0.0) - jnp.log(t))
        lf_ref[0] = jnp.maximum(a, b) + jnp.log(1.0 + jnp.exp(-jnp.abs(a - b)))

    forget(y[:, 2 * HK:3 * HK], lb_ref[0:1], kf_ref, lff_ref)
    forget(y[:, 3 * HK:4 * HK], lb_ref[1:2], kb_ref, lfb_ref)


def _hgrn_project(xa, g, mod3, w_in, lb, n_lat, ctx_row):
    B, NT, D = xa.shape
    TB = TOKEN_BLOCK
    HK = HG_HEADS * HG_DIM
    tok = pl.BlockSpec((1, TB, D), lambda b, t: (b, t, 0))
    tok_o = pl.BlockSpec((1, TB, HK), lambda b, t: (b, t, 0))
    const2 = lambda shape: pl.BlockSpec(shape, lambda b, t: (0, 0))
    out = jax.ShapeDtypeStruct((B, NT, HK), F32)
    return pl.pallas_call(
        _hgrn_proj_kernel,
        grid=(B, NT // TB),
        in_specs=[
            tok, const2((1, D)), _mod_spec(1, n_lat, ctx_row), _mod_spec(0, n_lat, ctx_row),
            pl.BlockSpec((D, 5 * HK), lambda b, t: (0, 0), pipeline_mode=pl.Buffered(1)),
            const2((2, HK)),
        ],
        out_specs=[tok_o] * 7,
        out_shape=[out] * 7,
        compiler_params=_cparams(("parallel", "arbitrary")),
        name="hgrn_proj",
    )(xa, g, mod3, mod3, w_in, lb)


def _split3(x):
    hi = x.astype(BF16)
    r = x - hi.astype(F32)
    mid = r.astype(BF16)
    return hi, mid, (r - mid.astype(F32)).astype(BF16)


def _scan_chunk_local(q, k, vb, a, forward):
    C, SB = SCAN_CHUNK, SCAN_SUB
    row = lax.broadcasted_iota(jnp.int32, a.shape, 0)
    first = row < SB
    mid_row = SB // 2
    m = jnp.where(first, a[mid_row:mid_row + 1], a[SB + mid_row:SB + mid_row + 1])
    qd = q * jnp.exp(jnp.minimum(a - m, EXP_CLAMP))
    kd = k * jnp.exp(jnp.minimum(m - a, EXP_CLAMP))
    edge = a[SB - 1:SB] if forward else a[SB:SB + 1]
    e_x = jnp.exp(-jnp.abs(a - edge))
    nt = (((1,), (1,)), ((), ()))
    s_d = lax.dot_general(qd.astype(BF16), kd.astype(BF16), nt, preferred_element_type=F32)
    s_x = lax.dot_general((q * e_x).astype(BF16), (k * e_x).astype(BF16), nt,
                          preferred_element_type=F32)
    t_i = lax.broadcasted_iota(jnp.int32, s_d.shape, 0)
    s_i = lax.broadcasted_iota(jnp.int32, s_d.shape, 1)
    same = (t_i < SB) == (s_i < SB)
    if forward:
        causal, cross = s_i <= t_i, (t_i >= SB) & (s_i < SB)
    else:
        causal, cross = s_i >= t_i, (t_i < SB) & (s_i >= SB)
    scores = jnp.where(same & causal, s_d, jnp.where(cross, s_x, 0.0))
    o_intra = jnp.dot(scores.astype(BF16), vb, preferred_element_type=F32)
    a_out = a[C - 1:C] if forward else a[0:1]
    k_out = k * jnp.exp(a_out - a)
    upd = lax.dot_general(vb, k_out.astype(BF16), (((0,), (0,)), ((), ())),
                          preferred_element_type=F32)
    return o_intra, (q * jnp.exp(a)).astype(BF16), upd, jnp.exp(a_out)


def _hgrn_scan_kernel(q_ref, v_ref, kf_ref, lff_ref, kb_ref, lfb_ref, gate_ref, gain_ref,
                      o_ref, acc_ref, qe_ref, upd_ref, dec_ref, *, n_chunks, n_lat_chunks):
    C, G = SCAN_CHUNK, SCAN_GROUP
    R = C * G
    r_i = lax.broadcasted_iota(jnp.int32, (R, R), 0)
    c_i = lax.broadcasted_iota(jnp.int32, (R, R), 1)
    same_chunk = (r_i // C) == (c_i // C)
    tri_f = (same_chunk & (c_i <= r_i)).astype(BF16)
    tri_b = (same_chunk & (c_i >= r_i)).astype(BF16)

    def cumsum(tri, lf):
        return sum(jnp.dot(tri, part, preferred_element_type=F32) for part in _split3(lf))

    def local(g, carry):
        rows = pl.ds(pl.multiple_of(g * R, R), R)
        q = q_ref[0, rows, :]
        vb = v_ref[0, rows, :].astype(BF16)
        kf, kb = kf_ref[0, rows, :], kb_ref[0, rows, :]
        a_f = cumsum(tri_f, lff_ref[0, rows, :])
        a_b = cumsum(tri_b, lfb_ref[0, rows, :])
        res = []
        for ci in range(G):
            sl = slice(ci * C, (ci + 1) * C)
            res.append((_scan_chunk_local(q[sl], kf[sl], vb[sl], a_f[sl], True),
                        _scan_chunk_local(q[sl], kb[sl], vb[sl], a_b[sl], False)))
        for ci, (fw, bw) in enumerate(res):
            c = g * G + ci
            crow = pl.ds(pl.multiple_of(c * C, C), C)
            acc_ref[crow, :] = fw[0] + bw[0]
            qe_ref[0, crow, :] = fw[1]
            qe_ref[1, crow, :] = bw[1]
            upd_ref[0, c] = fw[2]
            upd_ref[1, c] = bw[2]
            dec_ref[0, pl.ds(c, 1), :] = fw[3]
            dec_ref[1, pl.ds(c, 1), :] = bw[3]
        return carry

    lax.fori_loop(0, n_chunks // G, local, 0)

    nt = (((1,), (1,)), ((), ()))

    def carry_state(j, carry):
        st_f, st_b = carry
        cf = lax.rem(j + n_lat_chunks, n_chunks)
        cb = n_chunks - 1 - j
        rf = pl.ds(pl.multiple_of(cf * C, C), C)
        rb = pl.ds(pl.multiple_of(cb * C, C), C)
        acc_ref[rf, :] += lax.dot_general(qe_ref[0, rf, :], st_f.astype(BF16), nt,
                                          preferred_element_type=F32)
        acc_ref[rb, :] += lax.dot_general(qe_ref[1, rb, :], st_b.astype(BF16), nt,
                                          preferred_element_type=F32)
        st_f = st_f * dec_ref[0, pl.ds(cf, 1), :] + upd_ref[0, cf]
        st_b = st_b * dec_ref[1, pl.ds(cb, 1), :] + upd_ref[1, cb]
        return st_f, st_b

    zero = jnp.zeros((HG_DIM, HG_DIM), F32)
    lax.fori_loop(0, n_chunks, carry_state, (zero, zero), unroll=2)
    o = acc_ref[...]
    o = o * lax.rsqrt(jnp.mean(o * o, axis=-1, keepdims=True) + EPS) * gain_ref[...]
    o_ref[0] = (o * _silu(gate_ref[0])).astype(BF16)


def _hgrn_scan(q, v, kf, lff, kb, lfb, gate, gain, t_lat):
    B, NT, HK = q.shape
    blk = pl.BlockSpec((1, NT, HG_DIM), lambda b, h: (b, 0, h))
    return pl.pallas_call(
        functools.partial(_hgrn_scan_kernel, n_chunks=NT // SCAN_CHUNK,
                          n_lat_chunks=t_lat // SCAN_CHUNK),
        grid=(B, HG_HEADS),
        in_specs=[blk] * 7 + [pl.BlockSpec((1, HG_DIM), lambda b, h: (0, 0))],
        out_specs=blk,
        out_shape=jax.ShapeDtypeStruct((B, NT, HK), BF16),
        scratch_shapes=[pltpu.VMEM((NT, HG_DIM), F32),
                        pltpu.VMEM((2, NT, HG_DIM), BF16),
                        pltpu.VMEM((2, NT // SCAN_CHUNK, HG_DIM, HG_DIM), F32),
                        pltpu.VMEM((2, NT // SCAN_CHUNK, HG_DIM), F32)],
        compiler_params=_cparams(("parallel", "arbitrary")),
        name="hgrn_scan",
    )(q, v, kf, lff, kb, lfb, gate, gain)


def _rope_tables(t_lat, t_ctx):
    rows = t_lat // GRID_W
    r = jnp.repeat(jnp.arange(rows, dtype=F32), GRID_W)
    col = jnp.tile(jnp.arange(GRID_W, dtype=F32), rows)
    n_pairs = DA_QK_DIM // 4
    inv = ROPE_BASE ** (-jnp.arange(n_pairs, dtype=F32) / n_pairs)
    ang = jnp.concatenate([r[:, None] * inv, col[:, None] * inv], axis=-1)
    cos, sin = jnp.cos(ang), jnp.sin(ang)
    cos_l = jnp.tile(cos, (1, LANES // cos.shape[1]))
    sin_l = jnp.tile(jnp.concatenate([-sin, sin], axis=-1), (1, LANES // (2 * sin.shape[1])))
    cos_l = jnp.concatenate([cos_l, jnp.ones((t_ctx, LANES), F32)], axis=0)
    sin_l = jnp.concatenate([sin_l, jnp.zeros((t_ctx, LANES), F32)], axis=0)
    return cos_l, sin_l


def kernel(x, c, ctx, c_ctx, ada_w, ada_b, norm_mix, norm_ffn, attn_w_qkv, attn_w_o, attn_q_norm,
           attn_k_norm, attn_sub_norm, attn_lambda, hgrn_w_in, hgrn_w_o, hgrn_out_norm,
           hgrn_lb_gamma, router_w, router_bias, moe_w_gate, moe_w_up, moe_w_down):
    B, T, D = x.shape
    Tc = ctx.shape[1]
    TB = TOKEN_BLOCK
    assert D == D_MODEL and T % TB == 0 and Tc % TB == 0 and T % GRID_W == 0
    assert ada_w.shape[0] == DEPTH == 2
    assert (T + Tc) % (SCAN_CHUNK * SCAN_GROUP) == 0 and T % SCAN_CHUNK == 0
    assert T % ATTN_Q_BLOCK == 0
    NT = T + Tc
    n_lat = T // TB
    n_tokens = B * NT
    ctx_row = B

    n_rows = -(-(B + 1) // 8) * 8
    cvec = jnp.concatenate([c, c_ctx[None, :], jnp.zeros((n_rows - B - 1, D), F32)], axis=0)
    mod = _modulation(cvec, ada_w, ada_b)

    xa = jnp.concatenate([x, ctx], axis=1)
    rwt = router_w.T
    rb = router_bias.reshape(N_EXPERTS, 1)

    p = jax.nn.softmax(hgrn_lb_gamma.astype(F32), axis=1)
    cum = jnp.cumsum(p, axis=1)
    lb_all = cum - cum[:, :1]

    cos_t, sin_t = _rope_tables(T, Tc)
    lane = jnp.arange(LANES)
    bd = (lane[:, None] // DA_QK_DIM == lane[None, :] // DA_QK_DIM).astype(BF16)

    for i in range(DEPTH):
        mod3 = mod[i].reshape(n_rows, 1, 6 * D)
        last = i == DEPTH - 1
        j = i // 2
        g_mix = norm_mix[i].reshape(1, D)
        if i % 2 == 0:
            lam_init = 0.8 - 0.6 * math.exp(-0.3 * i)
            q, k, v = _attn_project(
                xa, g_mix, mod3, attn_w_qkv[j].astype(BF16),
                jnp.tile(attn_q_norm[j], LANES // DA_QK_DIM).reshape(1, LANES),
                jnp.tile(attn_k_norm[j], LANES // DA_QK_DIM).reshape(1, LANES),
                cos_t, sin_t, bd, n_lat, ctx_row)
            o = _attention(attn_lambda[j], q, k, v, attn_sub_norm[j].reshape(1, DA_V_DIM),
                           T, lam_init)
            w_o = attn_w_o[j]
        else:
            parts = _hgrn_project(xa, g_mix, mod3, hgrn_w_in[j].astype(BF16), lb_all[:, i],
                                  n_lat, ctx_row)
            o = _hgrn_scan(*parts, hgrn_out_norm[j].reshape(1, HG_DIM), T)
            w_o = hgrn_w_o[j]
        xa, f_ext, bucket = _out_router(
            o, xa, w_o.astype(BF16), mod3, norm_ffn[i].reshape(1, D), rwt, rb, n_lat, ctx_row)
        w_gu = jnp.concatenate([moe_w_gate[i], moe_w_up[i]], axis=-1).astype(BF16)
        tables = _routing_tables(bucket.reshape(n_tokens), n_tokens)
        y = _expert_ffn(tables, f_ext.reshape(n_tokens, D + LANES), w_gu, moe_w_down[i].astype(BF16))
        xa = _moe_combine(xa, mod3, y, n_lat, ctx_row, n_lat if last else NT // TB)
    return xa
```

```python
import functools
import math

import jax
import jax.numpy as jnp
from jax import lax
from jax.experimental import pallas as pl
from jax.experimental.pallas import tpu as pltpu

F32 = jnp.float32
BF16 = jnp.bfloat16
HIGHEST = lax.Precision.HIGHEST

D_MODEL = 1024
DEPTH = 2
GRID_W = 64
DA_HEADS = 8
DA_QK_DIM = 64
DA_V_DIM = 128
ROPE_BASE = 10000.0
HG_HEADS = 8
HG_DIM = 128
N_EXPERTS = 16
N_GROUPS = 4
EXPERTS_PER_GROUP = 4
D_FF = 512
EPS = 1e-6

LANES = 128
ROW_TILES = D_MODEL // LANES
TOKEN_BLOCK = 256
ATTN_Q_BLOCK = 256
ATTN_GROUP = 4
MOE_TILE = 256
PAIRS = [(i, j) for i in range(EXPERTS_PER_GROUP) for j in range(i + 1, EXPERTS_PER_GROUP)]
N_BUCKETS = N_GROUPS * len(PAIRS)
SCAN_CHUNK = 64
SCAN_SUB = SCAN_CHUNK // 2
SCAN_GROUP = 4
EXP_CLAMP = 80.0
VMEM_LIMIT = 56 * 1024 * 1024


def _cparams(sem):
    return pltpu.CompilerParams(dimension_semantics=sem, vmem_limit_bytes=VMEM_LIMIT)


def _silu(x):
    return x * jax.nn.sigmoid(x)


def _norm_mod(x, g, sc, sh):
    y = x * lax.rsqrt(jnp.mean(x * x, axis=-1, keepdims=True) + EPS)
    return (y * g) * (1.0 + sc) + sh


def _mod_kernel(c_ref, w_ref, b_ref, o_ref):
    o_ref[0] = jnp.dot(_silu(c_ref[...]), w_ref[0], preferred_element_type=F32,
                       precision=HIGHEST) + b_ref[0]


def _modulation(cvec, ada_w, ada_b):
    R, D = cvec.shape
    depth, _, n6 = ada_w.shape
    tn = 1024
    return pl.pallas_call(
        _mod_kernel,
        grid=(depth, n6 // tn),
        in_specs=[
            pl.BlockSpec((R, D), lambda i, j: (0, 0)),
            pl.BlockSpec((1, D, tn), lambda i, j: (i, 0, j)),
            pl.BlockSpec((1, 1, tn), lambda i, j: (i, 0, j)),
        ],
        out_specs=pl.BlockSpec((1, R, tn), lambda i, j: (i, 0, j)),
        out_shape=jax.ShapeDtypeStruct((depth, R, n6), F32),
        compiler_params=_cparams(("arbitrary", "arbitrary")),
        name="adaln_mod",
    )(cvec, ada_w, ada_b.reshape(depth, 1, n6))


def _mod_spec(col, n_lat, ctx_row):
    return pl.BlockSpec((1, 1, D_MODEL), lambda b, t: (jnp.where(t < n_lat, b, ctx_row), 0, col))


def _attn_proj_kernel(x_ref, g_ref, sc_ref, sh_ref, w_ref, qg_ref, kg_ref, cos_ref, sin_ref,
                      bd_ref, q_ref, k_ref, v_ref):
    D = D_MODEL
    h = _norm_mod(x_ref[0], g_ref[...], sc_ref[0], sh_ref[0])
    qkv = jnp.dot(h.astype(BF16), w_ref[...], preferred_element_type=F32)
    cos, sin, bd = cos_ref[...], sin_ref[...], bd_ref[...]
    lane = lax.broadcasted_iota(jnp.int32, cos.shape, 1)
    upper = (lane & (DA_QK_DIM // 2)) != 0

    def norm_rope(t, gain, scale):
        sq = t * t
        sq_hi = sq.astype(BF16)
        sq_lo = (sq - sq_hi.astype(F32)).astype(BF16)
        ss = (jnp.dot(sq_hi, bd, preferred_element_type=F32)
              + jnp.dot(sq_lo, bd, preferred_element_type=F32))
        tn = t * lax.rsqrt(ss * (1.0 / DA_QK_DIM) + EPS) * gain
        partner = jnp.where(upper, pltpu.roll(tn, DA_QK_DIM // 2, 1),
                            pltpu.roll(tn, LANES - DA_QK_DIM // 2, 1))
        return (tn * cos + partner * sin) * scale

    for j in range(D // LANES):
        sl = slice(j * LANES, (j + 1) * LANES)
        q_ref[0, :, sl] = norm_rope(qkv[:, j * LANES:(j + 1) * LANES], qg_ref[...],
                                    math.log2(math.e) / math.sqrt(DA_QK_DIM)).astype(BF16)
        k_ref[0, j] = norm_rope(qkv[:, D + j * LANES:D + (j + 1) * LANES], kg_ref[...],
                                1.0).T.astype(BF16)
    v_ref[0] = qkv[:, 2 * D:].astype(BF16)


def _attn_project(xa, g, mod3, w_qkv, qg, kg, cos_t, sin_t, bd, n_lat, ctx_row):
    B, NT, D = xa.shape
    TB = TOKEN_BLOCK
    tok = pl.BlockSpec((1, TB, D), lambda b, t: (b, t, 0))
    const2 = lambda shape: pl.BlockSpec(shape, lambda b, t: (0, 0))
    out = jax.ShapeDtypeStruct((B, NT, D), BF16)
    return pl.pallas_call(
        _attn_proj_kernel,
        grid=(B, NT // TB),
        in_specs=[
            tok, const2((1, D)), _mod_spec(1, n_lat, ctx_row), _mod_spec(0, n_lat, ctx_row),
            const2((D, 3 * D)), const2((1, LANES)), const2((1, LANES)),
            pl.BlockSpec((TB, LANES), lambda b, t: (t, 0)),
            pl.BlockSpec((TB, LANES), lambda b, t: (t, 0)),
            const2((LANES, LANES)),
        ],
        out_specs=[tok, pl.BlockSpec((1, DA_HEADS, LANES, TB), lambda b, t: (b, 0, 0, t)), tok],
        out_shape=[out, jax.ShapeDtypeStruct((B, DA_HEADS, LANES, NT), BF16), out],
        compiler_params=_cparams(("parallel", "arbitrary")),
        name="attn_qkv_proj",
    )(xa, g, mod3, mod3, w_qkv, qg, kg, cos_t, sin_t, bd)


def _attn_kernel(lam_ref, q_ref, kt_ref, v_ref, sn_ref, o_ref, *, t_lat, lam_init):
    QB = ATTN_Q_BLOCK
    lp = lam_ref[...]
    lam = (jnp.exp(jnp.sum(lp[0:1] * lp[1:2], keepdims=True))
           - jnp.exp(jnp.sum(lp[2:3] * lp[3:4], keepdims=True)) + lam_init)

    def scores(q, kt):
        lane = lax.broadcasted_iota(jnp.int32, q.shape, 1)
        zero = jnp.zeros_like(q)
        return (jnp.dot(jnp.where(lane < DA_QK_DIM, q, zero), kt, preferred_element_type=F32),
                jnp.dot(jnp.where(lane >= DA_QK_DIM, q, zero), kt, preferred_element_type=F32))

    def finish(s, v):
        def softmax_parts(sm):
            e = jnp.exp2(sm - jnp.max(sm, axis=-1, keepdims=True))
            return e, jnp.sum(e, axis=-1, keepdims=True)

        e0, l0 = softmax_parts(s[0])
        e1, l1 = softmax_parts(s[1])
        a = e0 - (lam * l0 / l1) * e1
        o = jnp.dot(a.astype(BF16), v, preferred_element_type=F32) * (1.0 / l0)
        o = o * lax.rsqrt(jnp.mean(o * o, axis=-1, keepdims=True) + EPS) * sn_ref[...]
        return (o * (1.0 - lam_init)).astype(BF16)

    G = math.gcd(ATTN_GROUP, t_lat // QB)

    def latent_group(i, carry):
        rows = [pl.ds(pl.multiple_of((G * i + j) * QB, QB), QB) for j in range(G)]
        s_next = scores(q_ref[0, rows[0], :], kt_ref[0, 0])
        for j in range(G):
            s_cur = s_next
            if j + 1 < G:
                s_next = scores(q_ref[0, rows[j + 1], :], kt_ref[0, 0])
            o_ref[0, rows[j], :] = finish(s_cur, v_ref[0])
        return carry

    lax.fori_loop(0, t_lat // (G * QB), latent_group, 0)
    o_ref[0, t_lat:, :] = finish(scores(q_ref[0, t_lat:, :], kt_ref[0, 0, :, t_lat:]),
                                 v_ref[0, t_lat:, :])


def _attention(lam_p, q, kt, v, sub_norm, t_lat, lam_init):
    B, NT, D = q.shape
    blk = pl.BlockSpec((1, NT, LANES), lambda b, h: (b, 0, h))
    return pl.pallas_call(
        functools.partial(_attn_kernel, t_lat=t_lat, lam_init=lam_init),
        grid=(B, DA_HEADS),
        in_specs=[
            pl.BlockSpec(lam_p.shape, lambda b, h: (0, 0)),
            blk, pl.BlockSpec((1, 1, LANES, NT), lambda b, h: (b, h, 0, 0)), blk,
            pl.BlockSpec((1, LANES), lambda b, h: (0, 0)),
        ],
        out_specs=blk,
        out_shape=jax.ShapeDtypeStruct((B, NT, D), BF16),
        compiler_params=_cparams(("parallel", "arbitrary")),
        name="diff_attention",
    )(lam_p, q, kt, v, sub_norm)


def _out_router_kernel(o_ref, x_ref, w_ref, gm_ref, g_ref, sc_ref, sh_ref, rwt_ref, rb_ref,
                       xo_ref, f_ref, bk_ref):
    D = D_MODEL
    out = jnp.dot(o_ref[0], w_ref[...], preferred_element_type=F32)
    x = x_ref[0] + gm_ref[0] * out
    xo_ref[0] = x
    f = _norm_mod(x, g_ref[...], sc_ref[0], sh_ref[0])
    for j in range(D // LANES):
        f_ref[0, :, j, :] = f[:, j * LANES:(j + 1) * LANES]
    logits = lax.dot_general(rwt_ref[...], f, (((1,), (1,)), ((), ())),
                             preferred_element_type=F32, precision=HIGHEST)
    aff = jax.nn.sigmoid(logits)
    biased = aff + rb_ref[...]
    G = EXPERTS_PER_GROUP
    a = [aff[e:e + 1] for e in range(N_EXPERTS)]
    s = [biased[e:e + 1] for e in range(N_EXPERTS)]
    gscore = []
    for g in range(N_GROUPS):
        best = None
        for i, j in PAIRS:
            ps = s[g * G + i] + s[g * G + j]
            best = ps if best is None else jnp.maximum(best, ps)
        gscore.append(best)
    sel = jnp.zeros_like(gscore[0], dtype=jnp.int32)
    best = gscore[0]
    for g in range(1, N_GROUPS):
        better = gscore[g] > best
        sel = jnp.where(better, g, sel)
        best = jnp.where(better, gscore[g], best)
    bucket = jnp.zeros_like(sel)
    a_lo = jnp.zeros_like(best)
    a_hi = jnp.zeros_like(best)
    for g in range(N_GROUPS):
        chosen = []
        for j in range(G):
            rank = jnp.zeros_like(sel)
            for i in range(G):
                if i == j:
                    continue
                si, sj = s[g * G + i], s[g * G + j]
                ahead = (si > sj) | ((si == sj) & (i < j))
                rank = rank + ahead.astype(jnp.int32)
            chosen.append(rank < 2)
        in_g = sel == g
        for p, (i, j) in enumerate(PAIRS):
            pm = chosen[i] & chosen[j] & in_g
            bucket = jnp.where(pm, g * len(PAIRS) + p, bucket)
            a_lo = jnp.where(pm, a[g * G + i], a_lo)
            a_hi = jnp.where(pm, a[g * G + j], a_hi)
    tot = a_lo + a_hi
    bk_ref[0] = bucket

    tb = x.shape[0]
    eye = (lax.broadcasted_iota(jnp.int32, (tb, tb), 0)
           == lax.broadcasted_iota(jnp.int32, (tb, tb), 1))

    def column(row):
        return jnp.sum(jnp.where(eye, row, 0.0), axis=1, keepdims=True)

    lane = lax.broadcasted_iota(jnp.int32, (tb, LANES), 1)
    f_ref[0, :, D // LANES, :] = jnp.where(lane < LANES // 2, column(a_lo / tot), column(a_hi / tot))


def _out_router(o, xa, w_o, mod3, g_ffn, rwt, rb, n_lat, ctx_row):
    B, NT, D = xa.shape
    TB = TOKEN_BLOCK
    nb = NT // TB
    tok = pl.BlockSpec((1, TB, D), lambda b, t: (b, t, 0))
    row = pl.BlockSpec((1, 1, TB), lambda b, t: (b * nb + t, 0, 0))
    const2 = lambda shape: pl.BlockSpec(shape, lambda b, t: (0, 0))
    rows = lambda dt: jax.ShapeDtypeStruct((B * nb, 1, TB), dt)
    return pl.pallas_call(
        _out_router_kernel,
        grid=(B, nb),
        in_specs=[
            tok, tok, const2((D, D)), _mod_spec(2, n_lat, ctx_row), const2((1, D)),
            _mod_spec(4, n_lat, ctx_row), _mod_spec(3, n_lat, ctx_row),
            const2((N_EXPERTS, D)), const2((N_EXPERTS, 1)),
        ],
        out_specs=[tok, pl.BlockSpec((1, TB, ROW_TILES + 1, LANES), lambda b, t: (b, t, 0, 0)), row],
        out_shape=[jax.ShapeDtypeStruct((B, NT, D), F32),
                   jax.ShapeDtypeStruct((B, NT, ROW_TILES + 1, LANES), F32), rows(jnp.int32)],
        compiler_params=_cparams(("parallel", "arbitrary")),
        name="out_proj_router",
    )(o, xa, w_o, mod3, g_ffn, mod3, mod3, rwt, rb)


def _routing_tables(bucket, n_tokens):
    tm = MOE_TILE
    max_tiles = n_tokens // tm + N_BUCKETS
    n_rows = max_tiles * tm
    onehot = (bucket[:, None] == jnp.arange(N_BUCKETS, dtype=jnp.int32)[None, :]).astype(jnp.int32)
    count = jnp.sum(onehot, axis=0)
    rank = jnp.sum(jnp.cumsum(onehot, axis=0) * onehot, axis=1) - 1
    btiles = (count + tm - 1) // tm
    tile_end = jnp.cumsum(btiles)
    tile_start = tile_end - btiles
    dest = tile_start[bucket] * tm + rank
    token_at = jnp.full((n_rows,), -1, jnp.int32).at[dest].set(jnp.arange(n_tokens, dtype=jnp.int32))
    is_pad = token_at < 0
    spare = n_tokens + jnp.cumsum(is_pad.astype(jnp.int32)) - 1
    gather_idx = jnp.where(is_pad, 0, token_at)
    scatter_idx = jnp.where(is_pad, spare, token_at)
    tile = jnp.arange(max_tiles, dtype=jnp.int32)
    tile_bucket = jnp.minimum(jnp.sum((tile[:, None] >= tile_end[None, :]).astype(jnp.int32), axis=1),
                              N_BUCKETS - 1)
    grp = tile_bucket // len(PAIRS)
    pair = tile_bucket % len(PAIRS)
    lo_tab = jnp.array([p[0] for p in PAIRS], jnp.int32)
    hi_tab = jnp.array([p[1] for p in PAIRS], jnp.int32)
    tile_lo = grp * EXPERTS_PER_GROUP + lo_tab[pair]
    tile_hi = grp * EXPERTS_PER_GROUP + hi_tab[pair]
    return gather_idx, scatter_idx, tile_lo, tile_hi


def _expert_kernel(gidx_ref, sidx_ref, tlo_ref, thi_ref, f_hbm, gu_lo_ref, dn_lo_ref, gu_hi_ref,
                   dn_hi_ref, y_hbm, xbuf, ybuf, gsem, ssem):
    TM, D = MOE_TILE, D_MODEL
    i = pl.program_id(0)
    n = pl.num_programs(0)
    slot = lax.rem(i, 2)
    other = 1 - slot

    def start_gather(tile, s):
        for r in range(TM):
            pltpu.make_async_copy(f_hbm.at[pl.ds(gidx_ref[tile * TM + r], 1)],
                                  xbuf.at[s, pl.ds(r, 1)], gsem.at[s]).start(priority=r % 2)

    def start_scatter(tile, s):
        for r in range(TM):
            pltpu.make_async_copy(ybuf.at[s, pl.ds(r, 1)],
                                  y_hbm.at[pl.ds(sidx_ref[tile * TM + r], 1)],
                                  ssem.at[s]).start(priority=r % 2)

    def wait_gather(s):
        pltpu.make_async_copy(f_hbm.at[pl.ds(0, TM)], xbuf.at[s], gsem.at[s]).wait()

    def wait_scatter(s):
        pltpu.make_async_copy(ybuf.at[s], y_hbm.at[pl.ds(0, TM)], ssem.at[s]).wait()

    @pl.when(i == 0)
    def _():
        start_gather(0, 0)

    wait_gather(slot)

    @pl.when(i >= 2)
    def _():
        wait_scatter(slot)

    start_gather(jnp.minimum(i + 1, n - 1), other)

    x = jnp.concatenate([xbuf[slot, :, j, :] for j in range(ROW_TILES)], axis=-1).astype(BF16)
    gates = xbuf[slot, :, ROW_TILES, :]

    def ffn(gu_ref, dn_ref):
        gu = jnp.dot(x, gu_ref[0], preferred_element_type=F32)
        hid = _silu(gu[:, :D_FF]) * gu[:, D_FF:]
        return jnp.dot(hid.astype(BF16), dn_ref[0], preferred_element_type=F32)

    y = (gates[:, 0:1] * ffn(gu_lo_ref, dn_lo_ref)
         + gates[:, LANES // 2:LANES // 2 + 1] * ffn(gu_hi_ref, dn_hi_ref))
    for j in range(ROW_TILES):
        ybuf[slot, :, j, :] = y[:, j * LANES:(j + 1) * LANES]
    start_scatter(i, slot)

    @pl.when(i == n - 1)
    def _():
        wait_gather(other)
        wait_scatter(other)
        wait_scatter(slot)


def _expert_ffn(tables, f_ext, w_gu, w_dn):
    gather_idx, scatter_idx, tile_lo, tile_hi = tables
    P = gather_idx.shape[0]
    D = D_MODEL
    tm = MOE_TILE
    lo3 = lambda i, gi, si, tlo, thi: (tlo[i], 0, 0)
    hi3 = lambda i, gi, si, tlo, thi: (thi[i], 0, 0)
    return pl.pallas_call(
        _expert_kernel,
        grid_spec=pltpu.PrefetchScalarGridSpec(
            num_scalar_prefetch=4,
            grid=(P // tm,),
            in_specs=[
                pl.BlockSpec(memory_space=pl.ANY),
                pl.BlockSpec((1, D, 2 * D_FF), lo3), pl.BlockSpec((1, D_FF, D), lo3),
                pl.BlockSpec((1, D, 2 * D_FF), hi3), pl.BlockSpec((1, D_FF, D), hi3),
            ],
            out_specs=pl.BlockSpec(memory_space=pl.ANY),
            scratch_shapes=[pltpu.VMEM((2, tm, ROW_TILES + 1, LANES), F32),
                            pltpu.VMEM((2, tm, ROW_TILES, LANES), F32),
                            pltpu.SemaphoreType.DMA((2,)), pltpu.SemaphoreType.DMA((2,))],
        ),
        out_shape=jax.ShapeDtypeStruct((P, ROW_TILES, LANES), F32),
        compiler_params=_cparams(("arbitrary",)),
        name="moe_expert_ffn",
    )(gather_idx, scatter_idx, tile_lo, tile_hi, f_ext, w_gu, w_dn, w_gu, w_dn)


def _combine_kernel(x_ref, gf_ref, y_ref, o_ref):
    y = jnp.concatenate([y_ref[:, j, :] for j in range(ROW_TILES)], axis=-1)
    o_ref[0] = x_ref[0] + gf_ref[0] * y


def _moe_combine(xa, mod3, y, n_lat, ctx_row, n_blocks_out):
    B, NT, D = xa.shape
    TB = TOKEN_BLOCK
    nb = NT // TB
    tok = pl.BlockSpec((1, TB, D), lambda b, t: (b, t, 0))
    return pl.pallas_call(
        _combine_kernel,
        grid=(B, n_blocks_out),
        in_specs=[tok, _mod_spec(5, n_lat, ctx_row),
                  pl.BlockSpec((TB, ROW_TILES, LANES), lambda b, t: (b * nb + t, 0, 0))],
        out_specs=tok,
        out_shape=jax.ShapeDtypeStruct((B, n_blocks_out * TB, D), F32),
        compiler_params=_cparams(("parallel", "arbitrary")),
        name="moe_combine",
    )(xa, mod3, y)


def _hgrn_proj_kernel(x_ref, g_ref, sc_ref, sh_ref, w_ref, lb_ref,
                      q_ref, v_ref, kf_ref, lff_ref, kb_ref, lfb_ref, gate_ref):
    HK = HG_HEADS * HG_DIM
    h = _norm_mod(x_ref[0], g_ref[...], sc_ref[0], sh_ref[0])
    y = jnp.dot(h.astype(BF16), w_ref[...], preferred_element_type=F32)
    q_ref[0] = _silu(y[:, :HK])
    v_ref[0] = y[:, HK:2 * HK]
    gate_ref[0] = y[:, 4 * HK:]

    def forget(z, lbd, k_ref, lf_ref):
        e = jnp.exp(-jnp.abs(z))
        t = 1.0 + e
        k_ref[0] = (1.0 - lbd) * (jnp.where(z >= 0.0, e, 1.0) / t)
        a = jnp.log(lbd)
        b = jnp.log1p(-lbd) + (jnp.minimum(z, 0.0) - jnp.log(t))
        lf_ref[0] = jnp.maximum(a, b) + jnp.log(1.0 + jnp.exp(-jnp.abs(a - b)))

    forget(y[:, 2 * HK:3 * HK], lb_ref[0:1], kf_ref, lff_ref)
    forget(y[:, 3 * HK:4 * HK], lb_ref[1:2], kb_ref, lfb_ref)


def _hgrn_project(xa, g, mod3, w_in, lb, n_lat, ctx_row):
    B, NT, D = xa.shape
    TB = TOKEN_BLOCK
    HK = HG_HEADS * HG_DIM
    tok = pl.BlockSpec((1, TB, D), lambda b, t: (b, t, 0))
    tok_o = pl.BlockSpec((1, TB, HK), lambda b, t: (b, t, 0))
    const2 = lambda shape: pl.BlockSpec(shape, lambda b, t: (0, 0))
    out = jax.ShapeDtypeStruct((B, NT, HK), F32)
    return pl.pallas_call(
        _hgrn_proj_kernel,
        grid=(B, NT // TB),
        in_specs=[
            tok, const2((1, D)), _mod_spec(1, n_lat, ctx_row), _mod_spec(0, n_lat, ctx_row),
            pl.BlockSpec((D, 5 * HK), lambda b, t: (0, 0), pipeline_mode=pl.Buffered(1)),
            const2((2, HK)),
        ],
        out_specs=[tok_o] * 7,
        out_shape=[out] * 7,
        compiler_params=_cparams(("parallel", "arbitrary")),
        name="hgrn_proj",
    )(xa, g, mod3, mod3, w_in, lb)


def _split3(x):
    hi = x.astype(BF16)
    r = x - hi.astype(F32)
    mid = r.astype(BF16)
    return hi, mid, (r - mid.astype(F32)).astype(BF16)


def _scan_chunk_local(q, k, vb, a, forward):
    C, SB = SCAN_CHUNK, SCAN_SUB
    row = lax.broadcasted_iota(jnp.int32, a.shape, 0)
    first = row < SB
    mid_row = SB // 2
    m = jnp.where(first, a[mid_row:mid_row + 1], a[SB + mid_row:SB + mid_row + 1])
    qd = q * jnp.exp(jnp.minimum(a - m, EXP_CLAMP))
    kd = k * jnp.exp(jnp.minimum(m - a, EXP_CLAMP))
    edge = a[SB - 1:SB] if forward else a[SB:SB + 1]
    e_x = jnp.exp(-jnp.abs(a - edge))
    nt = (((1,), (1,)), ((), ()))
    s_d = lax.dot_general(qd.astype(BF16), kd.astype(BF16), nt, preferred_element_type=F32)
    s_x = lax.dot_general((q * e_x).astype(BF16), (k * e_x).astype(BF16), nt,
                          preferred_element_type=F32)
    t_i = lax.broadcasted_iota(jnp.int32, s_d.shape, 0)
    s_i = lax.broadcasted_iota(jnp.int32, s_d.shape, 1)
    same = (t_i < SB) == (s_i < SB)
    if forward:
        causal, cross = s_i <= t_i, (t_i >= SB) & (s_i < SB)
    else:
        causal, cross = s_i >= t_i, (t_i < SB) & (s_i >= SB)
    scores = jnp.where(same & causal, s_d, jnp.where(cross, s_x, 0.0))
    o_intra = jnp.dot(scores.astype(BF16), vb, preferred_element_type=F32)
    a_out = a[C - 1:C] if forward else a[0:1]
    k_out = k * jnp.exp(a_out - a)
    upd = lax.dot_general(vb, k_out.astype(BF16), (((0,), (0,)), ((), ())),
                          preferred_element_type=F32)
    return o_intra, (q * jnp.exp(a)).astype(BF16), upd, jnp.exp(a_out)


def _hgrn_scan_kernel(q_ref, v_ref, kf_ref, lff_ref, kb_ref, lfb_ref, gate_ref, gain_ref,
                      o_ref, acc_ref, qe_ref, upd_ref, dec_ref, *, n_chunks, n_lat_chunks):
    C, G = SCAN_CHUNK, SCAN_GROUP
    R = C * G
    r_i = lax.broadcasted_iota(jnp.int32, (R, R), 0)
    c_i = lax.broadcasted_iota(jnp.int32, (R, R), 1)
    same_chunk = (r_i // C) == (c_i // C)
    tri_f = (same_chunk & (c_i <= r_i)).astype(BF16)
    tri_b = (same_chunk & (c_i >= r_i)).astype(BF16)

    def cumsum(tri, lf):
        return sum(jnp.dot(tri, part, preferred_element_type=F32) for part in _split3(lf))

    def local(g, carry):
        rows = pl.ds(pl.multiple_of(g * R, R), R)
        q = q_ref[0, rows, :]
        vb = v_ref[0, rows, :].astype(BF16)
        kf, kb = kf_ref[0, rows, :], kb_ref[0, rows, :]
        a_f = cumsum(tri_f, lff_ref[0, rows, :])
        a_b = cumsum(tri_b, lfb_ref[0, rows, :])
        res = []
        for ci in range(G):
            sl = slice(ci * C, (ci + 1) * C)
            res.append((_scan_chunk_local(q[sl], kf[sl], vb[sl], a_f[sl], True),
                        _scan_chunk_local(q[sl], kb[sl], vb[sl], a_b[sl], False)))
        for ci, (fw, bw) in enumerate(res):
            c = g * G + ci
            crow = pl.ds(pl.multiple_of(c * C, C), C)
            acc_ref[crow, :] = fw[0] + bw[0]
            qe_ref[0, crow, :] = fw[1]
            qe_ref[1, crow, :] = bw[1]
            upd_ref[0, c] = fw[2]
            upd_ref[1, c] = bw[2]
            dec_ref[0, pl.ds(c, 1), :] = fw[3]
            dec_ref[1, pl.ds(c, 1), :] = bw[3]
        return carry

    lax.fori_loop(0, n_chunks // G, local, 0)

    nt = (((1,), (1,)), ((), ()))

    def carry_state(j, carry):
        st_f, st_b = carry
        cf = lax.rem(j + n_lat_chunks, n_chunks)
        cb = n_chunks - 1 - j
        rf = pl.ds(pl.multiple_of(cf * C, C), C)
        rb = pl.ds(pl.multiple_of(cb * C, C), C)
        acc_ref[rf, :] += lax.dot_general(qe_ref[0, rf, :], st_f.astype(BF16), nt,
                                          preferred_element_type=F32)
        acc_ref[rb, :] += lax.dot_general(qe_ref[1, rb, :], st_b.astype(BF16), nt,
                                          preferred_element_type=F32)
        st_f = st_f * dec_ref[0, pl.ds(cf, 1), :] + upd_ref[0, cf]
        st_b = st_b * dec_ref[1, pl.ds(cb, 1), :] + upd_ref[1, cb]
        return st_f, st_b

    zero = jnp.zeros((HG_DIM, HG_DIM), F32)
    lax.fori_loop(0, n_chunks, carry_state, (zero, zero), unroll=2)
    o = acc_ref[...]
    o = o * lax.rsqrt(jnp.mean(o * o, axis=-1, keepdims=True) + EPS) * gain_ref[...]
    o_ref[0] = (o * _silu(gate_ref[0])).astype(BF16)


def _hgrn_scan(q, v, kf, lff, kb, lfb, gate, gain, t_lat):
    B, NT, HK = q.shape
    blk = pl.BlockSpec((1, NT, HG_DIM), lambda b, h: (b, 0, h))
    return pl.pallas_call(
        functools.partial(_hgrn_scan_kernel, n_chunks=NT // SCAN_CHUNK,
                          n_lat_chunks=t_lat // SCAN_CHUNK),
        grid=(B, HG_HEADS),
        in_specs=[blk] * 7 + [pl.BlockSpec((1, HG_DIM), lambda b, h: (0, 0))],
        out_specs=blk,
        out_shape=jax.ShapeDtypeStruct((B, NT, HK), BF16),
        scratch_shapes=[pltpu.VMEM((NT, HG_DIM), F32),
                        pltpu.VMEM((2, NT, HG_DIM), BF16),
                        pltpu.VMEM((2, NT // SCAN_CHUNK, HG_DIM, HG_DIM), F32),
                        pltpu.VMEM((2, NT // SCAN_CHUNK, HG_DIM), F32)],
        compiler_params=_cparams(("parallel", "arbitrary")),
        name="hgrn_scan",
    )(q, v, kf, lff, kb, lfb, gate, gain)


def _rope_tables(t_lat, t_ctx):
    rows = t_lat // GRID_W
    r = jnp.repeat(jnp.arange(rows, dtype=F32), GRID_W)
    col = jnp.tile(jnp.arange(GRID_W, dtype=F32), rows)
    n_pairs = DA_QK_DIM // 4
    inv = ROPE_BASE ** (-jnp.arange(n_pairs, dtype=F32) / n_pairs)
    ang = jnp.concatenate([r[:, None] * inv, col[:, None] * inv], axis=-1)
    cos, sin = jnp.cos(ang), jnp.sin(ang)
    cos_l = jnp.tile(cos, (1, LANES // cos.shape[1]))
    sin_l = jnp.tile(jnp.concatenate([-sin, sin], axis=-1), (1, LANES // (2 * sin.shape[1])))
    cos_l = jnp.concatenate([cos_l, jnp.ones((t_ctx, LANES), F32)], axis=0)
    sin_l = jnp.concatenate([sin_l, jnp.zeros((t_ctx, LANES), F32)], axis=0)
    return cos_l, sin_l


def kernel(x, c, ctx, c_ctx, ada_w, ada_b, norm_mix, norm_ffn, attn_w_qkv, attn_w_o, attn_q_norm,
           attn_k_norm, attn_sub_norm, attn_lambda, hgrn_w_in, hgrn_w_o, hgrn_out_norm,
           hgrn_lb_gamma, router_w, router_bias, moe_w_gate, moe_w_up, moe_w_down):
    B, T, D = x.shape
    Tc = ctx.shape[1]
    TB = TOKEN_BLOCK
    assert D == D_MODEL and T % TB == 0 and Tc % TB == 0 and T % GRID_W == 0
    assert ada_w.shape[0] == DEPTH == 2
    assert (T + Tc) % (SCAN_CHUNK * SCAN_GROUP) == 0 and T % SCAN_CHUNK == 0
    assert T % ATTN_Q_BLOCK == 0
    NT = T + Tc
    n_lat = T // TB
    n_tokens = B * NT
    ctx_row = B

    n_rows = -(-(B + 1) // 8) * 8
    cvec = jnp.concatenate([c, c_ctx[None, :], jnp.zeros((n_rows - B - 1, D), F32)], axis=0)
    mod = _modulation(cvec, ada_w, ada_b)

    xa = jnp.concatenate([x, ctx], axis=1)
    rwt = router_w.T
    rb = router_bias.reshape(N_EXPERTS, 1)

    p = jax.nn.softmax(hgrn_lb_gamma.astype(F32), axis=1)
    cum = jnp.cumsum(p, axis=1)
    lb_all = cum - cum[:, :1]

    cos_t, sin_t = _rope_tables(T, Tc)
    lane = jnp.arange(LANES)
    bd = (lane[:, None] // DA_QK_DIM == lane[None, :] // DA_QK_DIM).astype(BF16)

    for i in range(DEPTH):
        mod3 = mod[i].reshape(n_rows, 1, 6 * D)
        last = i == DEPTH - 1
        j = i // 2
        g_mix = norm_mix[i].reshape(1, D)
        if i % 2 == 0:
            lam_init = 0.8 - 0.6 * math.exp(-0.3 * i)
            q, k, v = _attn_project(
                xa, g_mix, mod3, attn_w_qkv[j].astype(BF16),
                jnp.tile(attn_q_norm[j], LANES // DA_QK_DIM).reshape(1, LANES),
                jnp.tile(attn_k_norm[j], LANES // DA_QK_DIM).reshape(1, LANES),
                cos_t, sin_t, bd, n_lat, ctx_row)
            o = _attention(attn_lambda[j], q, k, v, attn_sub_norm[j].reshape(1, DA_V_DIM),
                           T, lam_init)
            w_o = attn_w_o[j]
        else:
            parts = _hgrn_project(xa, g_mix, mod3, hgrn_w_in[j].astype(BF16), lb_all[:, i],
                                  n_lat, ctx_row)
            o = _hgrn_scan(*parts, hgrn_out_norm[j].reshape(1, HG_DIM), T)
            w_o = hgrn_w_o[j]
        xa, f_ext, bucket = _out_router(
            o, xa, w_o.astype(BF16), mod3, norm_ffn[i].reshape(1, D), rwt, rb, n_lat, ctx_row)
        w_gu = jnp.concatenate([moe_w_gate[i], moe_w_up[i]], axis=-1).astype(BF16)
        tables = _routing_tables(bucket.reshape(n_tokens), n_tokens)
        y = _expert_ffn(tables, f_ext.reshape(n_tokens, ROW_TILES + 1, LANES), w_gu,
                        moe_w_down[i].astype(BF16))
        xa = _moe_combine(xa, mod3, y, n_lat, ctx_row, n_lat if last else NT // TB)
    return xa
```

```python
import functools
import math

import jax
import jax.numpy as jnp
from jax import lax
from jax.experimental import pallas as pl
from jax.experimental.pallas import tpu as pltpu

F32 = jnp.float32
BF16 = jnp.bfloat16
HIGHEST = lax.Precision.HIGHEST

D_MODEL = 1024
DEPTH = 2
GRID_W = 64
DA_HEADS = 8
DA_QK_DIM = 64
DA_V_DIM = 128
ROPE_BASE = 10000.0
HG_HEADS = 8
HG_DIM = 128
N_EXPERTS = 16
N_GROUPS = 4
EXPERTS_PER_GROUP = 4
D_FF = 512
EPS = 1e-6

LANES = 128
TOKEN_BLOCK = 256
ATTN_Q_BLOCK = 256
ATTN_GROUP = 4
MOE_TILE = 256
MOE_DMA_UNROLL = 32
PAIRS = [(i, j) for i in range(EXPERTS_PER_GROUP) for j in range(i + 1, EXPERTS_PER_GROUP)]
N_BUCKETS = N_GROUPS * len(PAIRS)
SCAN_CHUNK = 64
SCAN_SUB = SCAN_CHUNK // 2
SCAN_GROUP = 4
EXP_CLAMP = 80.0
VMEM_LIMIT = 56 * 1024 * 1024


def _cparams(sem):
    return pltpu.CompilerParams(dimension_semantics=sem, vmem_limit_bytes=VMEM_LIMIT)


def _silu(x):
    return x * jax.nn.sigmoid(x)


def _norm_mod(x, g, sc, sh):
    y = x * lax.rsqrt(jnp.mean(x * x, axis=-1, keepdims=True) + EPS)
    return (y * g) * (1.0 + sc) + sh


def _mod_kernel(c_ref, w_ref, b_ref, o_ref):
    o_ref[0] = jnp.dot(_silu(c_ref[...]), w_ref[0], preferred_element_type=F32,
                       precision=HIGHEST) + b_ref[0]


def _modulation(cvec, ada_w, ada_b):
    R, D = cvec.shape
    depth, _, n6 = ada_w.shape
    tn = 1024
    return pl.pallas_call(
        _mod_kernel,
        grid=(depth, n6 // tn),
        in_specs=[
            pl.BlockSpec((R, D), lambda i, j: (0, 0)),
            pl.BlockSpec((1, D, tn), lambda i, j: (i, 0, j)),
            pl.BlockSpec((1, 1, tn), lambda i, j: (i, 0, j)),
        ],
        out_specs=pl.BlockSpec((1, R, tn), lambda i, j: (i, 0, j)),
        out_shape=jax.ShapeDtypeStruct((depth, R, n6), F32),
        compiler_params=_cparams(("arbitrary", "arbitrary")),
        name="adaln_mod",
    )(cvec, ada_w, ada_b.reshape(depth, 1, n6))


def _mod_spec(col, n_lat, ctx_row):
    return pl.BlockSpec((1, 1, D_MODEL), lambda b, t: (jnp.where(t < n_lat, b, ctx_row), 0, col))


def _attn_proj_kernel(x_ref, g_ref, sc_ref, sh_ref, w_ref, qg_ref, kg_ref, cos_ref, sin_ref,
                      bd_ref, q_ref, k_ref, v_ref):
    D = D_MODEL
    h = _norm_mod(x_ref[0], g_ref[...], sc_ref[0], sh_ref[0])
    qkv = jnp.dot(h.astype(BF16), w_ref[...], preferred_element_type=F32)
    cos, sin, bd = cos_ref[...], sin_ref[...], bd_ref[...]
    lane = lax.broadcasted_iota(jnp.int32, cos.shape, 1)
    upper = (lane & (DA_QK_DIM // 2)) != 0

    def norm_rope(t, gain, scale):
        sq = t * t
        sq_hi = sq.astype(BF16)
        sq_lo = (sq - sq_hi.astype(F32)).astype(BF16)
        ss = (jnp.dot(sq_hi, bd, preferred_element_type=F32)
              + jnp.dot(sq_lo, bd, preferred_element_type=F32))
        tn = t * lax.rsqrt(ss * (1.0 / DA_QK_DIM) + EPS) * gain
        partner = jnp.where(upper, pltpu.roll(tn, DA_QK_DIM // 2, 1),
                            pltpu.roll(tn, LANES - DA_QK_DIM // 2, 1))
        return (tn * cos + partner * sin) * scale

    for j in range(D // LANES):
        sl = slice(j * LANES, (j + 1) * LANES)
        q_ref[0, :, sl] = norm_rope(qkv[:, j * LANES:(j + 1) * LANES], qg_ref[...],
                                    math.log2(math.e) / math.sqrt(DA_QK_DIM)).astype(BF16)
        k_ref[0, j] = norm_rope(qkv[:, D + j * LANES:D + (j + 1) * LANES], kg_ref[...],
                                1.0).T.astype(BF16)
    v_ref[0] = qkv[:, 2 * D:].astype(BF16)


def _attn_project(xa, g, mod3, w_qkv, qg, kg, cos_t, sin_t, bd, n_lat, ctx_row):
    B, NT, D = xa.shape
    TB = TOKEN_BLOCK
    tok = pl.BlockSpec((1, TB, D), lambda b, t: (b, t, 0))
    const2 = lambda shape: pl.BlockSpec(shape, lambda b, t: (0, 0))
    out = jax.ShapeDtypeStruct((B, NT, D), BF16)
    return pl.pallas_call(
        _attn_proj_kernel,
        grid=(B, NT // TB),
        in_specs=[
            tok, const2((1, D)), _mod_spec(1, n_lat, ctx_row), _mod_spec(0, n_lat, ctx_row),
            const2((D, 3 * D)), const2((1, LANES)), const2((1, LANES)),
            pl.BlockSpec((TB, LANES), lambda b, t: (t, 0)),
            pl.BlockSpec((TB, LANES), lambda b, t: (t, 0)),
            const2((LANES, LANES)),
        ],
        out_specs=[tok, pl.BlockSpec((1, DA_HEADS, LANES, TB), lambda b, t: (b, 0, 0, t)), tok],
        out_shape=[out, jax.ShapeDtypeStruct((B, DA_HEADS, LANES, NT), BF16), out],
        compiler_params=_cparams(("parallel", "arbitrary")),
        name="attn_qkv_proj",
    )(xa, g, mod3, mod3, w_qkv, qg, kg, cos_t, sin_t, bd)


def _attn_kernel(lam_ref, q_ref, kt_ref, v_ref, sn_ref, o_ref, *, t_lat, lam_init):
    QB = ATTN_Q_BLOCK
    lp = lam_ref[...]
    lam = (jnp.exp(jnp.sum(lp[0:1] * lp[1:2], keepdims=True))
           - jnp.exp(jnp.sum(lp[2:3] * lp[3:4], keepdims=True)) + lam_init)

    def scores(q, kt):
        lane = lax.broadcasted_iota(jnp.int32, q.shape, 1)
        zero = jnp.zeros_like(q)
        return (jnp.dot(jnp.where(lane < DA_QK_DIM, q, zero), kt, preferred_element_type=F32),
                jnp.dot(jnp.where(lane >= DA_QK_DIM, q, zero), kt, preferred_element_type=F32))

    def finish(s, v):
        def softmax_parts(sm):
            e = jnp.exp2(sm - jnp.max(sm, axis=-1, keepdims=True))
            return e, jnp.sum(e, axis=-1, keepdims=True)

        e0, l0 = softmax_parts(s[0])
        e1, l1 = softmax_parts(s[1])
        a = e0 - (lam * l0 / l1) * e1
        o = jnp.dot(a.astype(BF16), v, preferred_element_type=F32) * (1.0 / l0)
        o = o * lax.rsqrt(jnp.mean(o * o, axis=-1, keepdims=True) + EPS) * sn_ref[...]
        return (o * (1.0 - lam_init)).astype(BF16)

    G = math.gcd(ATTN_GROUP, t_lat // QB)

    def latent_group(i, carry):
        rows = [pl.ds(pl.multiple_of((G * i + j) * QB, QB), QB) for j in range(G)]
        s_next = scores(q_ref[0, rows[0], :], kt_ref[0, 0])
        for j in range(G):
            s_cur = s_next
            if j + 1 < G:
                s_next = scores(q_ref[0, rows[j + 1], :], kt_ref[0, 0])
            o_ref[0, rows[j], :] = finish(s_cur, v_ref[0])
        return carry

    lax.fori_loop(0, t_lat // (G * QB), latent_group, 0)
    o_ref[0, t_lat:, :] = finish(scores(q_ref[0, t_lat:, :], kt_ref[0, 0, :, t_lat:]),
                                 v_ref[0, t_lat:, :])


def _attention(lam_p, q, kt, v, sub_norm, t_lat, lam_init):
    B, NT, D = q.shape
    blk = pl.BlockSpec((1, NT, LANES), lambda b, h: (b, 0, h))
    return pl.pallas_call(
        functools.partial(_attn_kernel, t_lat=t_lat, lam_init=lam_init),
        grid=(B, DA_HEADS),
        in_specs=[
            pl.BlockSpec(lam_p.shape, lambda b, h: (0, 0)),
            blk, pl.BlockSpec((1, 1, LANES, NT), lambda b, h: (b, h, 0, 0)), blk,
            pl.BlockSpec((1, LANES), lambda b, h: (0, 0)),
        ],
        out_specs=blk,
        out_shape=jax.ShapeDtypeStruct((B, NT, D), BF16),
        compiler_params=_cparams(("parallel", "arbitrary")),
        name="diff_attention",
    )(lam_p, q, kt, v, sub_norm)


def _out_router_kernel(o_ref, x_ref, w_ref, gm_ref, g_ref, sc_ref, sh_ref, rwt_ref, rb_ref,
                       xo_ref, f_ref, bk_ref):
    D = D_MODEL
    out = jnp.dot(o_ref[0], w_ref[...], preferred_element_type=F32)
    x = x_ref[0] + gm_ref[0] * out
    xo_ref[0] = x
    f = _norm_mod(x, g_ref[...], sc_ref[0], sh_ref[0])
    f_ref[0, :, :D] = f
    logits = lax.dot_general(rwt_ref[...], f, (((1,), (1,)), ((), ())),
                             preferred_element_type=F32, precision=HIGHEST)
    aff = jax.nn.sigmoid(logits)
    biased = aff + rb_ref[...]
    G = EXPERTS_PER_GROUP
    a = [aff[e:e + 1] for e in range(N_EXPERTS)]
    s = [biased[e:e + 1] for e in range(N_EXPERTS)]
    gscore = []
    for g in range(N_GROUPS):
        best = None
        for i, j in PAIRS:
            ps = s[g * G + i] + s[g * G + j]
            best = ps if best is None else jnp.maximum(best, ps)
        gscore.append(best)
    sel = jnp.zeros_like(gscore[0], dtype=jnp.int32)
    best = gscore[0]
    for g in range(1, N_GROUPS):
        better = gscore[g] > best
        sel = jnp.where(better, g, sel)
        best = jnp.where(better, gscore[g], best)
    bucket = jnp.zeros_like(sel)
    a_lo = jnp.zeros_like(best)
    a_hi = jnp.zeros_like(best)
    for g in range(N_GROUPS):
        chosen = []
        for j in range(G):
            rank = jnp.zeros_like(sel)
            for i in range(G):
                if i == j:
                    continue
                si, sj = s[g * G + i], s[g * G + j]
                ahead = (si > sj) | ((si == sj) & (i < j))
                rank = rank + ahead.astype(jnp.int32)
            chosen.append(rank < 2)
        in_g = sel == g
        for p, (i, j) in enumerate(PAIRS):
            pm = chosen[i] & chosen[j] & in_g
            bucket = jnp.where(pm, g * len(PAIRS) + p, bucket)
            a_lo = jnp.where(pm, a[g * G + i], a_lo)
            a_hi = jnp.where(pm, a[g * G + j], a_hi)
    tot = a_lo + a_hi
    bk_ref[0] = bucket

    tb = x.shape[0]
    eye = (lax.broadcasted_iota(jnp.int32, (tb, tb), 0)
           == lax.broadcasted_iota(jnp.int32, (tb, tb), 1))

    def column(row):
        return jnp.sum(jnp.where(eye, row, 0.0), axis=1, keepdims=True)

    lane = lax.broadcasted_iota(jnp.int32, (tb, LANES), 1)
    f_ref[0, :, D:] = jnp.where(lane < LANES // 2, column(a_lo / tot), column(a_hi / tot))


def _out_router(o, xa, w_o, mod3, g_ffn, rwt, rb, n_lat, ctx_row):
    B, NT, D = xa.shape
    TB = TOKEN_BLOCK
    nb = NT // TB
    tok = pl.BlockSpec((1, TB, D), lambda b, t: (b, t, 0))
    row = pl.BlockSpec((1, 1, TB), lambda b, t: (b * nb + t, 0, 0))
    const2 = lambda shape: pl.BlockSpec(shape, lambda b, t: (0, 0))
    rows = lambda dt: jax.ShapeDtypeStruct((B * nb, 1, TB), dt)
    return pl.pallas_call(
        _out_router_kernel,
        grid=(B, nb),
        in_specs=[
            tok, tok, const2((D, D)), _mod_spec(2, n_lat, ctx_row), const2((1, D)),
            _mod_spec(4, n_lat, ctx_row), _mod_spec(3, n_lat, ctx_row),
            const2((N_EXPERTS, D)), const2((N_EXPERTS, 1)),
        ],
        out_specs=[tok, pl.BlockSpec((1, TB, D + LANES), lambda b, t: (b, t, 0)), row],
        out_shape=[jax.ShapeDtypeStruct((B, NT, D), F32),
                   jax.ShapeDtypeStruct((B, NT, D + LANES), F32), rows(jnp.int32)],
        compiler_params=_cparams(("parallel", "arbitrary")),
        name="out_proj_router",
    )(o, xa, w_o, mod3, g_ffn, mod3, mod3, rwt, rb)


def _routing_tables(bucket, n_tokens):
    tm = MOE_TILE
    max_tiles = n_tokens // tm + N_BUCKETS
    n_rows = max_tiles * tm
    onehot = (bucket[:, None] == jnp.arange(N_BUCKETS, dtype=jnp.int32)[None, :]).astype(jnp.int32)
    count = jnp.sum(onehot, axis=0)
    rank = jnp.sum(jnp.cumsum(onehot, axis=0) * onehot, axis=1) - 1
    btiles = (count + tm - 1) // tm
    tile_end = jnp.cumsum(btiles)
    tile_start = tile_end - btiles
    dest = tile_start[bucket] * tm + rank
    token_at = jnp.zeros((n_rows,), jnp.int32).at[dest].set(jnp.arange(n_tokens, dtype=jnp.int32))
    tile = jnp.arange(max_tiles, dtype=jnp.int32)
    tile_bucket = jnp.minimum(jnp.sum((tile[:, None] >= tile_end[None, :]).astype(jnp.int32), axis=1),
                              N_BUCKETS - 1)
    in_bucket = tile - tile_start[tile_bucket]
    n_valid = jnp.clip(count[tile_bucket] - in_bucket * tm, 0, tm)
    n_valid = jnp.where(tile < tile_end[-1], n_valid, 0).astype(jnp.int32)
    grp = tile_bucket // len(PAIRS)
    pair = tile_bucket % len(PAIRS)
    lo_tab = jnp.array([p[0] for p in PAIRS], jnp.int32)
    hi_tab = jnp.array([p[1] for p in PAIRS], jnp.int32)
    tile_lo = grp * EXPERTS_PER_GROUP + lo_tab[pair]
    tile_hi = grp * EXPERTS_PER_GROUP + hi_tab[pair]
    return token_at, n_valid, tile_lo, tile_hi


def _expert_kernel(tok_ref, nv_ref, tlo_ref, thi_ref, f_hbm, gu_lo_ref, dn_lo_ref, gu_hi_ref,
                   dn_hi_ref, y_hbm, xbuf, ybuf, gsem, ssem):
    TM, D = MOE_TILE, D_MODEL
    i = pl.program_id(0)
    n = pl.num_programs(0)
    slot = lax.rem(i, 2)
    other = 1 - slot

    def gather_copy(tile, r, s):
        return pltpu.make_async_copy(f_hbm.at[pl.ds(tok_ref[tile * TM + r], 1), :],
                                     xbuf.at[s, pl.ds(r, 1), :], gsem.at[s])

    def scatter_copy(tile, r, s):
        return pltpu.make_async_copy(ybuf.at[s, pl.ds(r, 1), :],
                                     y_hbm.at[pl.ds(tok_ref[tile * TM + r], 1), :], ssem.at[s])

    def for_rows(n_rows, fn):
        U = MOE_DMA_UNROLL
        n_groups = n_rows // U

        def group(g, carry):
            base = pl.multiple_of(g * U, U)
            for u in range(U):
                fn(base + u)
            return carry

        def single(r, carry):
            fn(r)
            return carry

        lax.fori_loop(0, n_groups, group, 0)
        lax.fori_loop(n_groups * U, n_rows, single, 0)

    def wait_rows(n_rows, whole_tile_copy, row_copy):
        @pl.when(n_rows == TM)
        def _():
            whole_tile_copy.wait()

        @pl.when(n_rows < TM)
        def _():
            for_rows(n_rows, lambda r: row_copy(r).wait())

    def start_gather(tile, s):
        for_rows(nv_ref[tile], lambda r: gather_copy(tile, r, s).start())

    def wait_gather(tile, s):
        wait_rows(nv_ref[tile],
                  pltpu.make_async_copy(f_hbm.at[pl.ds(0, TM), :], xbuf.at[s], gsem.at[s]),
                  lambda r: gather_copy(tile, r, s))

    def start_scatter(tile, s):
        for_rows(nv_ref[tile], lambda r: scatter_copy(tile, r, s).start())

    def wait_scatter(tile, s):
        wait_rows(nv_ref[tile],
                  pltpu.make_async_copy(ybuf.at[s], y_hbm.at[pl.ds(0, TM), :], ssem.at[s]),
                  lambda r: scatter_copy(tile, r, s))

    @pl.when(i == 0)
    def _():
        xbuf[...] = jnp.zeros_like(xbuf)
        start_gather(0, 0)

    wait_gather(i, slot)

    @pl.when(i >= 2)
    def _():
        wait_scatter(i - 2, slot)

    @pl.when(i + 1 < n)
    def _():
        start_gather(i + 1, other)

    @pl.when(nv_ref[i] > 0)
    def _():
        xe = xbuf[slot]
        x = xe[:, :D].astype(BF16)

        def ffn(gu_ref, dn_ref):
            gu = jnp.dot(x, gu_ref[0], preferred_element_type=F32)
            hid = _silu(gu[:, :D_FF]) * gu[:, D_FF:]
            return jnp.dot(hid.astype(BF16), dn_ref[0], preferred_element_type=F32)

        ybuf[slot] = (xe[:, D:D + 1] * ffn(gu_lo_ref, dn_lo_ref)
                      + xe[:, D + LANES // 2:D + LANES // 2 + 1] * ffn(gu_hi_ref, dn_hi_ref))
        start_scatter(i, slot)

    @pl.when(i == n - 1)
    def _():
        wait_scatter(i - 1, other)
        wait_scatter(i, slot)


def _expert_ffn(tables, f_ext, w_gu, w_dn):
    token_at, n_valid, tile_lo, tile_hi = tables
    N = f_ext.shape[0]
    D = D_MODEL
    tm = MOE_TILE
    lo3 = lambda i, tok, nv, tlo, thi: (tlo[i], 0, 0)
    hi3 = lambda i, tok, nv, tlo, thi: (thi[i], 0, 0)
    return pl.pallas_call(
        _expert_kernel,
        grid_spec=pltpu.PrefetchScalarGridSpec(
            num_scalar_prefetch=4,
            grid=(n_valid.shape[0],),
            in_specs=[
                pl.BlockSpec(memory_space=pl.ANY),
                pl.BlockSpec((1, D, 2 * D_FF), lo3), pl.BlockSpec((1, D_FF, D), lo3),
                pl.BlockSpec((1, D, 2 * D_FF), hi3), pl.BlockSpec((1, D_FF, D), hi3),
            ],
            out_specs=pl.BlockSpec(memory_space=pl.ANY),
            scratch_shapes=[pltpu.VMEM((2, tm, D + LANES), F32), pltpu.VMEM((2, tm, D), F32),
                            pltpu.SemaphoreType.DMA((2,)), pltpu.SemaphoreType.DMA((2,))],
        ),
        out_shape=jax.ShapeDtypeStruct((N, D), F32),
        compiler_params=_cparams(("arbitrary",)),
        name="moe_expert_ffn",
    )(token_at, n_valid, tile_lo, tile_hi, f_ext, w_gu, w_dn, w_gu, w_dn)


def _combine_kernel(x_ref, gf_ref, y_ref, o_ref):
    o_ref[0] = x_ref[0] + gf_ref[0] * y_ref[...]


def _moe_combine(xa, mod3, y, n_lat, ctx_row, n_blocks_out):
    B, NT, D = xa.shape
    TB = TOKEN_BLOCK
    nb = NT // TB
    tok = pl.BlockSpec((1, TB, D), lambda b, t: (b, t, 0))
    return pl.pallas_call(
        _combine_kernel,
        grid=(B, n_blocks_out),
        in_specs=[tok, _mod_spec(5, n_lat, ctx_row),
                  pl.BlockSpec((TB, D), lambda b, t: (b * nb + t, 0))],
        out_specs=tok,
        out_shape=jax.ShapeDtypeStruct((B, n_blocks_out * TB, D), F32),
        compiler_params=_cparams(("parallel", "arbitrary")),
        name="moe_combine",
    )(xa, mod3, y)


def _hgrn_proj_kernel(x_ref, g_ref, sc_ref, sh_ref, w_ref, lb_ref,
                      q_ref, v_ref, kf_ref, lff_ref, kb_ref, lfb_ref, gate_ref):
    HK = HG_HEADS * HG_DIM
    h = _norm_mod(x_ref[0], g_ref[...], sc_ref[0], sh_ref[0])
    y = jnp.dot(h.astype(BF16), w_ref[...], preferred_element_type=F32)
    q_ref[0] = _silu(y[:, :HK])
    v_ref[0] = y[:, HK:2 * HK]
    gate_ref[0] = y[:, 4 * HK:]

    def forget(z, lbd, k_ref, lf_ref):
        e = jnp.exp(-jnp.abs(z))
        t = 1.0 + e
        k_ref[0] = (1.0 - lbd) * (jnp.where(z >= 0.0, e, 1.0) / t)
        a = jnp.log(lbd)
        b = jnp.log1p(-lbd) + (jnp.minimum(z, 0.0) - jnp.log(t))
        lf_ref[0] = jnp.maximum(a, b) + jnp.log(1.0 + jnp.exp(-jnp.abs(a - b)))

    forget(y[:, 2 * HK:3 * HK], lb_ref[0:1], kf_ref, lff_ref)
    forget(y[:, 3 * HK:4 * HK], lb_ref[1:2], kb_ref, lfb_ref)


def _hgrn_project(xa, g, mod3, w_in, lb, n_lat, ctx_row):
    B, NT, D = xa.shape
    TB = TOKEN_BLOCK
    HK = HG_HEADS * HG_DIM
    tok = pl.BlockSpec((1, TB, D), lambda b, t: (b, t, 0))
    tok_o = pl.BlockSpec((1, TB, HK), lambda b, t: (b, t, 0))
    const2 = lambda shape: pl.BlockSpec(shape, lambda b, t: (0, 0))
    out = jax.ShapeDtypeStruct((B, NT, HK), F32)
    return pl.pallas_call(
        _hgrn_proj_kernel,
        grid=(B, NT // TB),
        in_specs=[
            tok, const2((1, D)), _mod_spec(1, n_lat, ctx_row), _mod_spec(0, n_lat, ctx_row),
            pl.BlockSpec((D, 5 * HK), lambda b, t: (0, 0), pipeline_mode=pl.Buffered(1)),
            const2((2, HK)),
        ],
        out_specs=[tok_o] * 7,
        out_shape=[out] * 7,
        compiler_params=_cparams(("parallel", "arbitrary")),
        name="hgrn_proj",
    )(xa, g, mod3, mod3, w_in, lb)


def _split3(x):
    hi = x.astype(BF16)
    r = x - hi.astype(F32)
    mid = r.astype(BF16)
    return hi, mid, (r - mid.astype(F32)).astype(BF16)


def _scan_chunk_local(q, k, vb, a, forward):
    C, SB = SCAN_CHUNK, SCAN_SUB
    row = lax.broadcasted_iota(jnp.int32, a.shape, 0)
    first = row < SB
    mid_row = SB // 2
    m = jnp.where(first, a[mid_row:mid_row + 1], a[SB + mid_row:SB + mid_row + 1])
    qd = q * jnp.exp(jnp.minimum(a - m, EXP_CLAMP))
    kd = k * jnp.exp(jnp.minimum(m - a, EXP_CLAMP))
    edge = a[SB - 1:SB] if forward else a[SB:SB + 1]
    e_x = jnp.exp(-jnp.abs(a - edge))
    nt = (((1,), (1,)), ((), ()))
    s_d = lax.dot_general(qd.astype(BF16), kd.astype(BF16), nt, preferred_element_type=F32)
    s_x = lax.dot_general((q * e_x).astype(BF16), (k * e_x).astype(BF16), nt,
                          preferred_element_type=F32)
    t_i = lax.broadcasted_iota(jnp.int32, s_d.shape, 0)
    s_i = lax.broadcasted_iota(jnp.int32, s_d.shape, 1)
    same = (t_i < SB) == (s_i < SB)
    if forward:
        causal, cross = s_i <= t_i, (t_i >= SB) & (s_i < SB)
    else:
        causal, cross = s_i >= t_i, (t_i < SB) & (s_i >= SB)
    scores = jnp.where(same & causal, s_d, jnp.where(cross, s_x, 0.0))
    o_intra = jnp.dot(scores.astype(BF16), vb, preferred_element_type=F32)
    a_out = a[C - 1:C] if forward else a[0:1]
    k_out = k * jnp.exp(a_out - a)
    upd = lax.dot_general(vb, k_out.astype(BF16), (((0,), (0,)), ((), ())),
                          preferred_element_type=F32)
    return o_intra, (q * jnp.exp(a)).astype(BF16), upd, jnp.exp(a_out)


def _hgrn_scan_kernel(q_ref, v_ref, kf_ref, lff_ref, kb_ref, lfb_ref, gate_ref, gain_ref,
                      o_ref, acc_ref, qe_ref, upd_ref, dec_ref, inter_ref, *, n_chunks, n_lat_chunks):
    C, G = SCAN_CHUNK, SCAN_GROUP
    R = C * G
    pos = lax.rem(lax.broadcasted_iota(jnp.int32, (R, HG_DIM), 0), C)

    def cumsum(lf, forward):
        x = lf
        step = 1
        while step < C:
            if forward:
                x = x + jnp.where(pos >= step, pltpu.roll(x, step, 0), 0.0)
            else:
                x = x + jnp.where(pos < C - step, pltpu.roll(x, R - step, 0), 0.0)
            step *= 2
        return x

    def local(g, carry):
        rows = pl.ds(pl.multiple_of(g * R, R), R)
        q = q_ref[0, rows, :]
        vb = v_ref[0, rows, :].astype(BF16)
        kf, kb = kf_ref[0, rows, :], kb_ref[0, rows, :]
        a_f = cumsum(lff_ref[0, rows, :], True)
        a_b = cumsum(lfb_ref[0, rows, :], False)
        res = []
        for ci in range(G):
            sl = slice(ci * C, (ci + 1) * C)
            res.append((_scan_chunk_local(q[sl], kf[sl], vb[sl], a_f[sl], True),
                        _scan_chunk_local(q[sl], kb[sl], vb[sl], a_b[sl], False)))
        for ci, (fw, bw) in enumerate(res):
            c = g * G + ci
            crow = pl.ds(pl.multiple_of(c * C, C), C)
            acc_ref[crow, :] = fw[0] + bw[0]
            qe_ref[0, crow, :] = fw[1]
            qe_ref[1, crow, :] = bw[1]
            upd_ref[0, c] = fw[2]
            upd_ref[1, c] = bw[2]
            dec_ref[0, pl.ds(c, 1), :] = fw[3]
            dec_ref[1, pl.ds(c, 1), :] = bw[3]
        return carry

    lax.fori_loop(0, n_chunks // G, local, 0)

    nt = (((1,), (1,)), ((), ()))

    def carry_state(j, carry):
        st_f, st_b = carry
        cf = lax.rem(j + n_lat_chunks, n_chunks)
        cb = n_chunks - 1 - j
        rf = pl.ds(pl.multiple_of(cf * C, C), C)
        rb = pl.ds(pl.multiple_of(cb * C, C), C)
        inter_ref[0, rf, :] = lax.dot_general(qe_ref[0, rf, :], st_f.astype(BF16), nt,
                                              preferred_element_type=F32)
        inter_ref[1, rb, :] = lax.dot_general(qe_ref[1, rb, :], st_b.astype(BF16), nt,
                                              preferred_element_type=F32)
        st_f = st_f * dec_ref[0, pl.ds(cf, 1), :] + upd_ref[0, cf]
        st_b = st_b * dec_ref[1, pl.ds(cb, 1), :] + upd_ref[1, cb]
        return st_f, st_b

    zero = jnp.zeros((HG_DIM, HG_DIM), F32)
    lax.fori_loop(0, n_chunks, carry_state, (zero, zero), unroll=2)
    o = acc_ref[...] + inter_ref[0] + inter_ref[1]
    o = o * lax.rsqrt(jnp.mean(o * o, axis=-1, keepdims=True) + EPS) * gain_ref[...]
    o_ref[0] = (o * _silu(gate_ref[0])).astype(BF16)


def _hgrn_scan(q, v, kf, lff, kb, lfb, gate, gain, t_lat):
    B, NT, HK = q.shape
    blk = pl.BlockSpec((1, NT, HG_DIM), lambda b, h: (b, 0, h))
    return pl.pallas_call(
        functools.partial(_hgrn_scan_kernel, n_chunks=NT // SCAN_CHUNK,
                          n_lat_chunks=t_lat // SCAN_CHUNK),
        grid=(B, HG_HEADS),
        in_specs=[blk] * 7 + [pl.BlockSpec((1, HG_DIM), lambda b, h: (0, 0))],
        out_specs=blk,
        out_shape=jax.ShapeDtypeStruct((B, NT, HK), BF16),
        scratch_shapes=[pltpu.VMEM((NT, HG_DIM), F32),
                        pltpu.VMEM((2, NT, HG_DIM), BF16),
                        pltpu.VMEM((2, NT // SCAN_CHUNK, HG_DIM, HG_DIM), F32),
                        pltpu.VMEM((2, NT // SCAN_CHUNK, HG_DIM), F32),
                        pltpu.VMEM((2, NT, HG_DIM), F32)],
        compiler_params=_cparams(("parallel", "arbitrary")),
        name="hgrn_scan",
    )(q, v, kf, lff, kb, lfb, gate, gain)


def _rope_tables(t_lat, t_ctx):
    rows = t_lat // GRID_W
    r = jnp.repeat(jnp.arange(rows, dtype=F32), GRID_W)
    col = jnp.tile(jnp.arange(GRID_W, dtype=F32), rows)
    n_pairs = DA_QK_DIM // 4
    inv = ROPE_BASE ** (-jnp.arange(n_pairs, dtype=F32) / n_pairs)
    ang = jnp.concatenate([r[:, None] * inv, col[:, None] * inv], axis=-1)
    cos, sin = jnp.cos(ang), jnp.sin(ang)
    cos_l = jnp.tile(cos, (1, LANES // cos.shape[1]))
    sin_l = jnp.tile(jnp.concatenate([-sin, sin], axis=-1), (1, LANES // (2 * sin.shape[1])))
    cos_l = jnp.concatenate([cos_l, jnp.ones((t_ctx, LANES), F32)], axis=0)
    sin_l = jnp.concatenate([sin_l, jnp.zeros((t_ctx, LANES), F32)], axis=0)
    return cos_l, sin_l


def kernel(x, c, ctx, c_ctx, ada_w, ada_b, norm_mix, norm_ffn, attn_w_qkv, attn_w_o, attn_q_norm,
           attn_k_norm, attn_sub_norm, attn_lambda, hgrn_w_in, hgrn_w_o, hgrn_out_norm,
           hgrn_lb_gamma, router_w, router_bias, moe_w_gate, moe_w_up, moe_w_down):
    B, T, D = x.shape
    Tc = ctx.shape[1]
    TB = TOKEN_BLOCK
    assert D == D_MODEL and T % TB == 0 and Tc % TB == 0 and T % GRID_W == 0
    assert ada_w.shape[0] == DEPTH == 2
    assert (T + Tc) % (SCAN_CHUNK * SCAN_GROUP) == 0 and T % SCAN_CHUNK == 0
    assert T % ATTN_Q_BLOCK == 0
    NT = T + Tc
    n_lat = T // TB
    n_tokens = B * NT
    ctx_row = B

    n_rows = -(-(B + 1) // 8) * 8
    cvec = jnp.concatenate([c, c_ctx[None, :], jnp.zeros((n_rows - B - 1, D), F32)], axis=0)
    mod = _modulation(cvec, ada_w, ada_b)

    xa = jnp.concatenate([x, ctx], axis=1)
    rwt = router_w.T
    rb = router_bias.reshape(N_EXPERTS, 1)

    p = jax.nn.softmax(hgrn_lb_gamma.astype(F32), axis=1)
    cum = jnp.cumsum(p, axis=1)
    lb_all = cum - cum[:, :1]

    cos_t, sin_t = _rope_tables(T, Tc)
    lane = jnp.arange(LANES)
    bd = (lane[:, None] // DA_QK_DIM == lane[None, :] // DA_QK_DIM).astype(BF16)

    for i in range(DEPTH):
        mod3 = mod[i].reshape(n_rows, 1, 6 * D)
        last = i == DEPTH - 1
        j = i // 2
        g_mix = norm_mix[i].reshape(1, D)
        if i % 2 == 0:
            lam_init = 0.8 - 0.6 * math.exp(-0.3 * i)
            q, k, v = _attn_project(
                xa, g_mix, mod3, attn_w_qkv[j].astype(BF16),
                jnp.tile(attn_q_norm[j], LANES // DA_QK_DIM).reshape(1, LANES),
                jnp.tile(attn_k_norm[j], LANES // DA_QK_DIM).reshape(1, LANES),
                cos_t, sin_t, bd, n_lat, ctx_row)
            o = _attention(attn_lambda[j], q, k, v, attn_sub_norm[j].reshape(1, DA_V_DIM),
                           T, lam_init)
            w_o = attn_w_o[j]
        else:
            parts = _hgrn_project(xa, g_mix, mod3, hgrn_w_in[j].astype(BF16), lb_all[:, i],
                                  n_lat, ctx_row)
            o = _hgrn_scan(*parts, hgrn_out_norm[j].reshape(1, HG_DIM), T)
            w_o = hgrn_w_o[j]
        xa, f_ext, bucket = _out_router(
            o, xa, w_o.astype(BF16), mod3, norm_ffn[i].reshape(1, D), rwt, rb, n_lat, ctx_row)
        w_gu = jnp.concatenate([moe_w_gate[i], moe_w_up[i]], axis=-1).astype(BF16)
        tables = _routing_tables(bucket.reshape(n_tokens), n_tokens)
        y = _expert_ffn(tables, f_ext.reshape(n_tokens, D + LANES), w_gu, moe_w_down[i].astype(BF16))
        xa = _moe_combine(xa, mod3, y, n_lat, ctx_row, n_lat if last else NT // TB)
    return xa
```

```python
import functools
import math

import jax
import jax.numpy as jnp
from jax import lax
from jax.experimental import pallas as pl
from jax.experimental.pallas import tpu as pltpu

F32 = jnp.float32
BF16 = jnp.bfloat16
HIGHEST = lax.Precision.HIGHEST

D_MODEL = 1024
DEPTH = 2
GRID_W = 64
DA_HEADS = 8
DA_QK_DIM = 64
DA_V_DIM = 128
ROPE_BASE = 10000.0
HG_HEADS = 8
HG_DIM = 128
N_EXPERTS = 16
N_GROUPS = 4
EXPERTS_PER_GROUP = 4
D_FF = 512
EPS = 1e-6

LANES = 128
TOKEN_BLOCK = 256
ATTN_Q_BLOCK = 256
ATTN_GROUP = 4
MOE_TILE = 256
MOE_DMA_UNROLL = 32
PAIRS = [(i, j) for i in range(EXPERTS_PER_GROUP) for j in range(i + 1, EXPERTS_PER_GROUP)]
N_BUCKETS = N_GROUPS * len(PAIRS)
SCAN_CHUNK = 64
SCAN_SUB = SCAN_CHUNK // 2
SCAN_GROUP = 4
EXP_CLAMP = 80.0
VMEM_LIMIT = 56 * 1024 * 1024


def _cparams(sem):
    return pltpu.CompilerParams(dimension_semantics=sem, vmem_limit_bytes=VMEM_LIMIT)


def _silu(x):
    return x * jax.nn.sigmoid(x)


def _norm_mod(x, g, sc, sh):
    y = x * lax.rsqrt(jnp.mean(x * x, axis=-1, keepdims=True) + EPS)
    return (y * g) * (1.0 + sc) + sh


def _mod_kernel(c_ref, w_ref, b_ref, o_ref):
    o_ref[0] = jnp.dot(_silu(c_ref[...]), w_ref[0], preferred_element_type=F32,
                       precision=HIGHEST) + b_ref[0]


def _modulation(cvec, ada_w, ada_b):
    R, D = cvec.shape
    depth, _, n6 = ada_w.shape
    tn = 1024
    return pl.pallas_call(
        _mod_kernel,
        grid=(depth, n6 // tn),
        in_specs=[
            pl.BlockSpec((R, D), lambda i, j: (0, 0)),
            pl.BlockSpec((1, D, tn), lambda i, j: (i, 0, j)),
            pl.BlockSpec((1, 1, tn), lambda i, j: (i, 0, j)),
        ],
        out_specs=pl.BlockSpec((1, R, tn), lambda i, j: (i, 0, j)),
        out_shape=jax.ShapeDtypeStruct((depth, R, n6), F32),
        compiler_params=_cparams(("arbitrary", "arbitrary")),
        name="adaln_mod",
    )(cvec, ada_w, ada_b.reshape(depth, 1, n6))


def _mod_spec(col, n_lat, ctx_row):
    return pl.BlockSpec((1, 1, D_MODEL), lambda b, t: (jnp.where(t < n_lat, b, ctx_row), 0, col))


def _attn_proj_kernel(x_ref, g_ref, sc_ref, sh_ref, w_ref, qg_ref, kg_ref, cos_ref, sin_ref,
                      bd_ref, q_ref, k_ref, v_ref):
    D = D_MODEL
    h = _norm_mod(x_ref[0], g_ref[...], sc_ref[0], sh_ref[0])
    qkv = jnp.dot(h.astype(BF16), w_ref[...], preferred_element_type=F32)
    cos, sin, bd = cos_ref[...], sin_ref[...], bd_ref[...]
    lane = lax.broadcasted_iota(jnp.int32, cos.shape, 1)
    upper = (lane & (DA_QK_DIM // 2)) != 0

    def norm_rope(t, gain, scale):
        sq = t * t
        sq_hi = sq.astype(BF16)
        sq_lo = (sq - sq_hi.astype(F32)).astype(BF16)
        ss = (jnp.dot(sq_hi, bd, preferred_element_type=F32)
              + jnp.dot(sq_lo, bd, preferred_element_type=F32))
        tn = t * lax.rsqrt(ss * (1.0 / DA_QK_DIM) + EPS) * gain
        partner = jnp.where(upper, pltpu.roll(tn, DA_QK_DIM // 2, 1),
                            pltpu.roll(tn, LANES - DA_QK_DIM // 2, 1))
        return (tn * cos + partner * sin) * scale

    for j in range(D // LANES):
        sl = slice(j * LANES, (j + 1) * LANES)
        q_ref[0, :, sl] = norm_rope(qkv[:, j * LANES:(j + 1) * LANES], qg_ref[...],
                                    math.log2(math.e) / math.sqrt(DA_QK_DIM)).astype(BF16)
        k_ref[0, j] = norm_rope(qkv[:, D + j * LANES:D + (j + 1) * LANES], kg_ref[...],
                                1.0).T.astype(BF16)
    v_ref[0] = qkv[:, 2 * D:].astype(BF16)


def _attn_project(xa, g, mod3, w_qkv, qg, kg, cos_t, sin_t, bd, n_lat, ctx_row):
    B, NT, D = xa.shape
    TB = TOKEN_BLOCK
    tok = pl.BlockSpec((1, TB, D), lambda b, t: (b, t, 0))
    const2 = lambda shape: pl.BlockSpec(shape, lambda b, t: (0, 0))
    out = jax.ShapeDtypeStruct((B, NT, D), BF16)
    return pl.pallas_call(
        _attn_proj_kernel,
        grid=(B, NT // TB),
        in_specs=[
            tok, const2((1, D)), _mod_spec(1, n_lat, ctx_row), _mod_spec(0, n_lat, ctx_row),
            const2((D, 3 * D)), const2((1, LANES)), const2((1, LANES)),
            pl.BlockSpec((TB, LANES), lambda b, t: (t, 0)),
            pl.BlockSpec((TB, LANES), lambda b, t: (t, 0)),
            const2((LANES, LANES)),
        ],
        out_specs=[tok, pl.BlockSpec((1, DA_HEADS, LANES, TB), lambda b, t: (b, 0, 0, t)), tok],
        out_shape=[out, jax.ShapeDtypeStruct((B, DA_HEADS, LANES, NT), BF16), out],
        compiler_params=_cparams(("parallel", "arbitrary")),
        name="attn_qkv_proj",
    )(xa, g, mod3, mod3, w_qkv, qg, kg, cos_t, sin_t, bd)


def _attn_kernel(lam_ref, q_ref, kt_ref, v_ref, sn_ref, o_ref, *, t_lat, lam_init):
    QB = ATTN_Q_BLOCK
    lp = lam_ref[...]
    lam = (jnp.exp(jnp.sum(lp[0:1] * lp[1:2], keepdims=True))
           - jnp.exp(jnp.sum(lp[2:3] * lp[3:4], keepdims=True)) + lam_init)

    def scores(q, kt):
        lane = lax.broadcasted_iota(jnp.int32, q.shape, 1)
        zero = jnp.zeros_like(q)
        return (jnp.dot(jnp.where(lane < DA_QK_DIM, q, zero), kt, preferred_element_type=F32),
                jnp.dot(jnp.where(lane >= DA_QK_DIM, q, zero), kt, preferred_element_type=F32))

    def finish(s, v):
        def softmax_parts(sm):
            e = jnp.exp2(sm - jnp.max(sm, axis=-1, keepdims=True))
            return e, jnp.sum(e, axis=-1, keepdims=True)

        e0, l0 = softmax_parts(s[0])
        e1, l1 = softmax_parts(s[1])
        a = e0 - (lam * l0 / l1) * e1
        o = jnp.dot(a.astype(BF16), v, preferred_element_type=F32) * (1.0 / l0)
        o = o * lax.rsqrt(jnp.mean(o * o, axis=-1, keepdims=True) + EPS) * sn_ref[...]
        return (o * (1.0 - lam_init)).astype(BF16)

    G = math.gcd(ATTN_GROUP, t_lat // QB)

    def latent_group(i, carry):
        rows = [pl.ds(pl.multiple_of((G * i + j) * QB, QB), QB) for j in range(G)]
        s_next = scores(q_ref[0, rows[0], :], kt_ref[0, 0])
        for j in range(G):
            s_cur = s_next
            if j + 1 < G:
                s_next = scores(q_ref[0, rows[j + 1], :], kt_ref[0, 0])
            o_ref[0, rows[j], :] = finish(s_cur, v_ref[0])
        return carry

    lax.fori_loop(0, t_lat // (G * QB), latent_group, 0)
    o_ref[0, t_lat:, :] = finish(scores(q_ref[0, t_lat:, :], kt_ref[0, 0, :, t_lat:]),
                                 v_ref[0, t_lat:, :])


def _attention(lam_p, q, kt, v, sub_norm, t_lat, lam_init):
    B, NT, D = q.shape
    blk = pl.BlockSpec((1, NT, LANES), lambda b, h: (b, 0, h))
    return pl.pallas_call(
        functools.partial(_attn_kernel, t_lat=t_lat, lam_init=lam_init),
        grid=(B, DA_HEADS),
        in_specs=[
            pl.BlockSpec(lam_p.shape, lambda b, h: (0, 0)),
            blk, pl.BlockSpec((1, 1, LANES, NT), lambda b, h: (b, h, 0, 0)), blk,
            pl.BlockSpec((1, LANES), lambda b, h: (0, 0)),
        ],
        out_specs=blk,
        out_shape=jax.ShapeDtypeStruct((B, NT, D), BF16),
        compiler_params=_cparams(("parallel", "arbitrary")),
        name="diff_attention",
    )(lam_p, q, kt, v, sub_norm)


def _out_router_kernel(o_ref, x_ref, w_ref, gm_ref, g_ref, sc_ref, sh_ref, rw_ref, rb_ref,
                       xo_ref, f_ref, bk_ref):
    D = D_MODEL
    out = jnp.dot(o_ref[0], w_ref[...], preferred_element_type=F32)
    x = x_ref[0] + gm_ref[0] * out
    xo_ref[0] = x
    f = _norm_mod(x, g_ref[...], sc_ref[0], sh_ref[0])
    f_ref[0, :, :D] = f
    f_hi = f.astype(BF16)
    f_lo = (f - f_hi.astype(F32)).astype(BF16)
    rw_hi, rw_lo = rw_ref[0], rw_ref[1]
    logits_t = (jnp.dot(f_hi, rw_hi, preferred_element_type=F32)
                + (jnp.dot(f_lo, rw_hi, preferred_element_type=F32)
                   + jnp.dot(f_hi, rw_lo, preferred_element_type=F32)))
    logits = logits_t.T[:N_EXPERTS]
    aff = jax.nn.sigmoid(logits)
    biased = aff + rb_ref[...]
    G, E = EXPERTS_PER_GROUP, N_EXPERTS
    row = lax.broadcasted_iota(jnp.int32, biased.shape, 0)
    member = lax.rem(row, G)
    group = row // G

    def shifted(x, k):
        return jnp.where(member + k < G, pltpu.roll(x, E - k, 0), pltpu.roll(x, G - k, 0))

    rank = jnp.zeros_like(row)
    for k in range(1, G):
        other = shifted(biased, k)
        ahead = (other > biased) | ((other == biased) & (member + k >= G))
        rank = rank + ahead.astype(jnp.int32)
    top2 = rank < 2
    t = jnp.where(top2, biased, 0.0)
    gscore = t
    for k in range(1, G):
        gscore = gscore + shifted(t, k)
    best = jnp.ones_like(top2)
    for m in range(1, N_GROUPS):
        other = pltpu.roll(gscore, G * m, 0)
        best = best & ((gscore > other) | ((gscore == other) & (group < m)))
    chosen = top2 & best
    lowest = jnp.min(jnp.where(chosen, row, E), axis=0, keepdims=True)
    is_lo = chosen & (row == lowest)
    is_hi = chosen & (row != lowest)
    a_lo = jnp.sum(jnp.where(is_lo, aff, 0.0), axis=0, keepdims=True)
    a_hi = jnp.sum(jnp.where(is_hi, aff, 0.0), axis=0, keepdims=True)
    m_lo = lax.rem(lowest, G)
    m_hi = jnp.sum(jnp.where(is_hi, member, 0), axis=0, keepdims=True)
    pair = m_lo * (2 * G - 1 - m_lo) // 2 + (m_hi - m_lo - 1)
    tot = a_lo + a_hi
    bk_ref[0] = (lowest // G) * len(PAIRS) + pair

    tb = x.shape[0]
    eye = (lax.broadcasted_iota(jnp.int32, (tb, tb), 0)
           == lax.broadcasted_iota(jnp.int32, (tb, tb), 1))

    def column(row):
        return jnp.sum(jnp.where(eye, row, 0.0), axis=1, keepdims=True)

    lane = lax.broadcasted_iota(jnp.int32, (tb, LANES), 1)
    f_ref[0, :, D:] = jnp.where(lane < LANES // 2, column(a_lo / tot), column(a_hi / tot))


def _out_router(o, xa, w_o, mod3, g_ffn, rwt, rb, n_lat, ctx_row):
    B, NT, D = xa.shape
    TB = TOKEN_BLOCK
    nb = NT // TB
    tok = pl.BlockSpec((1, TB, D), lambda b, t: (b, t, 0))
    row = pl.BlockSpec((1, 1, TB), lambda b, t: (b * nb + t, 0, 0))
    const2 = lambda shape: pl.BlockSpec(shape, lambda b, t: (0, 0))
    rows = lambda dt: jax.ShapeDtypeStruct((B * nb, 1, TB), dt)
    return pl.pallas_call(
        _out_router_kernel,
        grid=(B, nb),
        in_specs=[
            tok, tok, const2((D, D)), _mod_spec(2, n_lat, ctx_row), const2((1, D)),
            _mod_spec(4, n_lat, ctx_row), _mod_spec(3, n_lat, ctx_row),
            pl.BlockSpec((2, D, LANES), lambda b, t: (0, 0, 0)), const2((N_EXPERTS, 1)),
        ],
        out_specs=[tok, pl.BlockSpec((1, TB, D + LANES), lambda b, t: (b, t, 0)), row],
        out_shape=[jax.ShapeDtypeStruct((B, NT, D), F32),
                   jax.ShapeDtypeStruct((B, NT, D + LANES), F32), rows(jnp.int32)],
        compiler_params=_cparams(("parallel", "arbitrary")),
        name="out_proj_router",
    )(o, xa, w_o, mod3, g_ffn, mod3, mod3, rwt, rb)


def _routing_tables(bucket, n_tokens):
    tm = MOE_TILE
    max_tiles = n_tokens // tm + N_BUCKETS
    n_rows = max_tiles * tm
    onehot = (bucket[:, None] == jnp.arange(N_BUCKETS, dtype=jnp.int32)[None, :]).astype(jnp.int32)
    count = jnp.sum(onehot, axis=0)
    rank = jnp.sum(jnp.cumsum(onehot, axis=0) * onehot, axis=1) - 1
    btiles = (count + tm - 1) // tm
    tile_end = jnp.cumsum(btiles)
    tile_start = tile_end - btiles
    dest = tile_start[bucket] * tm + rank
    token_at = jnp.zeros((n_rows,), jnp.int32).at[dest].set(jnp.arange(n_tokens, dtype=jnp.int32))
    tile = jnp.arange(max_tiles, dtype=jnp.int32)
    tile_bucket = jnp.minimum(jnp.sum((tile[:, None] >= tile_end[None, :]).astype(jnp.int32), axis=1),
                              N_BUCKETS - 1)
    in_bucket = tile - tile_start[tile_bucket]
    n_valid = jnp.clip(count[tile_bucket] - in_bucket * tm, 0, tm)
    n_valid = jnp.where(tile < tile_end[-1], n_valid, 0).astype(jnp.int32)
    grp = tile_bucket // len(PAIRS)
    pair = tile_bucket % len(PAIRS)
    lo_tab = jnp.array([p[0] for p in PAIRS], jnp.int32)
    hi_tab = jnp.array([p[1] for p in PAIRS], jnp.int32)
    tile_lo = grp * EXPERTS_PER_GROUP + lo_tab[pair]
    tile_hi = grp * EXPERTS_PER_GROUP + hi_tab[pair]
    return token_at, n_valid, tile_lo, tile_hi


def _expert_kernel(tok_ref, nv_ref, tlo_ref, thi_ref, f_hbm, gu_lo_ref, dn_lo_ref, gu_hi_ref,
                   dn_hi_ref, y_hbm, xbuf, ybuf, gsem, ssem):
    TM, D = MOE_TILE, D_MODEL
    i = pl.program_id(0)
    n = pl.num_programs(0)
    slot = lax.rem(i, 2)
    other = 1 - slot

    def gather_copy(tile, r, s):
        return pltpu.make_async_copy(f_hbm.at[pl.ds(tok_ref[tile * TM + r], 1), :],
                                     xbuf.at[s, pl.ds(r, 1), :], gsem.at[s])

    def scatter_copy(tile, r, s):
        return pltpu.make_async_copy(ybuf.at[s, pl.ds(r, 1), :],
                                     y_hbm.at[pl.ds(tok_ref[tile * TM + r], 1), :], ssem.at[s])

    def for_rows(n_rows, fn):
        U = MOE_DMA_UNROLL
        n_groups = n_rows // U

        def group(g, carry):
            base = pl.multiple_of(g * U, U)
            for u in range(U):
                fn(base + u)
            return carry

        def single(r, carry):
            fn(r)
            return carry

        lax.fori_loop(0, n_groups, group, 0)
        lax.fori_loop(n_groups * U, n_rows, single, 0)

    def wait_rows(n_rows, whole_tile_copy, row_copy):
        @pl.when(n_rows == TM)
        def _():
            whole_tile_copy.wait()

        @pl.when(n_rows < TM)
        def _():
            for_rows(n_rows, lambda r: row_copy(r).wait())

    def start_gather(tile, s):
        for_rows(nv_ref[tile], lambda r: gather_copy(tile, r, s).start())

    def wait_gather(tile, s):
        wait_rows(nv_ref[tile],
                  pltpu.make_async_copy(f_hbm.at[pl.ds(0, TM), :], xbuf.at[s], gsem.at[s]),
                  lambda r: gather_copy(tile, r, s))

    def start_scatter(tile, s):
        for_rows(nv_ref[tile], lambda r: scatter_copy(tile, r, s).start())

    def wait_scatter(tile, s):
        wait_rows(nv_ref[tile],
                  pltpu.make_async_copy(ybuf.at[s], y_hbm.at[pl.ds(0, TM), :], ssem.at[s]),
                  lambda r: scatter_copy(tile, r, s))

    @pl.when(i == 0)
    def _():
        xbuf[...] = jnp.zeros_like(xbuf)
        start_gather(0, 0)

    wait_gather(i, slot)

    @pl.when(i >= 2)
    def _():
        wait_scatter(i - 2, slot)

    @pl.when(i + 1 < n)
    def _():
        start_gather(i + 1, other)

    @pl.when(nv_ref[i] > 0)
    def _():
        xe = xbuf[slot]
        x = xe[:, :D].astype(BF16)

        def ffn(gu_ref, dn_ref):
            gu = jnp.dot(x, gu_ref[0], preferred_element_type=F32)
            hid = _silu(gu[:, :D_FF]) * gu[:, D_FF:]
            return jnp.dot(hid.astype(BF16), dn_ref[0], preferred_element_type=F32)

        ybuf[slot] = (xe[:, D:D + 1] * ffn(gu_lo_ref, dn_lo_ref)
                      + xe[:, D + LANES // 2:D + LANES // 2 + 1] * ffn(gu_hi_ref, dn_hi_ref))
        start_scatter(i, slot)

    @pl.when(i == n - 1)
    def _():
        wait_scatter(i - 1, other)
        wait_scatter(i, slot)


def _expert_ffn(tables, f_ext, w_gu, w_dn):
    token_at, n_valid, tile_lo, tile_hi = tables
    N = f_ext.shape[0]
    D = D_MODEL
    tm = MOE_TILE
    lo3 = lambda i, tok, nv, tlo, thi: (tlo[i], 0, 0)
    hi3 = lambda i, tok, nv, tlo, thi: (thi[i], 0, 0)
    return pl.pallas_call(
        _expert_kernel,
        grid_spec=pltpu.PrefetchScalarGridSpec(
            num_scalar_prefetch=4,
            grid=(n_valid.shape[0],),
            in_specs=[
                pl.BlockSpec(memory_space=pl.ANY),
                pl.BlockSpec((1, D, 2 * D_FF), lo3), pl.BlockSpec((1, D_FF, D), lo3),
                pl.BlockSpec((1, D, 2 * D_FF), hi3), pl.BlockSpec((1, D_FF, D), hi3),
            ],
            out_specs=pl.BlockSpec(memory_space=pl.ANY),
            scratch_shapes=[pltpu.VMEM((2, tm, D + LANES), F32), pltpu.VMEM((2, tm, D), F32),
                            pltpu.SemaphoreType.DMA((2,)), pltpu.SemaphoreType.DMA((2,))],
        ),
        out_shape=jax.ShapeDtypeStruct((N, D), F32),
        compiler_params=_cparams(("arbitrary",)),
        name="moe_expert_ffn",
    )(token_at, n_valid, tile_lo, tile_hi, f_ext, w_gu, w_dn, w_gu, w_dn)


def _combine_kernel(x_ref, gf_ref, y_ref, o_ref):
    o_ref[0] = x_ref[0] + gf_ref[0] * y_ref[...]


def _moe_combine(xa, mod3, y, n_lat, ctx_row, n_blocks_out):
    B, NT, D = xa.shape
    TB = TOKEN_BLOCK
    nb = NT // TB
    tok = pl.BlockSpec((1, TB, D), lambda b, t: (b, t, 0))
    return pl.pallas_call(
        _combine_kernel,
        grid=(B, n_blocks_out),
        in_specs=[tok, _mod_spec(5, n_lat, ctx_row),
                  pl.BlockSpec((TB, D), lambda b, t: (b * nb + t, 0))],
        out_specs=tok,
        out_shape=jax.ShapeDtypeStruct((B, n_blocks_out * TB, D), F32),
        compiler_params=_cparams(("parallel", "arbitrary")),
        name="moe_combine",
    )(xa, mod3, y)


def _hgrn_proj_kernel(x_ref, g_ref, sc_ref, sh_ref, w_ref, lb_ref,
                      q_ref, v_ref, kf_ref, lff_ref, kb_ref, lfb_ref, gate_ref):
    HK = HG_HEADS * HG_DIM
    h = _norm_mod(x_ref[0], g_ref[...], sc_ref[0], sh_ref[0])
    y = jnp.dot(h.astype(BF16), w_ref[...], preferred_element_type=F32)
    q_ref[0] = _silu(y[:, :HK])
    v_ref[0] = y[:, HK:2 * HK].astype(BF16)
    gate_ref[0] = y[:, 4 * HK:]

    def forget(z, lbd, k_ref, lf_ref):
        e = jnp.exp(-jnp.abs(z))
        t = 1.0 + e
        k_ref[0] = (1.0 - lbd) * (jnp.where(z >= 0.0, e, 1.0) / t)
        a = jnp.log(lbd)
        b = jnp.log1p(-lbd) + (jnp.minimum(z, 0.0) - jnp.log(t))
        lf_ref[0] = jnp.maximum(a, b) + jnp.log(1.0 + jnp.exp(-jnp.abs(a - b)))

    forget(y[:, 2 * HK:3 * HK], lb_ref[0:1], kf_ref, lff_ref)
    forget(y[:, 3 * HK:4 * HK], lb_ref[1:2], kb_ref, lfb_ref)


def _hgrn_project(xa, g, mod3, w_in, lb, n_lat, ctx_row):
    B, NT, D = xa.shape
    TB = TOKEN_BLOCK
    HK = HG_HEADS * HG_DIM
    tok = pl.BlockSpec((1, TB, D), lambda b, t: (b, t, 0))
    tok_o = pl.BlockSpec((1, TB, HK), lambda b, t: (b, t, 0))
    const2 = lambda shape: pl.BlockSpec(shape, lambda b, t: (0, 0))
    out = jax.ShapeDtypeStruct((B, NT, HK), F32)
    return pl.pallas_call(
        _hgrn_proj_kernel,
        grid=(B, NT // TB),
        in_specs=[
            tok, const2((1, D)), _mod_spec(1, n_lat, ctx_row), _mod_spec(0, n_lat, ctx_row),
            pl.BlockSpec((D, 5 * HK), lambda b, t: (0, 0), pipeline_mode=pl.Buffered(1)),
            const2((2, HK)),
        ],
        out_specs=[tok_o] * 7,
        out_shape=[out, jax.ShapeDtypeStruct((B, NT, HK), BF16)] + [out] * 5,
        compiler_params=_cparams(("parallel", "arbitrary")),
        name="hgrn_proj",
    )(xa, g, mod3, mod3, w_in, lb)


def _scan_chunk_local(q, k, vb, a, forward):
    C, SB = SCAN_CHUNK, SCAN_SUB
    row = lax.broadcasted_iota(jnp.int32, a.shape, 0)
    first = row < SB
    mid_row = SB // 2
    m = jnp.where(first, a[mid_row:mid_row + 1], a[SB + mid_row:SB + mid_row + 1])
    qd = q * jnp.exp(jnp.minimum(a - m, EXP_CLAMP))
    kd = k * jnp.exp(jnp.minimum(m - a, EXP_CLAMP))
    edge = a[SB - 1:SB] if forward else a[SB:SB + 1]
    e_x = jnp.exp(-jnp.abs(a - edge))
    nt = (((1,), (1,)), ((), ()))
    s_d = lax.dot_general(qd.astype(BF16), kd.astype(BF16), nt, preferred_element_type=F32)
    s_x = lax.dot_general((q * e_x).astype(BF16), (k * e_x).astype(BF16), nt,
                          preferred_element_type=F32)
    t_i = lax.broadcasted_iota(jnp.int32, s_d.shape, 0)
    s_i = lax.broadcasted_iota(jnp.int32, s_d.shape, 1)
    same = (t_i < SB) == (s_i < SB)
    if forward:
        causal, cross = s_i <= t_i, (t_i >= SB) & (s_i < SB)
    else:
        causal, cross = s_i >= t_i, (t_i < SB) & (s_i >= SB)
    scores = jnp.where(same & causal, s_d, jnp.where(cross, s_x, 0.0))
    o_intra = jnp.dot(scores.astype(BF16), vb, preferred_element_type=F32)
    a_out = a[C - 1:C] if forward else a[0:1]
    k_out = k * jnp.exp(a_out - a)
    upd = lax.dot_general(vb, k_out.astype(BF16), (((0,), (0,)), ((), ())),
                          preferred_element_type=F32)
    return o_intra, q * jnp.exp(a), upd, jnp.exp(a_out)


def _hgrn_scan_kernel(q_ref, v_ref, kf_ref, lff_ref, kb_ref, lfb_ref, gate_ref, gain_ref,
                      o_ref, acc_ref, qe_ref, upd_ref, dec_ref, inter_ref, *, n_chunks, n_lat_chunks):
    C, G = SCAN_CHUNK, SCAN_GROUP
    R = C * G
    pos = lax.rem(lax.broadcasted_iota(jnp.int32, (R, HG_DIM), 0), C)

    def cumsum(lf, forward):
        x = lf
        step = 1
        while step < C:
            if forward:
                x = x + jnp.where(pos >= step, pltpu.roll(x, step, 0), 0.0)
            else:
                x = x + jnp.where(pos < C - step, pltpu.roll(x, R - step, 0), 0.0)
            step *= 2
        return x

    def local(g, carry):
        rows = pl.ds(pl.multiple_of(g * R, R), R)
        q = q_ref[0, rows, :]
        vb = v_ref[0, rows, :].astype(BF16)
        kf, kb = kf_ref[0, rows, :], kb_ref[0, rows, :]
        a_f = cumsum(lff_ref[0, rows, :], True)
        a_b = cumsum(lfb_ref[0, rows, :], False)
        res = []
        for ci in range(G):
            sl = slice(ci * C, (ci + 1) * C)
            res.append((_scan_chunk_local(q[sl], kf[sl], vb[sl], a_f[sl], True),
                        _scan_chunk_local(q[sl], kb[sl], vb[sl], a_b[sl], False)))
        o_sum = [fw[0] + bw[0] for fw, bw in res]
        for d in range(2):
            p = l = None
            for ci in (range(G) if d == 0 else reversed(range(G))):
                _, qe, upd, dec = res[ci][d]
                crow = pl.ds(pl.multiple_of((g * G + ci) * C, C), C)
                if p is None:
                    qe_ref[d, crow, :] = qe.astype(BF16)
                    p, l = dec, upd
                else:
                    qe_ref[d, crow, :] = (qe * p).astype(BF16)
                    o_sum[ci] = o_sum[ci] + lax.dot_general(qe.astype(BF16), l.astype(BF16), nt,
                                                            preferred_element_type=F32)
                    p, l = p * dec, l * dec + upd
            upd_ref[d, g] = l
            dec_ref[d, pl.ds(g, 1), :] = p
        for ci in range(G):
            acc_ref[pl.ds(pl.multiple_of((g * G + ci) * C, C), C), :] = o_sum[ci]
        return carry

    nt = (((1,), (1,)), ((), ()))
    n_groups = n_chunks // G
    lax.fori_loop(0, n_groups, local, 0)

    def carry_state(j, carry):
        st_f, st_b = carry
        gf = lax.rem(j + n_lat_chunks // G, n_groups)
        gb = n_groups - 1 - j
        rf = pl.ds(pl.multiple_of(gf * R, R), R)
        rb = pl.ds(pl.multiple_of(gb * R, R), R)
        inter_ref[0, rf, :] = lax.dot_general(qe_ref[0, rf, :], st_f.astype(BF16), nt,
                                              preferred_element_type=F32)
        inter_ref[1, rb, :] = lax.dot_general(qe_ref[1, rb, :], st_b.astype(BF16), nt,
                                              preferred_element_type=F32)
        st_f = st_f * dec_ref[0, pl.ds(gf, 1), :] + upd_ref[0, gf]
        st_b = st_b * dec_ref[1, pl.ds(gb, 1), :] + upd_ref[1, gb]
        return st_f, st_b

    zero = jnp.zeros((HG_DIM, HG_DIM), F32)
    lax.fori_loop(0, n_groups, carry_state, (zero, zero))
    o = acc_ref[...] + inter_ref[0] + inter_ref[1]
    o = o * lax.rsqrt(jnp.mean(o * o, axis=-1, keepdims=True) + EPS) * gain_ref[...]
    o_ref[0] = (o * _silu(gate_ref[0])).astype(BF16)


def _hgrn_scan(q, v, kf, lff, kb, lfb, gate, gain, t_lat):
    B, NT, HK = q.shape
    blk = pl.BlockSpec((1, NT, HG_DIM), lambda b, h: (b, 0, h))
    return pl.pallas_call(
        functools.partial(_hgrn_scan_kernel, n_chunks=NT // SCAN_CHUNK,
                          n_lat_chunks=t_lat // SCAN_CHUNK),
        grid=(B, HG_HEADS),
        in_specs=[blk] * 7 + [pl.BlockSpec((1, HG_DIM), lambda b, h: (0, 0))],
        out_specs=blk,
        out_shape=jax.ShapeDtypeStruct((B, NT, HK), BF16),
        scratch_shapes=[pltpu.VMEM((NT, HG_DIM), F32),
                        pltpu.VMEM((2, NT, HG_DIM), BF16),
                        pltpu.VMEM((2, NT // (SCAN_CHUNK * SCAN_GROUP), HG_DIM, HG_DIM), F32),
                        pltpu.VMEM((2, NT // (SCAN_CHUNK * SCAN_GROUP), HG_DIM), F32),
                        pltpu.VMEM((2, NT, HG_DIM), F32)],
        compiler_params=_cparams(("parallel", "arbitrary")),
        name="hgrn_scan",
    )(q, v, kf, lff, kb, lfb, gate, gain)


def _rope_tables(t_lat, t_ctx):
    rows = t_lat // GRID_W
    r = jnp.repeat(jnp.arange(rows, dtype=F32), GRID_W)
    col = jnp.tile(jnp.arange(GRID_W, dtype=F32), rows)
    n_pairs = DA_QK_DIM // 4
    inv = ROPE_BASE ** (-jnp.arange(n_pairs, dtype=F32) / n_pairs)
    ang = jnp.concatenate([r[:, None] * inv, col[:, None] * inv], axis=-1)
    cos, sin = jnp.cos(ang), jnp.sin(ang)
    cos_l = jnp.tile(cos, (1, LANES // cos.shape[1]))
    sin_l = jnp.tile(jnp.concatenate([-sin, sin], axis=-1), (1, LANES // (2 * sin.shape[1])))
    cos_l = jnp.concatenate([cos_l, jnp.ones((t_ctx, LANES), F32)], axis=0)
    sin_l = jnp.concatenate([sin_l, jnp.zeros((t_ctx, LANES), F32)], axis=0)
    return cos_l, sin_l


def kernel(x, c, ctx, c_ctx, ada_w, ada_b, norm_mix, norm_ffn, attn_w_qkv, attn_w_o, attn_q_norm,
           attn_k_norm, attn_sub_norm, attn_lambda, hgrn_w_in, hgrn_w_o, hgrn_out_norm,
           hgrn_lb_gamma, router_w, router_bias, moe_w_gate, moe_w_up, moe_w_down):
    B, T, D = x.shape
    Tc = ctx.shape[1]
    TB = TOKEN_BLOCK
    assert D == D_MODEL and T % TB == 0 and Tc % TB == 0 and T % GRID_W == 0
    assert ada_w.shape[0] == DEPTH == 2
    assert T % (SCAN_CHUNK * SCAN_GROUP) == 0 and Tc % (SCAN_CHUNK * SCAN_GROUP) == 0
    assert T % ATTN_Q_BLOCK == 0
    NT = T + Tc
    n_lat = T // TB
    n_tokens = B * NT
    ctx_row = B

    n_rows = -(-(B + 1) // 8) * 8
    cvec = jnp.concatenate([c, c_ctx[None, :], jnp.zeros((n_rows - B - 1, D), F32)], axis=0)
    mod = _modulation(cvec, ada_w, ada_b)

    xa = jnp.concatenate([x, ctx], axis=1)
    rw_pad = jnp.pad(router_w.astype(F32), ((0, 0), (0, LANES - N_EXPERTS)))
    rw_hi = rw_pad.astype(BF16)
    rwt = jnp.stack([rw_hi, (rw_pad - rw_hi.astype(F32)).astype(BF16)])
    rb = router_bias.reshape(N_EXPERTS, 1)

    p = jax.nn.softmax(hgrn_lb_gamma.astype(F32), axis=1)
    cum = jnp.cumsum(p, axis=1)
    lb_all = cum - cum[:, :1]

    cos_t, sin_t = _rope_tables(T, Tc)
    lane = jnp.arange(LANES)
    bd = (lane[:, None] // DA_QK_DIM == lane[None, :] // DA_QK_DIM).astype(BF16)

    for i in range(DEPTH):
        mod3 = mod[i].reshape(n_rows, 1, 6 * D)
        last = i == DEPTH - 1
        j = i // 2
        g_mix = norm_mix[i].reshape(1, D)
        if i % 2 == 0:
            lam_init = 0.8 - 0.6 * math.exp(-0.3 * i)
            q, k, v = _attn_project(
                xa, g_mix, mod3, attn_w_qkv[j].astype(BF16),
                jnp.tile(attn_q_norm[j], LANES // DA_QK_DIM).reshape(1, LANES),
                jnp.tile(attn_k_norm[j], LANES // DA_QK_DIM).reshape(1, LANES),
                cos_t, sin_t, bd, n_lat, ctx_row)
            o = _attention(attn_lambda[j], q, k, v, attn_sub_norm[j].reshape(1, DA_V_DIM),
                           T, lam_init)
            w_o = attn_w_o[j]
        else:
            parts = _hgrn_project(xa, g_mix, mod3, hgrn_w_in[j].astype(BF16), lb_all[:, i],
                                  n_lat, ctx_row)
            o = _hgrn_scan(*parts, hgrn_out_norm[j].reshape(1, HG_DIM), T)
            w_o = hgrn_w_o[j]
        xa, f_ext, bucket = _out_router(
            o, xa, w_o.astype(BF16), mod3, norm_ffn[i].reshape(1, D), rwt, rb, n_lat, ctx_row)
        w_gu = jnp.concatenate([moe_w_gate[i], moe_w_up[i]], axis=-1).astype(BF16)
        tables = _routing_tables(bucket.reshape(n_tokens), n_tokens)
        y = _expert_ffn(tables, f_ext.reshape(n_tokens, D + LANES), w_gu, moe_w_down[i].astype(BF16))
        xa = _moe_combine(xa, mod3, y, n_lat, ctx_row, n_lat if last else NT // TB)
    return xa
```

```python
import functools
import math

import jax
import jax.numpy as jnp
from jax import lax
from jax.experimental import pallas as pl
from jax.experimental.pallas import tpu as pltpu

F32 = jnp.float32
BF16 = jnp.bfloat16
HIGHEST = lax.Precision.HIGHEST

D_MODEL = 1024
DEPTH = 2
GRID_W = 64
DA_HEADS = 8
DA_QK_DIM = 64
DA_V_DIM = 128
ROPE_BASE = 10000.0
HG_HEADS = 8
HG_DIM = 128
N_EXPERTS = 16
N_GROUPS = 4
EXPERTS_PER_GROUP = 4
D_FF = 512
EPS = 1e-6

LANES = 128
TOKEN_BLOCK = 256
ATTN_Q_BLOCK = 256
ATTN_GROUP = 8
MOE_TILE = 256
MOE_DMA_UNROLL = 32
PAIRS = [(i, j) for i in range(EXPERTS_PER_GROUP) for j in range(i + 1, EXPERTS_PER_GROUP)]
N_BUCKETS = N_GROUPS * len(PAIRS)
SCAN_CHUNK = 64
SCAN_SUB = SCAN_CHUNK // 2
SCAN_GROUP = 4
SCAN_TRIP = 3
EXP_CLAMP = 80.0
VMEM_LIMIT = 56 * 1024 * 1024


def _cparams(sem):
    return pltpu.CompilerParams(dimension_semantics=sem, vmem_limit_bytes=VMEM_LIMIT)


def _silu(x):
    return x * jax.nn.sigmoid(x)


def _norm_mod(x, g, sc, sh):
    y = x * lax.rsqrt(jnp.mean(x * x, axis=-1, keepdims=True) + EPS)
    return (y * g) * (1.0 + sc) + sh


def _mod_kernel(c_ref, w_ref, b_ref, o_ref):
    o_ref[0] = jnp.dot(_silu(c_ref[...]), w_ref[0], preferred_element_type=F32,
                       precision=HIGHEST) + b_ref[0]


def _modulation(cvec, ada_w, ada_b):
    R, D = cvec.shape
    depth, _, n6 = ada_w.shape
    tn = 1024
    return pl.pallas_call(
        _mod_kernel,
        grid=(depth, n6 // tn),
        in_specs=[
            pl.BlockSpec((R, D), lambda i, j: (0, 0)),
            pl.BlockSpec((1, D, tn), lambda i, j: (i, 0, j)),
            pl.BlockSpec((1, 1, tn), lambda i, j: (i, 0, j)),
        ],
        out_specs=pl.BlockSpec((1, R, tn), lambda i, j: (i, 0, j)),
        out_shape=jax.ShapeDtypeStruct((depth, R, n6), F32),
        compiler_params=_cparams(("arbitrary", "arbitrary")),
        name="adaln_mod",
    )(cvec, ada_w, ada_b.reshape(depth, 1, n6))


def _mod_spec(col, n_lat, ctx_row):
    return pl.BlockSpec((1, 1, D_MODEL), lambda b, t: (jnp.where(t < n_lat, b, ctx_row), 0, col))


def _stream_specs(n_lat, ctx_block0):
    blk = (1, TOKEN_BLOCK, D_MODEL)
    return (pl.BlockSpec(blk, lambda b, t: (b, jnp.minimum(t, n_lat - 1), 0)),
            pl.BlockSpec(blk, lambda b, t: (b, jnp.maximum(t - n_lat, 0) + ctx_block0, 0)))


def _stream_block(lat_ref, ctx_ref, n_lat):
    return jnp.where(pl.program_id(1) < n_lat, lat_ref[0], ctx_ref[0])


def _attn_proj_kernel(x_ref, c_ref, g_ref, sc_ref, sh_ref, w_ref, qg_ref, kg_ref, cos_ref, sin_ref,
                      bd_ref, q_ref, k_ref, v_ref, *, n_lat):
    D = D_MODEL
    h = _norm_mod(_stream_block(x_ref, c_ref, n_lat), g_ref[...], sc_ref[0], sh_ref[0])
    qkv = jnp.dot(h.astype(BF16), w_ref[...], preferred_element_type=F32)
    cos, sin, bd = cos_ref[...], sin_ref[...], bd_ref[...]
    lane = lax.broadcasted_iota(jnp.int32, cos.shape, 1)
    upper = (lane & (DA_QK_DIM // 2)) != 0

    def norm_rope(t, gain, scale):
        sq = t * t
        sq_hi = sq.astype(BF16)
        sq_lo = (sq - sq_hi.astype(F32)).astype(BF16)
        ss = (jnp.dot(sq_hi, bd, preferred_element_type=F32)
              + jnp.dot(sq_lo, bd, preferred_element_type=F32))
        tn = t * lax.rsqrt(ss * (1.0 / DA_QK_DIM) + EPS) * gain
        partner = jnp.where(upper, pltpu.roll(tn, DA_QK_DIM // 2, 1),
                            pltpu.roll(tn, LANES - DA_QK_DIM // 2, 1))
        return (tn * cos + partner * sin) * scale

    for j in range(D // LANES):
        sl = slice(j * LANES, (j + 1) * LANES)
        q_ref[0, :, sl] = norm_rope(qkv[:, j * LANES:(j + 1) * LANES], qg_ref[...],
                                    math.log2(math.e) / math.sqrt(DA_QK_DIM)).astype(BF16)
        k_ref[0, j] = norm_rope(qkv[:, D + j * LANES:D + (j + 1) * LANES], kg_ref[...],
                                1.0).T.astype(BF16)
    v_ref[0] = qkv[:, 2 * D:].astype(BF16)


def _attn_project(x, ctx, g, mod3, w_qkv, qg, kg, cos_t, sin_t, bd, n_lat, ctx_row):
    B, T, D = x.shape
    NT = T + ctx.shape[1]
    TB = TOKEN_BLOCK
    tok = pl.BlockSpec((1, TB, D), lambda b, t: (b, t, 0))
    const2 = lambda shape: pl.BlockSpec(shape, lambda b, t: (0, 0))
    out = jax.ShapeDtypeStruct((B, NT, D), BF16)
    return pl.pallas_call(
        functools.partial(_attn_proj_kernel, n_lat=n_lat),
        grid=(B, NT // TB),
        in_specs=[
            *_stream_specs(n_lat, 0), const2((1, D)),
            _mod_spec(1, n_lat, ctx_row), _mod_spec(0, n_lat, ctx_row),
            const2((D, 3 * D)), const2((1, LANES)), const2((1, LANES)),
            pl.BlockSpec((TB, LANES), lambda b, t: (t, 0)),
            pl.BlockSpec((TB, LANES), lambda b, t: (t, 0)),
            const2((LANES, LANES)),
        ],
        out_specs=[tok, pl.BlockSpec((1, DA_HEADS, LANES, TB), lambda b, t: (b, 0, 0, t)), tok],
        out_shape=[out, jax.ShapeDtypeStruct((B, DA_HEADS, LANES, NT), BF16), out],
        compiler_params=_cparams(("parallel", "arbitrary")),
        name="attn_qkv_proj",
    )(x, ctx, g, mod3, mod3, w_qkv, qg, kg, cos_t, sin_t, bd)


def _attn_kernel(lam_ref, q_ref, kt_ref, v_ref, sn_ref, o_ref, *, t_lat, lam_init):
    QB = ATTN_Q_BLOCK
    lp = lam_ref[...]
    lam = (jnp.exp(jnp.sum(lp[0:1] * lp[1:2], keepdims=True))
           - jnp.exp(jnp.sum(lp[2:3] * lp[3:4], keepdims=True)) + lam_init)

    def scores(q, kt):
        lane = lax.broadcasted_iota(jnp.int32, q.shape, 1)
        zero = jnp.zeros_like(q)
        return (jnp.dot(jnp.where(lane < DA_QK_DIM, q, zero), kt, preferred_element_type=F32),
                jnp.dot(jnp.where(lane >= DA_QK_DIM, q, zero), kt, preferred_element_type=F32))

    def finish(s, v):
        def softmax_parts(sm):
            e = jnp.exp2(sm - jnp.max(sm, axis=-1, keepdims=True))
            return e, jnp.sum(e, axis=-1, keepdims=True)

        e0, l0 = softmax_parts(s[0])
        e1, l1 = softmax_parts(s[1])
        a = e0 - (lam * l0 / l1) * e1
        o = jnp.dot(a.astype(BF16), v, preferred_element_type=F32) * (1.0 / l0)
        o = o * lax.rsqrt(jnp.mean(o * o, axis=-1, keepdims=True) + EPS) * sn_ref[...]
        return (o * (1.0 - lam_init)).astype(BF16)

    G = math.gcd(ATTN_GROUP, t_lat // QB)

    def latent_group(i, carry):
        rows = [pl.ds(pl.multiple_of((G * i + j) * QB, QB), QB) for j in range(G)]
        s_next = scores(q_ref[0, rows[0], :], kt_ref[0, 0])
        for j in range(G):
            s_cur = s_next
            if j + 1 < G:
                s_next = scores(q_ref[0, rows[j + 1], :], kt_ref[0, 0])
            o_ref[0, rows[j], :] = finish(s_cur, v_ref[0])
        return carry

    lax.fori_loop(0, t_lat // (G * QB), latent_group, 0)
    o_ref[0, t_lat:, :] = finish(scores(q_ref[0, t_lat:, :], kt_ref[0, 0, :, t_lat:]),
                                 v_ref[0, t_lat:, :])


def _attention(lam_p, q, kt, v, sub_norm, t_lat, lam_init):
    B, NT, D = q.shape
    blk = pl.BlockSpec((1, NT, LANES), lambda b, h: (b, 0, h))
    return pl.pallas_call(
        functools.partial(_attn_kernel, t_lat=t_lat, lam_init=lam_init),
        grid=(B, DA_HEADS),
        in_specs=[
            pl.BlockSpec(lam_p.shape, lambda b, h: (0, 0)),
            blk, pl.BlockSpec((1, 1, LANES, NT), lambda b, h: (b, h, 0, 0)), blk,
            pl.BlockSpec((1, LANES), lambda b, h: (0, 0)),
        ],
        out_specs=blk,
        out_shape=jax.ShapeDtypeStruct((B, NT, D), BF16),
        compiler_params=_cparams(("parallel", "arbitrary")),
        name="diff_attention",
    )(lam_p, q, kt, v, sub_norm)


def _out_router_kernel(o_ref, x_ref, c_ref, w_ref, gm_ref, g_ref, sc_ref, sh_ref, rw_ref, rb_ref,
                       xo_ref, f_ref, bk_ref, *, n_lat):
    D = D_MODEL
    out = jnp.dot(o_ref[0], w_ref[...], preferred_element_type=F32)
    x = _stream_block(x_ref, c_ref, n_lat) + gm_ref[0] * out
    xo_ref[0] = x
    f = _norm_mod(x, g_ref[...], sc_ref[0], sh_ref[0])
    f_ref[0, :, :D] = f
    def split3(a):
        a1 = a.astype(BF16)
        r = a - a1.astype(F32)
        a2 = r.astype(BF16)
        return a1, a2, (r - a2.astype(F32)).astype(BF16)

    f1, f2, f3 = split3(f)
    w1, w2, w3 = split3(rw_ref[...])
    mm = functools.partial(jnp.dot, preferred_element_type=F32)
    logits_t = ((mm(f3, w1) + mm(f2, w2) + mm(f1, w3)) + (mm(f2, w1) + mm(f1, w2))) + mm(f1, w1)
    logits = logits_t.T[:N_EXPERTS]
    aff = jax.nn.sigmoid(logits)
    biased = aff + rb_ref[...]
    G, E = EXPERTS_PER_GROUP, N_EXPERTS
    row = lax.broadcasted_iota(jnp.int32, biased.shape, 0)
    member = lax.rem(row, G)
    group = row // G

    def shifted(x, k):
        return jnp.where(member + k < G, pltpu.roll(x, E - k, 0), pltpu.roll(x, G - k, 0))

    rank = jnp.zeros_like(row)
    for k in range(1, G):
        other = shifted(biased, k)
        ahead = (other > biased) | ((other == biased) & (member + k >= G))
        rank = rank + ahead.astype(jnp.int32)
    top2 = rank < 2
    t = jnp.where(top2, biased, 0.0)
    gscore = t
    for k in range(1, G):
        gscore = gscore + shifted(t, k)
    best = jnp.ones_like(top2)
    for m in range(1, N_GROUPS):
        other = pltpu.roll(gscore, G * m, 0)
        best = best & ((gscore > other) | ((gscore == other) & (group < m)))
    chosen = top2 & best
    lowest = jnp.min(jnp.where(chosen, row, E), axis=0, keepdims=True)
    is_lo = chosen & (row == lowest)
    is_hi = chosen & (row != lowest)
    a_lo = jnp.sum(jnp.where(is_lo, aff, 0.0), axis=0, keepdims=True)
    a_hi = jnp.sum(jnp.where(is_hi, aff, 0.0), axis=0, keepdims=True)
    m_lo = lax.rem(lowest, G)
    m_hi = jnp.sum(jnp.where(is_hi, member, 0), axis=0, keepdims=True)
    pair = m_lo * (2 * G - 1 - m_lo) // 2 + (m_hi - m_lo - 1)
    tot = a_lo + a_hi
    bk_ref[0] = (lowest // G) * len(PAIRS) + pair

    tb = x.shape[0]
    eye = (lax.broadcasted_iota(jnp.int32, (tb, tb), 0)
           == lax.broadcasted_iota(jnp.int32, (tb, tb), 1))

    def column(row):
        return jnp.sum(jnp.where(eye, row, 0.0), axis=1, keepdims=True)

    lane = lax.broadcasted_iota(jnp.int32, (tb, LANES), 1)
    f_ref[0, :, D:] = jnp.where(lane < LANES // 2, column(a_lo / tot), column(a_hi / tot))


def _out_router(o, x, ctx, ctx_block0, w_o, mod3, g_ffn, rwt, rb, n_lat, ctx_row):
    B, NT, D = o.shape
    TB = TOKEN_BLOCK
    nb = NT // TB
    tok = pl.BlockSpec((1, TB, D), lambda b, t: (b, t, 0))
    row = pl.BlockSpec((1, 1, TB), lambda b, t: (b * nb + t, 0, 0))
    const2 = lambda shape: pl.BlockSpec(shape, lambda b, t: (0, 0))
    rows = lambda dt: jax.ShapeDtypeStruct((B * nb, 1, TB), dt)
    return pl.pallas_call(
        functools.partial(_out_router_kernel, n_lat=n_lat),
        grid=(B, nb),
        in_specs=[
            tok, *_stream_specs(n_lat, ctx_block0), const2((D, D)),
            _mod_spec(2, n_lat, ctx_row), const2((1, D)),
            _mod_spec(4, n_lat, ctx_row), _mod_spec(3, n_lat, ctx_row),
            const2((D, LANES)), const2((N_EXPERTS, 1)),
        ],
        out_specs=[tok, pl.BlockSpec((1, TB, D + LANES), lambda b, t: (b, t, 0)), row],
        out_shape=[jax.ShapeDtypeStruct((B, NT, D), F32),
                   jax.ShapeDtypeStruct((B, NT, D + LANES), F32), rows(jnp.int32)],
        compiler_params=_cparams(("parallel", "arbitrary")),
        name="out_proj_router",
    )(o, x, ctx, w_o, mod3, g_ffn, mod3, mod3, rwt, rb)


def _routing_tables(bucket, n_tokens):
    tm = MOE_TILE
    max_tiles = n_tokens // tm + N_BUCKETS
    n_rows = max_tiles * tm
    onehot = (bucket[:, None] == jnp.arange(N_BUCKETS, dtype=jnp.int32)[None, :]).astype(jnp.int32)
    count = jnp.sum(onehot, axis=0)
    rank = jnp.sum(jnp.cumsum(onehot, axis=0) * onehot, axis=1) - 1
    btiles = (count + tm - 1) // tm
    tile_end = jnp.cumsum(btiles)
    tile_start = tile_end - btiles
    dest = tile_start[bucket] * tm + rank
    token_at = jnp.zeros((n_rows,), jnp.int32).at[dest].set(jnp.arange(n_tokens, dtype=jnp.int32))
    tile = jnp.arange(max_tiles, dtype=jnp.int32)
    tile_bucket = jnp.minimum(jnp.sum((tile[:, None] >= tile_end[None, :]).astype(jnp.int32), axis=1),
                              N_BUCKETS - 1)
    in_bucket = tile - tile_start[tile_bucket]
    n_valid = jnp.clip(count[tile_bucket] - in_bucket * tm, 0, tm)
    n_valid = jnp.where(tile < tile_end[-1], n_valid, 0).astype(jnp.int32)
    grp = tile_bucket // len(PAIRS)
    pair = tile_bucket % len(PAIRS)
    lo_tab = jnp.array([p[0] for p in PAIRS], jnp.int32)
    hi_tab = jnp.array([p[1] for p in PAIRS], jnp.int32)
    tile_lo = grp * EXPERTS_PER_GROUP + lo_tab[pair]
    tile_hi = grp * EXPERTS_PER_GROUP + hi_tab[pair]
    return token_at, n_valid, tile_lo, tile_hi


def _expert_kernel(tok_ref, nv_ref, tlo_ref, thi_ref, f_hbm, wg_lo_ref, wu_lo_ref, wd_lo_ref,
                   wg_hi_ref, wu_hi_ref, wd_hi_ref, y_hbm, xbuf, ybuf, gsem, ssem):
    TM, D = MOE_TILE, D_MODEL
    i = pl.program_id(0)
    n = pl.num_programs(0)
    slot = lax.rem(i, 2)
    other = 1 - slot

    def gather_copy(tile, r, s):
        return pltpu.make_async_copy(f_hbm.at[pl.ds(tok_ref[tile * TM + r], 1), :],
                                     xbuf.at[s, pl.ds(r, 1), :], gsem.at[s])

    def scatter_copy(tile, r, s):
        return pltpu.make_async_copy(ybuf.at[s, pl.ds(r, 1), :],
                                     y_hbm.at[pl.ds(tok_ref[tile * TM + r], 1), :], ssem.at[s])

    def for_rows(n_rows, fn):
        U = MOE_DMA_UNROLL
        n_groups = n_rows // U

        def group(g, carry):
            base = pl.multiple_of(g * U, U)
            for u in range(U):
                fn(base + u)
            return carry

        def single(r, carry):
            fn(r)
            return carry

        lax.fori_loop(0, n_groups, group, 0)
        lax.fori_loop(n_groups * U, n_rows, single, 0)

    def wait_rows(n_rows, whole_tile_copy, row_copy):
        @pl.when(n_rows == TM)
        def _():
            whole_tile_copy.wait()

        @pl.when(n_rows < TM)
        def _():
            for_rows(n_rows, lambda r: row_copy(r).wait())

    def start_gather(tile, s):
        for_rows(nv_ref[tile], lambda r: gather_copy(tile, r, s).start())

    def wait_gather(tile, s):
        wait_rows(nv_ref[tile],
                  pltpu.make_async_copy(f_hbm.at[pl.ds(0, TM), :], xbuf.at[s], gsem.at[s]),
                  lambda r: gather_copy(tile, r, s))

    def start_scatter(tile, s):
        for_rows(nv_ref[tile], lambda r: scatter_copy(tile, r, s).start())

    def wait_scatter(tile, s):
        wait_rows(nv_ref[tile],
                  pltpu.make_async_copy(ybuf.at[s], y_hbm.at[pl.ds(0, TM), :], ssem.at[s]),
                  lambda r: scatter_copy(tile, r, s))

    @pl.when(i == 0)
    def _():
        xbuf[...] = jnp.zeros_like(xbuf)
        start_gather(0, 0)

    wait_gather(i, slot)

    @pl.when(i >= 2)
    def _():
        wait_scatter(i - 2, slot)

    @pl.when(i + 1 < n)
    def _():
        start_gather(i + 1, other)

    @pl.when(nv_ref[i] > 0)
    def _():
        xe = xbuf[slot]
        x = xe[:, :D].astype(BF16)

        def ffn(wg_ref, wu_ref, wd_ref):
            hid = (_silu(jnp.dot(x, wg_ref[0, 0], preferred_element_type=F32))
                   * jnp.dot(x, wu_ref[0, 0], preferred_element_type=F32))
            return jnp.dot(hid.astype(BF16), wd_ref[0, 0], preferred_element_type=F32)

        ybuf[slot] = (xe[:, D:D + 1] * ffn(wg_lo_ref, wu_lo_ref, wd_lo_ref)
                      + xe[:, D + LANES // 2:D + LANES // 2 + 1] * ffn(wg_hi_ref, wu_hi_ref, wd_hi_ref))
        start_scatter(i, slot)

    @pl.when(i == n - 1)
    def _():
        wait_scatter(i - 1, other)
        wait_scatter(i, slot)


def _expert_ffn(tables, f_ext, layer, w_gate, w_up, w_down):
    token_at, n_valid, tile_lo, tile_hi = tables
    N = f_ext.shape[0]
    D = D_MODEL
    tm = MOE_TILE
    lo = lambda i, tok, nv, tlo, thi: (layer, tlo[i], 0, 0)
    hi = lambda i, tok, nv, tlo, thi: (layer, thi[i], 0, 0)
    w_in, w_out = (1, 1, D, D_FF), (1, 1, D_FF, D)
    return pl.pallas_call(
        _expert_kernel,
        grid_spec=pltpu.PrefetchScalarGridSpec(
            num_scalar_prefetch=4,
            grid=(n_valid.shape[0],),
            in_specs=[
                pl.BlockSpec(memory_space=pl.ANY),
                pl.BlockSpec(w_in, lo), pl.BlockSpec(w_in, lo), pl.BlockSpec(w_out, lo),
                pl.BlockSpec(w_in, hi), pl.BlockSpec(w_in, hi), pl.BlockSpec(w_out, hi),
            ],
            out_specs=pl.BlockSpec(memory_space=pl.ANY),
            scratch_shapes=[pltpu.VMEM((2, tm, D + LANES), F32), pltpu.VMEM((2, tm, D), F32),
                            pltpu.SemaphoreType.DMA((2,)), pltpu.SemaphoreType.DMA((2,))],
        ),
        out_shape=jax.ShapeDtypeStruct((N, D), F32),
        compiler_params=_cparams(("arbitrary",)),
        name="moe_expert_ffn",
    )(token_at, n_valid, tile_lo, tile_hi, f_ext, w_gate, w_up, w_down, w_gate, w_up, w_down)


def _combine_kernel(x_ref, gf_ref, y_ref, o_ref):
    o_ref[0] = x_ref[0] + gf_ref[0] * y_ref[...]


def _moe_combine(xa, mod3, y, n_lat, ctx_row, n_blocks_out):
    B, NT, D = xa.shape
    TB = TOKEN_BLOCK
    nb = NT // TB
    tok = pl.BlockSpec((1, TB, D), lambda b, t: (b, t, 0))
    return pl.pallas_call(
        _combine_kernel,
        grid=(B, n_blocks_out),
        in_specs=[tok, _mod_spec(5, n_lat, ctx_row),
                  pl.BlockSpec((TB, D), lambda b, t: (b * nb + t, 0))],
        out_specs=tok,
        out_shape=jax.ShapeDtypeStruct((B, n_blocks_out * TB, D), F32),
        compiler_params=_cparams(("parallel", "arbitrary")),
        name="moe_combine",
    )(xa, mod3, y)


def _hgrn_proj_kernel(x_ref, g_ref, sc_ref, sh_ref, w_ref, lb_ref,
                      q_ref, v_ref, kf_ref, lff_ref, kb_ref, lfb_ref, gate_ref):
    HK = HG_HEADS * HG_DIM
    h = _norm_mod(x_ref[0], g_ref[...], sc_ref[0], sh_ref[0])
    y = jnp.dot(h.astype(BF16), w_ref[...], preferred_element_type=F32)
    q_ref[0] = _silu(y[:, :HK])
    v_ref[0] = y[:, HK:2 * HK].astype(BF16)
    gate_ref[0] = y[:, 4 * HK:]

    def forget(z, lbd, k_ref, lf_ref):
        e = jnp.exp(-jnp.abs(z))
        t = 1.0 + e
        k_ref[0] = (1.0 - lbd) * (jnp.where(z >= 0.0, e, 1.0) / t)
        a = jnp.log(lbd)
        b = jnp.log1p(-lbd) + (jnp.minimum(z, 0.0) - jnp.log(t))
        lf_ref[0] = jnp.maximum(a, b) + jnp.log(1.0 + jnp.exp(-jnp.abs(a - b)))

    forget(y[:, 2 * HK:3 * HK], lb_ref[0:1], kf_ref, lff_ref)
    forget(y[:, 3 * HK:4 * HK], lb_ref[1:2], kb_ref, lfb_ref)


def _hgrn_project(xa, g, mod3, w_in, lb, n_lat, ctx_row):
    B, NT, D = xa.shape
    TB = TOKEN_BLOCK
    HK = HG_HEADS * HG_DIM
    tok = pl.BlockSpec((1, TB, D), lambda b, t: (b, t, 0))
    tok_o = pl.BlockSpec((1, TB, HK), lambda b, t: (b, t, 0))
    const2 = lambda shape: pl.BlockSpec(shape, lambda b, t: (0, 0))
    out = jax.ShapeDtypeStruct((B, NT, HK), F32)
    return pl.pallas_call(
        _hgrn_proj_kernel,
        grid=(B, NT // TB),
        in_specs=[
            tok, const2((1, D)), _mod_spec(1, n_lat, ctx_row), _mod_spec(0, n_lat, ctx_row),
            pl.BlockSpec((D, 5 * HK), lambda b, t: (0, 0), pipeline_mode=pl.Buffered(1)),
            const2((2, HK)),
        ],
        out_specs=[tok_o] * 7,
        out_shape=[out, jax.ShapeDtypeStruct((B, NT, HK), BF16)] + [out] * 5,
        compiler_params=_cparams(("parallel", "arbitrary")),
        name="hgrn_proj",
    )(xa, g, mod3, mod3, w_in, lb)


def _scan_chunk_scores(q, k, vb, a, forward):
    C, SB = SCAN_CHUNK, SCAN_SUB
    row = lax.broadcasted_iota(jnp.int32, a.shape, 0)
    first = row < SB
    mid_row = SB // 2
    m = jnp.where(first, a[mid_row:mid_row + 1], a[SB + mid_row:SB + mid_row + 1])
    qd = q * jnp.exp(jnp.minimum(a - m, EXP_CLAMP))
    kd = k * jnp.exp(jnp.minimum(m - a, EXP_CLAMP))
    edge = a[SB - 1:SB] if forward else a[SB:SB + 1]
    e_x = jnp.exp(-jnp.abs(a - edge))
    nt = (((1,), (1,)), ((), ()))
    s_d = lax.dot_general(qd.astype(BF16), kd.astype(BF16), nt, preferred_element_type=F32)
    s_x = lax.dot_general((q * e_x).astype(BF16), (k * e_x).astype(BF16), nt,
                          preferred_element_type=F32)
    a_out = a[C - 1:C] if forward else a[0:1]
    k_out = k * jnp.exp(a_out - a)
    upd = lax.dot_general(vb, k_out.astype(BF16), (((0,), (0,)), ((), ())),
                          preferred_element_type=F32)
    return s_d, s_x, q * jnp.exp(a), upd, jnp.exp(a_out)


def _scan_chunk_intra(s_d, s_x, vb, forward):
    SB = SCAN_SUB
    t_i = lax.broadcasted_iota(jnp.int32, s_d.shape, 0)
    s_i = lax.broadcasted_iota(jnp.int32, s_d.shape, 1)
    same = (t_i < SB) == (s_i < SB)
    if forward:
        causal, cross = s_i <= t_i, (t_i >= SB) & (s_i < SB)
    else:
        causal, cross = s_i >= t_i, (t_i < SB) & (s_i >= SB)
    scores = jnp.where(same & causal, s_d, jnp.where(cross, s_x, 0.0))
    return jnp.dot(scores.astype(BF16), vb, preferred_element_type=F32)


def _hgrn_scan_kernel(q_ref, v_ref, kf_ref, lff_ref, kb_ref, lfb_ref, gate_ref, gain_ref,
                      o_ref, acc_ref, qe_ref, upd_ref, dec_ref, inter_ref, *, n_chunks, n_lat_chunks):
    C, G = SCAN_CHUNK, SCAN_GROUP
    R = C * G
    pos = lax.rem(lax.broadcasted_iota(jnp.int32, (R, HG_DIM), 0), C)

    def cumsum(lf, forward):
        x = lf
        step = 1
        while step < C:
            if forward:
                x = x + jnp.where(pos >= step, pltpu.roll(x, step, 0), 0.0)
            else:
                x = x + jnp.where(pos < C - step, pltpu.roll(x, R - step, 0), 0.0)
            step *= 2
        return x

    def group_scores(g):
        rows = pl.ds(pl.multiple_of(g * R, R), R)
        q = q_ref[0, rows, :]
        vb = v_ref[0, rows, :].astype(BF16)
        kf, kb = kf_ref[0, rows, :], kb_ref[0, rows, :]
        a_f = cumsum(lff_ref[0, rows, :], True)
        a_b = cumsum(lfb_ref[0, rows, :], False)
        res = []
        for ci in range(G):
            sl = slice(ci * C, (ci + 1) * C)
            res.append((vb[sl], _scan_chunk_scores(q[sl], kf[sl], vb[sl], a_f[sl], True),
                        _scan_chunk_scores(q[sl], kb[sl], vb[sl], a_b[sl], False)))
        return res

    def group_finish(g, res):
        o_sum = [_scan_chunk_intra(fw[0], fw[1], vb, True) + _scan_chunk_intra(bw[0], bw[1], vb, False)
                 for vb, fw, bw in res]
        for d in range(2):
            p = l = None
            for ci in (range(G) if d == 0 else reversed(range(G))):
                _, _, qe, upd, dec = res[ci][1 + d]
                crow = pl.ds(pl.multiple_of((g * G + ci) * C, C), C)
                if p is None:
                    qe_ref[d, crow, :] = qe.astype(BF16)
                    p, l = dec, upd
                else:
                    qe_ref[d, crow, :] = (qe * p).astype(BF16)
                    o_sum[ci] = o_sum[ci] + lax.dot_general(qe.astype(BF16), l.astype(BF16), nt,
                                                            preferred_element_type=F32)
                    p, l = p * dec, l * dec + upd
            upd_ref[d, g] = l
            dec_ref[d, pl.ds(g, 1), :] = p
        for ci in range(G):
            acc_ref[pl.ds(pl.multiple_of((g * G + ci) * C, C), C), :] = o_sum[ci]

    nt = (((1,), (1,)), ((), ()))
    n_groups = n_chunks // G
    n_trip = math.gcd(SCAN_TRIP, n_groups)

    def local(i, carry):
        nxt = group_scores(i * n_trip)
        for j in range(n_trip):
            cur = nxt
            if j + 1 < n_trip:
                nxt = group_scores(i * n_trip + j + 1)
            group_finish(i * n_trip + j, cur)
        return carry

    lax.fori_loop(0, n_groups // n_trip, local, 0)

    def carry_state(j, carry):
        st_f, st_b = carry
        gf = lax.rem(j + n_lat_chunks // G, n_groups)
        gb = n_groups - 1 - j
        rf = pl.ds(pl.multiple_of(gf * R, R), R)
        rb = pl.ds(pl.multiple_of(gb * R, R), R)
        inter_ref[0, rf, :] = lax.dot_general(qe_ref[0, rf, :], st_f.astype(BF16), nt,
                                              preferred_element_type=F32)
        inter_ref[1, rb, :] = lax.dot_general(qe_ref[1, rb, :], st_b.astype(BF16), nt,
                                              preferred_element_type=F32)
        st_f = st_f * dec_ref[0, pl.ds(gf, 1), :] + upd_ref[0, gf]
        st_b = st_b * dec_ref[1, pl.ds(gb, 1), :] + upd_ref[1, gb]
        return st_f, st_b

    zero = jnp.zeros((HG_DIM, HG_DIM), F32)
    lax.fori_loop(0, n_groups, carry_state, (zero, zero))
    o = acc_ref[...] + inter_ref[0] + inter_ref[1]
    o = o * lax.rsqrt(jnp.mean(o * o, axis=-1, keepdims=True) + EPS) * gain_ref[...]
    o_ref[0] = (o * _silu(gate_ref[0])).astype(BF16)


def _hgrn_scan(q, v, kf, lff, kb, lfb, gate, gain, t_lat):
    B, NT, HK = q.shape
    blk = pl.BlockSpec((1, NT, HG_DIM), lambda b, h: (b, 0, h))
    return pl.pallas_call(
        functools.partial(_hgrn_scan_kernel, n_chunks=NT // SCAN_CHUNK,
                          n_lat_chunks=t_lat // SCAN_CHUNK),
        grid=(B, HG_HEADS),
        in_specs=[blk] * 7 + [pl.BlockSpec((1, HG_DIM), lambda b, h: (0, 0))],
        out_specs=blk,
        out_shape=jax.ShapeDtypeStruct((B, NT, HK), BF16),
        scratch_shapes=[pltpu.VMEM((NT, HG_DIM), F32),
                        pltpu.VMEM((2, NT, HG_DIM), BF16),
                        pltpu.VMEM((2, NT // (SCAN_CHUNK * SCAN_GROUP), HG_DIM, HG_DIM), F32),
                        pltpu.VMEM((2, NT // (SCAN_CHUNK * SCAN_GROUP), HG_DIM), F32),
                        pltpu.VMEM((2, NT, HG_DIM), F32)],
        compiler_params=_cparams(("parallel", "arbitrary")),
        name="hgrn_scan",
    )(q, v, kf, lff, kb, lfb, gate, gain)


def _rope_tables(t_lat, t_ctx):
    rows = t_lat // GRID_W
    r = jnp.repeat(jnp.arange(rows, dtype=F32), GRID_W)
    col = jnp.tile(jnp.arange(GRID_W, dtype=F32), rows)
    n_pairs = DA_QK_DIM // 4
    inv = ROPE_BASE ** (-jnp.arange(n_pairs, dtype=F32) / n_pairs)
    ang = jnp.concatenate([r[:, None] * inv, col[:, None] * inv], axis=-1)
    cos, sin = jnp.cos(ang), jnp.sin(ang)
    cos_l = jnp.tile(cos, (1, LANES // cos.shape[1]))
    sin_l = jnp.tile(jnp.concatenate([-sin, sin], axis=-1), (1, LANES // (2 * sin.shape[1])))
    cos_l = jnp.concatenate([cos_l, jnp.ones((t_ctx, LANES), F32)], axis=0)
    sin_l = jnp.concatenate([sin_l, jnp.zeros((t_ctx, LANES), F32)], axis=0)
    return cos_l, sin_l


def kernel(x, c, ctx, c_ctx, ada_w, ada_b, norm_mix, norm_ffn, attn_w_qkv, attn_w_o, attn_q_norm,
           attn_k_norm, attn_sub_norm, attn_lambda, hgrn_w_in, hgrn_w_o, hgrn_out_norm,
           hgrn_lb_gamma, router_w, router_bias, moe_w_gate, moe_w_up, moe_w_down):
    B, T, D = x.shape
    Tc = ctx.shape[1]
    TB = TOKEN_BLOCK
    assert D == D_MODEL and T % TB == 0 and Tc % TB == 0 and T % GRID_W == 0
    assert ada_w.shape[0] == DEPTH == 2
    assert T % (SCAN_CHUNK * SCAN_GROUP) == 0 and Tc % (SCAN_CHUNK * SCAN_GROUP) == 0
    assert T % ATTN_Q_BLOCK == 0
    NT = T + Tc
    n_lat = T // TB
    n_tokens = B * NT
    ctx_row = B

    n_rows = -(-(B + 1) // 8) * 8
    cvec = jnp.concatenate([c, c_ctx[None, :], jnp.zeros((n_rows - B - 1, D), F32)], axis=0)
    mod = _modulation(cvec, ada_w, ada_b)

    wg_all, wu_all, wd_all = (w.astype(BF16) for w in (moe_w_gate, moe_w_up, moe_w_down))
    rwt = jnp.pad(router_w.astype(F32), ((0, 0), (0, LANES - N_EXPERTS)))
    rb = router_bias.reshape(N_EXPERTS, 1)

    p = jax.nn.softmax(hgrn_lb_gamma.astype(F32), axis=1)
    cum = jnp.cumsum(p, axis=1)
    lb_all = cum - cum[:, :1]

    cos_t, sin_t = _rope_tables(T, Tc)
    lane = jnp.arange(LANES)
    bd = (lane[:, None] // DA_QK_DIM == lane[None, :] // DA_QK_DIM).astype(BF16)

    xa = None
    for i in range(DEPTH):
        mod3 = mod[i].reshape(n_rows, 1, 6 * D)
        last = i == DEPTH - 1
        j = i // 2
        g_mix = norm_mix[i].reshape(1, D)
        if i % 2 == 0:
            assert i == 0
            lam_init = 0.8 - 0.6 * math.exp(-0.3 * i)
            q, k, v = _attn_project(
                x, ctx, g_mix, mod3, attn_w_qkv[j].astype(BF16),
                jnp.tile(attn_q_norm[j], LANES // DA_QK_DIM).reshape(1, LANES),
                jnp.tile(attn_k_norm[j], LANES // DA_QK_DIM).reshape(1, LANES),
                cos_t, sin_t, bd, n_lat, ctx_row)
            o = _attention(attn_lambda[j], q, k, v, attn_sub_norm[j].reshape(1, DA_V_DIM),
                           T, lam_init)
            w_o = attn_w_o[j]
        else:
            parts = _hgrn_project(xa, g_mix, mod3, hgrn_w_in[j].astype(BF16), lb_all[:, i],
                                  n_lat, ctx_row)
            o = _hgrn_scan(*parts, hgrn_out_norm[j].reshape(1, HG_DIM), T)
            w_o = hgrn_w_o[j]
        stream = (x, ctx, 0) if xa is None else (xa, xa, n_lat)
        xa, f_ext, bucket = _out_router(
            o, *stream, w_o.astype(BF16), mod3, norm_ffn[i].reshape(1, D), rwt, rb, n_lat, ctx_row)
        tables = _routing_tables(bucket.reshape(n_tokens), n_tokens)
        y = _expert_ffn(tables, f_ext.reshape(n_tokens, D + LANES), i, wg_all, wu_all, wd_all)
        xa = _moe_combine(xa, mod3, y, n_lat, ctx_row, n_lat if last else NT // TB)
    return xa
```

```python
import functools
import math

import jax
import jax.numpy as jnp
from jax import lax
from jax.experimental import pallas as pl
from jax.experimental.pallas import tpu as pltpu

F32 = jnp.float32
BF16 = jnp.bfloat16
HIGHEST = lax.Precision.HIGHEST

D_MODEL = 1024
DEPTH = 2
GRID_W = 64
DA_HEADS = 8
DA_QK_DIM = 64
DA_V_DIM = 128
ROPE_BASE = 10000.0
HG_HEADS = 8
HG_DIM = 128
N_EXPERTS = 16
N_GROUPS = 4
EXPERTS_PER_GROUP = 4
D_FF = 512
EPS = 1e-6

LANES = 128
TOKEN_BLOCK = 256
ATTN_Q_BLOCK = 256
ATTN_GROUP = 8
MOE_TILE = 256
MOE_DMA_UNROLL = 32
PAIRS = [(i, j) for i in range(EXPERTS_PER_GROUP) for j in range(i + 1, EXPERTS_PER_GROUP)]
N_BUCKETS = N_GROUPS * len(PAIRS)
HGRN_PROJ_PIECE = 256
SCAN_CHUNK = 64
SCAN_SUB = SCAN_CHUNK // 2
SCAN_GROUP = 4
SCAN_TRIP = 3
EXP_CLAMP = 80.0
VMEM_LIMIT = 56 * 1024 * 1024


def _cparams(sem):
    return pltpu.CompilerParams(dimension_semantics=sem, vmem_limit_bytes=VMEM_LIMIT)


def _silu(x):
    return x * jax.nn.sigmoid(x)


def _norm_mod(x, g, sc, sh):
    y = x * lax.rsqrt(jnp.mean(x * x, axis=-1, keepdims=True) + EPS)
    return (y * g) * (1.0 + sc) + sh


def _mod_kernel(c_ref, w_ref, b_ref, o_ref):
    o_ref[0] = jnp.dot(_silu(c_ref[...]), w_ref[0], preferred_element_type=F32,
                       precision=HIGHEST) + b_ref[0]


def _modulation(cvec, ada_w, ada_b):
    R, D = cvec.shape
    depth, _, n6 = ada_w.shape
    tn = 1024
    return pl.pallas_call(
        _mod_kernel,
        grid=(depth, n6 // tn),
        in_specs=[
            pl.BlockSpec((R, D), lambda i, j: (0, 0)),
            pl.BlockSpec((1, D, tn), lambda i, j: (i, 0, j)),
            pl.BlockSpec((1, 1, tn), lambda i, j: (i, 0, j)),
        ],
        out_specs=pl.BlockSpec((1, R, tn), lambda i, j: (i, 0, j)),
        out_shape=jax.ShapeDtypeStruct((depth, R, n6), F32),
        compiler_params=_cparams(("arbitrary", "arbitrary")),
        name="adaln_mod",
    )(cvec, ada_w, ada_b.reshape(depth, 1, n6))


def _mod_spec(col, n_lat, ctx_row):
    return pl.BlockSpec((1, 1, D_MODEL), lambda b, t: (jnp.where(t < n_lat, b, ctx_row), 0, col))


def _stream_specs(n_lat, ctx_block0):
    blk = (1, TOKEN_BLOCK, D_MODEL)
    return (pl.BlockSpec(blk, lambda b, t: (b, jnp.minimum(t, n_lat - 1), 0)),
            pl.BlockSpec(blk, lambda b, t: (b, jnp.maximum(t - n_lat, 0) + ctx_block0, 0)))


def _stream_block(lat_ref, ctx_ref, n_lat):
    return jnp.where(pl.program_id(1) < n_lat, lat_ref[0], ctx_ref[0])


def _attn_proj_kernel(x_ref, c_ref, g_ref, sc_ref, sh_ref, w_ref, qg_ref, kg_ref, cos_ref, sin_ref,
                      bd_ref, q_ref, k_ref, v_ref, *, n_lat):
    D = D_MODEL
    h = _norm_mod(_stream_block(x_ref, c_ref, n_lat), g_ref[...], sc_ref[0], sh_ref[0])
    qkv = jnp.dot(h.astype(BF16), w_ref[...], preferred_element_type=F32)
    cos, sin, bd = cos_ref[...], sin_ref[...], bd_ref[...]
    lane = lax.broadcasted_iota(jnp.int32, cos.shape, 1)
    upper = (lane & (DA_QK_DIM // 2)) != 0

    def norm_rope(t, gain, scale):
        sq = t * t
        sq_hi = sq.astype(BF16)
        sq_lo = (sq - sq_hi.astype(F32)).astype(BF16)
        ss = (jnp.dot(sq_hi, bd, preferred_element_type=F32)
              + jnp.dot(sq_lo, bd, preferred_element_type=F32))
        tn = t * lax.rsqrt(ss * (1.0 / DA_QK_DIM) + EPS) * gain
        partner = jnp.where(upper, pltpu.roll(tn, DA_QK_DIM // 2, 1),
                            pltpu.roll(tn, LANES - DA_QK_DIM // 2, 1))
        return (tn * cos + partner * sin) * scale

    for j in range(D // LANES):
        sl = slice(j * LANES, (j + 1) * LANES)
        q_ref[0, :, sl] = norm_rope(qkv[:, j * LANES:(j + 1) * LANES], qg_ref[...],
                                    math.log2(math.e) / math.sqrt(DA_QK_DIM)).astype(BF16)
        k_ref[0, j] = norm_rope(qkv[:, D + j * LANES:D + (j + 1) * LANES], kg_ref[...],
                                1.0).T.astype(BF16)
    v_ref[0] = qkv[:, 2 * D:].astype(BF16)


def _attn_project(x, ctx, g, mod3, w_qkv, qg, kg, cos_t, sin_t, bd, n_lat, ctx_row):
    B, T, D = x.shape
    NT = T + ctx.shape[1]
    TB = TOKEN_BLOCK
    tok = pl.BlockSpec((1, TB, D), lambda b, t: (b, t, 0))
    const2 = lambda shape: pl.BlockSpec(shape, lambda b, t: (0, 0))
    out = jax.ShapeDtypeStruct((B, NT, D), BF16)
    return pl.pallas_call(
        functools.partial(_attn_proj_kernel, n_lat=n_lat),
        grid=(B, NT // TB),
        in_specs=[
            *_stream_specs(n_lat, 0), const2((1, D)),
            _mod_spec(1, n_lat, ctx_row), _mod_spec(0, n_lat, ctx_row),
            const2((D, 3 * D)), const2((1, LANES)), const2((1, LANES)),
            pl.BlockSpec((TB, LANES), lambda b, t: (t, 0)),
            pl.BlockSpec((TB, LANES), lambda b, t: (t, 0)),
            const2((LANES, LANES)),
        ],
        out_specs=[tok, pl.BlockSpec((1, DA_HEADS, LANES, TB), lambda b, t: (b, 0, 0, t)), tok],
        out_shape=[out, jax.ShapeDtypeStruct((B, DA_HEADS, LANES, NT), BF16), out],
        compiler_params=_cparams(("parallel", "arbitrary")),
        name="attn_qkv_proj",
    )(x, ctx, g, mod3, mod3, w_qkv, qg, kg, cos_t, sin_t, bd)


def _attn_kernel(lam_ref, q_ref, kt_ref, v_ref, sn_ref, o_ref, *, t_lat, lam_init):
    QB = ATTN_Q_BLOCK
    lp = lam_ref[...]
    lam = (jnp.exp(jnp.sum(lp[0:1] * lp[1:2], keepdims=True))
           - jnp.exp(jnp.sum(lp[2:3] * lp[3:4], keepdims=True)) + lam_init)

    def scores(q, kt):
        lane = lax.broadcasted_iota(jnp.int32, q.shape, 1)
        zero = jnp.zeros_like(q)
        return (jnp.dot(jnp.where(lane < DA_QK_DIM, q, zero), kt, preferred_element_type=F32),
                jnp.dot(jnp.where(lane >= DA_QK_DIM, q, zero), kt, preferred_element_type=F32))

    def finish(s, v):
        def softmax_parts(sm):
            e = jnp.exp2(sm - jnp.max(sm, axis=-1, keepdims=True))
            return e, jnp.sum(e, axis=-1, keepdims=True)

        e0, l0 = softmax_parts(s[0])
        e1, l1 = softmax_parts(s[1])
        a = e0 - (lam * l0 / l1) * e1
        o = jnp.dot(a.astype(BF16), v, preferred_element_type=F32) * (1.0 / l0)
        o = o * lax.rsqrt(jnp.mean(o * o, axis=-1, keepdims=True) + EPS) * sn_ref[...]
        return (o * (1.0 - lam_init)).astype(BF16)

    G = math.gcd(ATTN_GROUP, t_lat // QB)

    def latent_group(i, carry):
        rows = [pl.ds(pl.multiple_of((G * i + j) * QB, QB), QB) for j in range(G)]
        s_next = scores(q_ref[0, rows[0], :], kt_ref[0, 0])
        for j in range(G):
            s_cur = s_next
            if j + 1 < G:
                s_next = scores(q_ref[0, rows[j + 1], :], kt_ref[0, 0])
            o_ref[0, rows[j], :] = finish(s_cur, v_ref[0])
        return carry

    lax.fori_loop(0, t_lat // (G * QB), latent_group, 0)
    o_ref[0, t_lat:, :] = finish(scores(q_ref[0, t_lat:, :], kt_ref[0, 0, :, t_lat:]),
                                 v_ref[0, t_lat:, :])


def _attention(lam_p, q, kt, v, sub_norm, t_lat, lam_init):
    B, NT, D = q.shape
    blk = pl.BlockSpec((1, NT, LANES), lambda b, h: (b, 0, h))
    return pl.pallas_call(
        functools.partial(_attn_kernel, t_lat=t_lat, lam_init=lam_init),
        grid=(B, DA_HEADS),
        in_specs=[
            pl.BlockSpec(lam_p.shape, lambda b, h: (0, 0)),
            blk, pl.BlockSpec((1, 1, LANES, NT), lambda b, h: (b, h, 0, 0)), blk,
            pl.BlockSpec((1, LANES), lambda b, h: (0, 0)),
        ],
        out_specs=blk,
        out_shape=jax.ShapeDtypeStruct((B, NT, D), BF16),
        compiler_params=_cparams(("parallel", "arbitrary")),
        name="diff_attention",
    )(lam_p, q, kt, v, sub_norm)


def _out_router_kernel(o_ref, x_ref, c_ref, w_ref, gm_ref, g_ref, sc_ref, sh_ref, rw_ref, rb_ref,
                       xo_ref, f_ref, bk_ref, *, n_lat):
    D = D_MODEL
    out = jnp.dot(o_ref[0], w_ref[...], preferred_element_type=F32)
    x = _stream_block(x_ref, c_ref, n_lat) + gm_ref[0] * out
    xo_ref[0] = x
    f = _norm_mod(x, g_ref[...], sc_ref[0], sh_ref[0])
    f_ref[0, :, :D] = f
    def split3(a):
        a1 = a.astype(BF16)
        r = a - a1.astype(F32)
        a2 = r.astype(BF16)
        return a1, a2, (r - a2.astype(F32)).astype(BF16)

    f1, f2, f3 = split3(f)
    w1, w2, w3 = split3(rw_ref[...])
    mm = functools.partial(jnp.dot, preferred_element_type=F32)
    logits_t = ((mm(f3, w1) + mm(f2, w2) + mm(f1, w3)) + (mm(f2, w1) + mm(f1, w2))) + mm(f1, w1)
    logits = logits_t.T[:N_EXPERTS]
    aff = jax.nn.sigmoid(logits)
    biased = aff + rb_ref[...]
    G, E = EXPERTS_PER_GROUP, N_EXPERTS
    row = lax.broadcasted_iota(jnp.int32, biased.shape, 0)
    member = lax.rem(row, G)
    group = row // G

    def shifted(x, k):
        return jnp.where(member + k < G, pltpu.roll(x, E - k, 0), pltpu.roll(x, G - k, 0))

    rank = jnp.zeros_like(row)
    for k in range(1, G):
        other = shifted(biased, k)
        ahead = (other > biased) | ((other == biased) & (member + k >= G))
        rank = rank + ahead.astype(jnp.int32)
    top2 = rank < 2
    t = jnp.where(top2, biased, 0.0)
    gscore = t
    for k in range(1, G):
        gscore = gscore + shifted(t, k)
    best = jnp.ones_like(top2)
    for m in range(1, N_GROUPS):
        other = pltpu.roll(gscore, G * m, 0)
        best = best & ((gscore > other) | ((gscore == other) & (group < m)))
    chosen = top2 & best
    lowest = jnp.min(jnp.where(chosen, row, E), axis=0, keepdims=True)
    is_lo = chosen & (row == lowest)
    is_hi = chosen & (row != lowest)
    a_lo = jnp.sum(jnp.where(is_lo, aff, 0.0), axis=0, keepdims=True)
    a_hi = jnp.sum(jnp.where(is_hi, aff, 0.0), axis=0, keepdims=True)
    m_lo = lax.rem(lowest, G)
    m_hi = jnp.sum(jnp.where(is_hi, member, 0), axis=0, keepdims=True)
    pair = m_lo * (2 * G - 1 - m_lo) // 2 + (m_hi - m_lo - 1)
    tot = a_lo + a_hi
    bk_ref[0] = (lowest // G) * len(PAIRS) + pair

    tb = x.shape[0]
    eye = (lax.broadcasted_iota(jnp.int32, (tb, tb), 0)
           == lax.broadcasted_iota(jnp.int32, (tb, tb), 1))

    def column(row):
        return jnp.sum(jnp.where(eye, row, 0.0), axis=1, keepdims=True)

    lane = lax.broadcasted_iota(jnp.int32, (tb, LANES), 1)
    f_ref[0, :, D:] = jnp.where(lane < LANES // 2, column(a_lo / tot), column(a_hi / tot))


def _out_router(o, x, ctx, ctx_block0, w_o, mod3, g_ffn, rwt, rb, n_lat, ctx_row):
    B, NT, D = o.shape
    TB = TOKEN_BLOCK
    nb = NT // TB
    tok = pl.BlockSpec((1, TB, D), lambda b, t: (b, t, 0))
    row = pl.BlockSpec((1, 1, TB), lambda b, t: (b * nb + t, 0, 0))
    const2 = lambda shape: pl.BlockSpec(shape, lambda b, t: (0, 0))
    rows = lambda dt: jax.ShapeDtypeStruct((B * nb, 1, TB), dt)
    return pl.pallas_call(
        functools.partial(_out_router_kernel, n_lat=n_lat),
        grid=(B, nb),
        in_specs=[
            tok, *_stream_specs(n_lat, ctx_block0), const2((D, D)),
            _mod_spec(2, n_lat, ctx_row), const2((1, D)),
            _mod_spec(4, n_lat, ctx_row), _mod_spec(3, n_lat, ctx_row),
            const2((D, LANES)), const2((N_EXPERTS, 1)),
        ],
        out_specs=[tok, pl.BlockSpec((1, TB, D + LANES), lambda b, t: (b, t, 0)), row],
        out_shape=[jax.ShapeDtypeStruct((B, NT, D), F32),
                   jax.ShapeDtypeStruct((B, NT, D + LANES), F32), rows(jnp.int32)],
        compiler_params=_cparams(("parallel", "arbitrary")),
        name="out_proj_router",
    )(o, x, ctx, w_o, mod3, g_ffn, mod3, mod3, rwt, rb)


def _routing_tables(bucket, n_tokens):
    tm = MOE_TILE
    max_tiles = n_tokens // tm + N_BUCKETS
    n_rows = max_tiles * tm
    onehot = (bucket[:, None] == jnp.arange(N_BUCKETS, dtype=jnp.int32)[None, :]).astype(jnp.int32)
    count = jnp.sum(onehot, axis=0)
    rank = jnp.sum(jnp.cumsum(onehot, axis=0) * onehot, axis=1) - 1
    btiles = (count + tm - 1) // tm
    tile_end = jnp.cumsum(btiles)
    tile_start = tile_end - btiles
    dest = tile_start[bucket] * tm + rank
    token_at = jnp.zeros((n_rows,), jnp.int32).at[dest].set(jnp.arange(n_tokens, dtype=jnp.int32))
    tile = jnp.arange(max_tiles, dtype=jnp.int32)
    tile_bucket = jnp.minimum(jnp.sum((tile[:, None] >= tile_end[None, :]).astype(jnp.int32), axis=1),
                              N_BUCKETS - 1)
    in_bucket = tile - tile_start[tile_bucket]
    n_valid = jnp.clip(count[tile_bucket] - in_bucket * tm, 0, tm)
    n_valid = jnp.where(tile < tile_end[-1], n_valid, 0).astype(jnp.int32)
    grp = tile_bucket // len(PAIRS)
    pair = tile_bucket % len(PAIRS)
    lo_tab = jnp.array([p[0] for p in PAIRS], jnp.int32)
    hi_tab = jnp.array([p[1] for p in PAIRS], jnp.int32)
    tile_lo = grp * EXPERTS_PER_GROUP + lo_tab[pair]
    tile_hi = grp * EXPERTS_PER_GROUP + hi_tab[pair]
    return token_at, n_valid, tile_lo, tile_hi


def _expert_kernel(tok_ref, nv_ref, tlo_ref, thi_ref, f_hbm, wg_lo_ref, wu_lo_ref, wd_lo_ref,
                   wg_hi_ref, wu_hi_ref, wd_hi_ref, y_hbm, xbuf, ybuf, gsem, ssem):
    TM, D = MOE_TILE, D_MODEL
    i = pl.program_id(0)
    n = pl.num_programs(0)
    slot = lax.rem(i, 2)
    other = 1 - slot

    def gather_copy(tile, r, s):
        return pltpu.make_async_copy(f_hbm.at[pl.ds(tok_ref[tile * TM + r], 1), :],
                                     xbuf.at[s, pl.ds(r, 1), :], gsem.at[s])

    def scatter_copy(tile, r, s):
        return pltpu.make_async_copy(ybuf.at[s, pl.ds(r, 1), :],
                                     y_hbm.at[pl.ds(tok_ref[tile * TM + r], 1), :], ssem.at[s])

    def for_rows(n_rows, fn):
        U = MOE_DMA_UNROLL
        for g in range(TM // U):
            @pl.when((g + 1) * U <= n_rows)
            def _():
                for u in range(U):
                    fn(g * U + u)

        def single(r, carry):
            fn(r)
            return carry

        lax.fori_loop((n_rows // U) * U, n_rows, single, 0)

    def wait_rows(n_rows, whole_tile_copy, row_copy):
        @pl.when(n_rows == TM)
        def _():
            whole_tile_copy.wait()

        @pl.when(n_rows < TM)
        def _():
            for_rows(n_rows, lambda r: row_copy(r).wait())

    def start_gather(tile, s):
        for_rows(nv_ref[tile], lambda r: gather_copy(tile, r, s).start())

    def wait_gather(tile, s):
        wait_rows(nv_ref[tile],
                  pltpu.make_async_copy(f_hbm.at[pl.ds(0, TM), :], xbuf.at[s], gsem.at[s]),
                  lambda r: gather_copy(tile, r, s))

    def start_scatter(tile, s):
        for_rows(nv_ref[tile], lambda r: scatter_copy(tile, r, s).start())

    def wait_scatter(tile, s):
        wait_rows(nv_ref[tile],
                  pltpu.make_async_copy(ybuf.at[s], y_hbm.at[pl.ds(0, TM), :], ssem.at[s]),
                  lambda r: scatter_copy(tile, r, s))

    @pl.when(i == 0)
    def _():
        xbuf[...] = jnp.zeros_like(xbuf)
        start_gather(0, 0)

    wait_gather(i, slot)

    @pl.when(i >= 2)
    def _():
        wait_scatter(i - 2, slot)

    @pl.when(i + 1 < n)
    def _():
        start_gather(i + 1, other)

    @pl.when(nv_ref[i] > 0)
    def _():
        xe = xbuf[slot]
        x = xe[:, :D].astype(BF16)

        def ffn(wg_ref, wu_ref, wd_ref):
            hid = (_silu(jnp.dot(x, wg_ref[0, 0], preferred_element_type=F32))
                   * jnp.dot(x, wu_ref[0, 0], preferred_element_type=F32))
            return jnp.dot(hid.astype(BF16), wd_ref[0, 0], preferred_element_type=F32)

        ybuf[slot] = (xe[:, D:D + 1] * ffn(wg_lo_ref, wu_lo_ref, wd_lo_ref)
                      + xe[:, D + LANES // 2:D + LANES // 2 + 1] * ffn(wg_hi_ref, wu_hi_ref, wd_hi_ref))
        start_scatter(i, slot)

    @pl.when(i == n - 1)
    def _():
        wait_scatter(i - 1, other)
        wait_scatter(i, slot)


def _expert_ffn(tables, f_ext, layer, w_gate, w_up, w_down):
    token_at, n_valid, tile_lo, tile_hi = tables
    N = f_ext.shape[0]
    D = D_MODEL
    tm = MOE_TILE
    lo = lambda i, tok, nv, tlo, thi: (layer, tlo[i], 0, 0)
    hi = lambda i, tok, nv, tlo, thi: (layer, thi[i], 0, 0)
    w_in, w_out = (1, 1, D, D_FF), (1, 1, D_FF, D)
    return pl.pallas_call(
        _expert_kernel,
        grid_spec=pltpu.PrefetchScalarGridSpec(
            num_scalar_prefetch=4,
            grid=(n_valid.shape[0],),
            in_specs=[
                pl.BlockSpec(memory_space=pl.ANY),
                pl.BlockSpec(w_in, lo), pl.BlockSpec(w_in, lo), pl.BlockSpec(w_out, lo),
                pl.BlockSpec(w_in, hi), pl.BlockSpec(w_in, hi), pl.BlockSpec(w_out, hi),
            ],
            out_specs=pl.BlockSpec(memory_space=pl.ANY),
            scratch_shapes=[pltpu.VMEM((2, tm, D + LANES), F32), pltpu.VMEM((2, tm, D), F32),
                            pltpu.SemaphoreType.DMA((2,)), pltpu.SemaphoreType.DMA((2,))],
        ),
        out_shape=jax.ShapeDtypeStruct((N, D), F32),
        compiler_params=_cparams(("arbitrary",)),
        name="moe_expert_ffn",
    )(token_at, n_valid, tile_lo, tile_hi, f_ext, w_gate, w_up, w_down, w_gate, w_up, w_down)


def _combine_kernel(x_ref, gf_ref, y_ref, o_ref):
    o_ref[0] = x_ref[0] + gf_ref[0] * y_ref[...]


def _moe_combine(xa, mod3, y, n_lat, ctx_row, n_blocks_out):
    B, NT, D = xa.shape
    TB = TOKEN_BLOCK
    nb = NT // TB
    tok = pl.BlockSpec((1, TB, D), lambda b, t: (b, t, 0))
    return pl.pallas_call(
        _combine_kernel,
        grid=(B, n_blocks_out),
        in_specs=[tok, _mod_spec(5, n_lat, ctx_row),
                  pl.BlockSpec((TB, D), lambda b, t: (b * nb + t, 0))],
        out_specs=tok,
        out_shape=jax.ShapeDtypeStruct((B, n_blocks_out * TB, D), F32),
        compiler_params=_cparams(("parallel", "arbitrary")),
        name="moe_combine",
    )(xa, mod3, y)


def _hgrn_proj_kernel(x_ref, g_ref, sc_ref, sh_ref, w_ref, lb_ref,
                      q_ref, v_ref, kf_ref, lff_ref, kb_ref, lfb_ref, gate_ref):
    HK = HG_HEADS * HG_DIM
    h = _norm_mod(x_ref[0], g_ref[...], sc_ref[0], sh_ref[0]).astype(BF16)

    W = HGRN_PROJ_PIECE

    def proj(part, c):
        return jnp.dot(h, w_ref[:, part * HK + c:part * HK + c + W], preferred_element_type=F32)

    def forget(z, lbd, k_ref, lf_ref, cols):
        e = jnp.exp(-jnp.abs(z))
        t = 1.0 + e
        k_ref[0, :, cols] = (1.0 - lbd) * (jnp.where(z >= 0.0, e, 1.0) / t)
        a = jnp.log(lbd)
        b = jnp.log1p(-lbd) + (jnp.minimum(z, 0.0) - jnp.log(t))
        lf_ref[0, :, cols] = jnp.maximum(a, b) + jnp.log(1.0 + jnp.exp(-jnp.abs(a - b)))

    def finish(part, c, y):
        cols = slice(c, c + W)
        if part == 0:
            q_ref[0, :, cols] = _silu(y)
        elif part == 1:
            v_ref[0, :, cols] = y.astype(BF16)
        elif part == 2:
            forget(y, lb_ref[0:1, cols], kf_ref, lff_ref, cols)
        elif part == 3:
            forget(y, lb_ref[1:2, cols], kb_ref, lfb_ref, cols)
        else:
            gate_ref[0, :, cols] = y

    pieces = [(part, c) for c in range(0, HK, W) for part in (2, 0, 3, 1, 4)]
    nxt = proj(*pieces[0])
    for k, piece in enumerate(pieces):
        cur = nxt
        if k + 1 < len(pieces):
            nxt = proj(*pieces[k + 1])
        finish(*piece, cur)


def _hgrn_project(xa, g, mod3, w_in, lb, n_lat, ctx_row):
    B, NT, D = xa.shape
    TB = TOKEN_BLOCK
    HK = HG_HEADS * HG_DIM
    tok = pl.BlockSpec((1, TB, D), lambda b, t: (b, t, 0))
    tok_o = pl.BlockSpec((1, TB, HK), lambda b, t: (b, t, 0))
    const2 = lambda shape: pl.BlockSpec(shape, lambda b, t: (0, 0))
    out = jax.ShapeDtypeStruct((B, NT, HK), F32)
    return pl.pallas_call(
        _hgrn_proj_kernel,
        grid=(B, NT // TB),
        in_specs=[
            tok, const2((1, D)), _mod_spec(1, n_lat, ctx_row), _mod_spec(0, n_lat, ctx_row),
            pl.BlockSpec((D, 5 * HK), lambda b, t: (0, 0), pipeline_mode=pl.Buffered(1)),
            const2((2, HK)),
        ],
        out_specs=[tok_o] * 7,
        out_shape=[out, jax.ShapeDtypeStruct((B, NT, HK), BF16)] + [out] * 5,
        compiler_params=_cparams(("parallel", "arbitrary")),
        name="hgrn_proj",
    )(xa, g, mod3, mod3, w_in, lb)


def _scan_chunk_scores(q, k, vb, a, forward):
    C, SB = SCAN_CHUNK, SCAN_SUB
    row = lax.broadcasted_iota(jnp.int32, a.shape, 0)
    first = row < SB
    mid_row = SB // 2
    m = jnp.where(first, a[mid_row:mid_row + 1], a[SB + mid_row:SB + mid_row + 1])
    qd = q * jnp.exp(jnp.minimum(a - m, EXP_CLAMP))
    kd = k * jnp.exp(jnp.minimum(m - a, EXP_CLAMP))
    edge = a[SB - 1:SB] if forward else a[SB:SB + 1]
    e_x = jnp.exp(-jnp.abs(a - edge))
    nt = (((1,), (1,)), ((), ()))
    s_d = lax.dot_general(qd.astype(BF16), kd.astype(BF16), nt, preferred_element_type=F32)
    s_x = lax.dot_general((q * e_x).astype(BF16), (k * e_x).astype(BF16), nt,
                          preferred_element_type=F32)
    a_out = a[C - 1:C] if forward else a[0:1]
    k_out = k * jnp.exp(a_out - a)
    upd = lax.dot_general(vb, k_out.astype(BF16), (((0,), (0,)), ((), ())),
                          preferred_element_type=F32)
    return s_d, s_x, q * jnp.exp(a), upd, jnp.exp(a_out)


def _scan_chunk_intra(s_d, s_x, vb, forward):
    SB = SCAN_SUB
    t_i = lax.broadcasted_iota(jnp.int32, s_d.shape, 0)
    s_i = lax.broadcasted_iota(jnp.int32, s_d.shape, 1)
    same = (t_i < SB) == (s_i < SB)
    if forward:
        causal, cross = s_i <= t_i, (t_i >= SB) & (s_i < SB)
    else:
        causal, cross = s_i >= t_i, (t_i < SB) & (s_i >= SB)
    scores = jnp.where(same & causal, s_d, jnp.where(cross, s_x, 0.0))
    return jnp.dot(scores.astype(BF16), vb, preferred_element_type=F32)


def _hgrn_scan_kernel(q_ref, v_ref, kf_ref, lff_ref, kb_ref, lfb_ref, gate_ref, gain_ref,
                      o_ref, acc_ref, qe_ref, upd_ref, dec_ref, inter_ref, *, n_chunks, n_lat_chunks):
    C, G = SCAN_CHUNK, SCAN_GROUP
    R = C * G
    pos = lax.rem(lax.broadcasted_iota(jnp.int32, (R, HG_DIM), 0), C)

    def cumsum(lf, forward):
        x = lf
        step = 1
        while step < C:
            if forward:
                x = x + jnp.where(pos >= step, pltpu.roll(x, step, 0), 0.0)
            else:
                x = x + jnp.where(pos < C - step, pltpu.roll(x, R - step, 0), 0.0)
            step *= 2
        return x

    def group_scores(g):
        rows = pl.ds(pl.multiple_of(g * R, R), R)
        q = q_ref[0, rows, :]
        vb = v_ref[0, rows, :].astype(BF16)
        kf, kb = kf_ref[0, rows, :], kb_ref[0, rows, :]
        a_f = cumsum(lff_ref[0, rows, :], True)
        a_b = cumsum(lfb_ref[0, rows, :], False)
        res = []
        for ci in range(G):
            sl = slice(ci * C, (ci + 1) * C)
            res.append((vb[sl], _scan_chunk_scores(q[sl], kf[sl], vb[sl], a_f[sl], True),
                        _scan_chunk_scores(q[sl], kb[sl], vb[sl], a_b[sl], False)))
        return res

    def group_finish(g, res):
        o_sum = [_scan_chunk_intra(fw[0], fw[1], vb, True) + _scan_chunk_intra(bw[0], bw[1], vb, False)
                 for vb, fw, bw in res]
        for d in range(2):
            p = l = None
            for ci in (range(G) if d == 0 else reversed(range(G))):
                _, _, qe, upd, dec = res[ci][1 + d]
                crow = pl.ds(pl.multiple_of((g * G + ci) * C, C), C)
                if p is None:
                    qe_ref[d, crow, :] = qe.astype(BF16)
                    p, l = dec, upd
                else:
                    qe_ref[d, crow, :] = (qe * p).astype(BF16)
                    o_sum[ci] = o_sum[ci] + lax.dot_general(qe.astype(BF16), l.astype(BF16), nt,
                                                            preferred_element_type=F32)
                    p, l = p * dec, l * dec + upd
            upd_ref[d, g] = l
            dec_ref[d, pl.ds(g, 1), :] = p
        for ci in range(G):
            acc_ref[pl.ds(pl.multiple_of((g * G + ci) * C, C), C), :] = o_sum[ci]

    nt = (((1,), (1,)), ((), ()))
    n_groups = n_chunks // G
    n_trip = math.gcd(SCAN_TRIP, n_groups)

    def local(i, carry):
        nxt = group_scores(i * n_trip)
        for j in range(n_trip):
            cur = nxt
            if j + 1 < n_trip:
                nxt = group_scores(i * n_trip + j + 1)
            group_finish(i * n_trip + j, cur)
        return carry

    lax.fori_loop(0, n_groups // n_trip, local, 0)

    def carry_state(j, carry):
        st_f, st_b = carry
        gf = lax.rem(j + n_lat_chunks // G, n_groups)
        gb = n_groups - 1 - j
        rf = pl.ds(pl.multiple_of(gf * R, R), R)
        rb = pl.ds(pl.multiple_of(gb * R, R), R)
        inter_ref[0, rf, :] = lax.dot_general(qe_ref[0, rf, :], st_f.astype(BF16), nt,
                                              preferred_element_type=F32)
        inter_ref[1, rb, :] = lax.dot_general(qe_ref[1, rb, :], st_b.astype(BF16), nt,
                                              preferred_element_type=F32)
        st_f = st_f * dec_ref[0, pl.ds(gf, 1), :] + upd_ref[0, gf]
        st_b = st_b * dec_ref[1, pl.ds(gb, 1), :] + upd_ref[1, gb]
        return st_f, st_b

    zero = jnp.zeros((HG_DIM, HG_DIM), F32)
    lax.fori_loop(0, n_groups, carry_state, (zero, zero))
    o = acc_ref[...] + inter_ref[0] + inter_ref[1]
    o = o * lax.rsqrt(jnp.mean(o * o, axis=-1, keepdims=True) + EPS) * gain_ref[...]
    o_ref[0] = (o * _silu(gate_ref[0])).astype(BF16)


def _hgrn_scan(q, v, kf, lff, kb, lfb, gate, gain, t_lat):
    B, NT, HK = q.shape
    blk = pl.BlockSpec((1, NT, HG_DIM), lambda b, h: (b, 0, h))
    return pl.pallas_call(
        functools.partial(_hgrn_scan_kernel, n_chunks=NT // SCAN_CHUNK,
                          n_lat_chunks=t_lat // SCAN_CHUNK),
        grid=(B, HG_HEADS),
        in_specs=[blk] * 7 + [pl.BlockSpec((1, HG_DIM), lambda b, h: (0, 0))],
        out_specs=blk,
        out_shape=jax.ShapeDtypeStruct((B, NT, HK), BF16),
        scratch_shapes=[pltpu.VMEM((NT, HG_DIM), F32),
                        pltpu.VMEM((2, NT, HG_DIM), BF16),
                        pltpu.VMEM((2, NT // (SCAN_CHUNK * SCAN_GROUP), HG_DIM, HG_DIM), F32),
                        pltpu.VMEM((2, NT // (SCAN_CHUNK * SCAN_GROUP), HG_DIM), F32),
                        pltpu.VMEM((2, NT, HG_DIM), F32)],
        compiler_params=_cparams(("parallel", "arbitrary")),
        name="hgrn_scan",
    )(q, v, kf, lff, kb, lfb, gate, gain)


def _rope_tables(t_lat, t_ctx):
    rows = t_lat // GRID_W
    r = jnp.repeat(jnp.arange(rows, dtype=F32), GRID_W)
    col = jnp.tile(jnp.arange(GRID_W, dtype=F32), rows)
    n_pairs = DA_QK_DIM // 4
    inv = ROPE_BASE ** (-jnp.arange(n_pairs, dtype=F32) / n_pairs)
    ang = jnp.concatenate([r[:, None] * inv, col[:, None] * inv], axis=-1)
    cos, sin = jnp.cos(ang), jnp.sin(ang)
    cos_l = jnp.tile(cos, (1, LANES // cos.shape[1]))
    sin_l = jnp.tile(jnp.concatenate([-sin, sin], axis=-1), (1, LANES // (2 * sin.shape[1])))
    cos_l = jnp.concatenate([cos_l, jnp.ones((t_ctx, LANES), F32)], axis=0)
    sin_l = jnp.concatenate([sin_l, jnp.zeros((t_ctx, LANES), F32)], axis=0)
    return cos_l, sin_l


def kernel(x, c, ctx, c_ctx, ada_w, ada_b, norm_mix, norm_ffn, attn_w_qkv, attn_w_o, attn_q_norm,
           attn_k_norm, attn_sub_norm, attn_lambda, hgrn_w_in, hgrn_w_o, hgrn_out_norm,
           hgrn_lb_gamma, router_w, router_bias, moe_w_gate, moe_w_up, moe_w_down):
    B, T, D = x.shape
    Tc = ctx.shape[1]
    TB = TOKEN_BLOCK
    assert D == D_MODEL and T % TB == 0 and Tc % TB == 0 and T % GRID_W == 0
    assert ada_w.shape[0] == DEPTH == 2
    assert T % (SCAN_CHUNK * SCAN_GROUP) == 0 and Tc % (SCAN_CHUNK * SCAN_GROUP) == 0
    assert T % ATTN_Q_BLOCK == 0
    NT = T + Tc
    n_lat = T // TB
    n_tokens = B * NT
    ctx_row = B

    n_rows = -(-(B + 1) // 8) * 8
    cvec = jnp.concatenate([c, c_ctx[None, :], jnp.zeros((n_rows - B - 1, D), F32)], axis=0)
    mod = _modulation(cvec, ada_w, ada_b)

    wg_all, wu_all, wd_all = (w.astype(BF16) for w in (moe_w_gate, moe_w_up, moe_w_down))
    rwt = jnp.pad(router_w.astype(F32), ((0, 0), (0, LANES - N_EXPERTS)))
    rb = router_bias.reshape(N_EXPERTS, 1)

    p = jax.nn.softmax(hgrn_lb_gamma.astype(F32), axis=1)
    cum = jnp.cumsum(p, axis=1)
    lb_all = cum - cum[:, :1]

    cos_t, sin_t = _rope_tables(T, Tc)
    lane = jnp.arange(LANES)
    bd = (lane[:, None] // DA_QK_DIM == lane[None, :] // DA_QK_DIM).astype(BF16)

    xa = None
    for i in range(DEPTH):
        mod3 = mod[i].reshape(n_rows, 1, 6 * D)
        last = i == DEPTH - 1
        j = i // 2
        g_mix = norm_mix[i].reshape(1, D)
        if i % 2 == 0:
            assert i == 0
            lam_init = 0.8 - 0.6 * math.exp(-0.3 * i)
            q, k, v = _attn_project(
                x, ctx, g_mix, mod3, attn_w_qkv[j].astype(BF16),
                jnp.tile(attn_q_norm[j], LANES // DA_QK_DIM).reshape(1, LANES),
                jnp.tile(attn_k_norm[j], LANES // DA_QK_DIM).reshape(1, LANES),
                cos_t, sin_t, bd, n_lat, ctx_row)
            o = _attention(attn_lambda[j], q, k, v, attn_sub_norm[j].reshape(1, DA_V_DIM),
                           T, lam_init)
            w_o = attn_w_o[j]
        else:
            parts = _hgrn_project(xa, g_mix, mod3, hgrn_w_in[j].astype(BF16), lb_all[:, i],
                                  n_lat, ctx_row)
            o = _hgrn_scan(*parts, hgrn_out_norm[j].reshape(1, HG_DIM), T)
            w_o = hgrn_w_o[j]
        stream = (x, ctx, 0) if xa is None else (xa, xa, n_lat)
        xa, f_ext, bucket = _out_router(
            o, *stream, w_o.astype(BF16), mod3, norm_ffn[i].reshape(1, D), rwt, rb, n_lat, ctx_row)
        tables = _routing_tables(bucket.reshape(n_tokens), n_tokens)
        y = _expert_ffn(tables, f_ext.reshape(n_tokens, D + LANES), i, wg_all, wu_all, wd_all)
        xa = _moe_combine(xa, mod3, y, n_lat, ctx_row, n_lat if last else NT // TB)
    return xa
```

```python
import functools
import math

import jax
import jax.numpy as jnp
from jax import lax
from jax.experimental import pallas as pl
from jax.experimental.pallas import tpu as pltpu

F32 = jnp.float32
BF16 = jnp.bfloat16
HIGHEST = lax.Precision.HIGHEST

D_MODEL = 1024
DEPTH = 2
GRID_W = 64
DA_HEADS = 8
DA_QK_DIM = 64
DA_V_DIM = 128
ROPE_BASE = 10000.0
HG_HEADS = 8
HG_DIM = 128
N_EXPERTS = 16
N_GROUPS = 4
EXPERTS_PER_GROUP = 4
D_FF = 512
EPS = 1e-6

LANES = 128
TOKEN_BLOCK = 256
ATTN_Q_BLOCK = 256
ATTN_GROUP = 8
MOE_TILE = 256
MOE_DMA_UNROLL = 32
MOE_Y_SLOTS = 3
PAIRS = [(i, j) for i in range(EXPERTS_PER_GROUP) for j in range(i + 1, EXPERTS_PER_GROUP)]
N_BUCKETS = N_GROUPS * len(PAIRS)
HGRN_PROJ_PIECE = 256
SCAN_CHUNK = 64
SCAN_SUB = SCAN_CHUNK // 2
SCAN_GROUP = 4
SCAN_TRIP = 3
EXP_CLAMP = 80.0
VMEM_LIMIT = 56 * 1024 * 1024


def _cparams(sem):
    return pltpu.CompilerParams(dimension_semantics=sem, vmem_limit_bytes=VMEM_LIMIT)


def _silu(x):
    return x * jax.nn.sigmoid(x)


def _norm_mod(x, g, sc, sh):
    y = x * lax.rsqrt(jnp.mean(x * x, axis=-1, keepdims=True) + EPS)
    return (y * g) * (1.0 + sc) + sh


def _mod_kernel(c_ref, w_ref, b_ref, o_ref):
    o_ref[0] = jnp.dot(_silu(c_ref[...]), w_ref[0], preferred_element_type=F32,
                       precision=HIGHEST) + b_ref[0]


def _modulation(cvec, ada_w, ada_b):
    R, D = cvec.shape
    depth, _, n6 = ada_w.shape
    tn = 1024
    return pl.pallas_call(
        _mod_kernel,
        grid=(depth, n6 // tn),
        in_specs=[
            pl.BlockSpec((R, D), lambda i, j: (0, 0)),
            pl.BlockSpec((1, D, tn), lambda i, j: (i, 0, j)),
            pl.BlockSpec((1, 1, tn), lambda i, j: (i, 0, j)),
        ],
        out_specs=pl.BlockSpec((1, R, tn), lambda i, j: (i, 0, j)),
        out_shape=jax.ShapeDtypeStruct((depth, R, n6), F32),
        compiler_params=_cparams(("arbitrary", "arbitrary")),
        name="adaln_mod",
    )(cvec, ada_w, ada_b.reshape(depth, 1, n6))


def _mod_spec(col, n_lat, ctx_row):
    return pl.BlockSpec((1, 1, D_MODEL), lambda b, t: (jnp.where(t < n_lat, b, ctx_row), 0, col))


def _stream_specs(n_lat, ctx_block0):
    blk = (1, TOKEN_BLOCK, D_MODEL)
    return (pl.BlockSpec(blk, lambda b, t: (b, jnp.minimum(t, n_lat - 1), 0)),
            pl.BlockSpec(blk, lambda b, t: (b, jnp.maximum(t - n_lat, 0) + ctx_block0, 0)))


def _stream_block(lat_ref, ctx_ref, n_lat):
    return jnp.where(pl.program_id(1) < n_lat, lat_ref[0], ctx_ref[0])


def _attn_proj_kernel(x_ref, c_ref, g_ref, sc_ref, sh_ref, w_ref, qg_ref, kg_ref, cos_ref, sin_ref,
                      bd_ref, q_ref, k_ref, v_ref, *, n_lat):
    D = D_MODEL
    h = _norm_mod(_stream_block(x_ref, c_ref, n_lat), g_ref[...], sc_ref[0], sh_ref[0])
    qkv = jnp.dot(h.astype(BF16), w_ref[...], preferred_element_type=F32)
    cos, sin, bd = cos_ref[...], sin_ref[...], bd_ref[...]
    lane = lax.broadcasted_iota(jnp.int32, cos.shape, 1)
    upper = (lane & (DA_QK_DIM // 2)) != 0

    def norm_rope(t, gain, scale):
        sq = t * t
        sq_hi = sq.astype(BF16)
        sq_lo = (sq - sq_hi.astype(F32)).astype(BF16)
        ss = (jnp.dot(sq_hi, bd, preferred_element_type=F32)
              + jnp.dot(sq_lo, bd, preferred_element_type=F32))
        tn = t * lax.rsqrt(ss * (1.0 / DA_QK_DIM) + EPS) * gain
        partner = jnp.where(upper, pltpu.roll(tn, DA_QK_DIM // 2, 1),
                            pltpu.roll(tn, LANES - DA_QK_DIM // 2, 1))
        return (tn * cos + partner * sin) * scale

    for j in range(D // LANES):
        sl = slice(j * LANES, (j + 1) * LANES)
        q_ref[0, :, sl] = norm_rope(qkv[:, j * LANES:(j + 1) * LANES], qg_ref[...],
                                    math.log2(math.e) / math.sqrt(DA_QK_DIM)).astype(BF16)
        k_ref[0, j] = norm_rope(qkv[:, D + j * LANES:D + (j + 1) * LANES], kg_ref[...],
                                1.0).T.astype(BF16)
    v_ref[0] = qkv[:, 2 * D:].astype(BF16)


def _attn_project(x, ctx, g, mod3, w_qkv, qg, kg, cos_t, sin_t, bd, n_lat, ctx_row):
    B, T, D = x.shape
    NT = T + ctx.shape[1]
    TB = TOKEN_BLOCK
    tok = pl.BlockSpec((1, TB, D), lambda b, t: (b, t, 0))
    const2 = lambda shape: pl.BlockSpec(shape, lambda b, t: (0, 0))
    out = jax.ShapeDtypeStruct((B, NT, D), BF16)
    return pl.pallas_call(
        functools.partial(_attn_proj_kernel, n_lat=n_lat),
        grid=(B, NT // TB),
        in_specs=[
            *_stream_specs(n_lat, 0), const2((1, D)),
            _mod_spec(1, n_lat, ctx_row), _mod_spec(0, n_lat, ctx_row),
            const2((D, 3 * D)), const2((1, LANES)), const2((1, LANES)),
            pl.BlockSpec((TB, LANES), lambda b, t: (t, 0)),
            pl.BlockSpec((TB, LANES), lambda b, t: (t, 0)),
            const2((LANES, LANES)),
        ],
        out_specs=[tok, pl.BlockSpec((1, DA_HEADS, LANES, TB), lambda b, t: (b, 0, 0, t)), tok],
        out_shape=[out, jax.ShapeDtypeStruct((B, DA_HEADS, LANES, NT), BF16), out],
        compiler_params=_cparams(("parallel", "arbitrary")),
        name="attn_qkv_proj",
    )(x, ctx, g, mod3, mod3, w_qkv, qg, kg, cos_t, sin_t, bd)


def _attn_kernel(lam_ref, q_ref, kt_ref, v_ref, sn_ref, o_ref, *, t_lat, lam_init):
    QB = ATTN_Q_BLOCK
    lp = lam_ref[...]
    lam = (jnp.exp(jnp.sum(lp[0:1] * lp[1:2], keepdims=True))
           - jnp.exp(jnp.sum(lp[2:3] * lp[3:4], keepdims=True)) + lam_init)

    def scores(q, kt):
        lane = lax.broadcasted_iota(jnp.int32, q.shape, 1)
        zero = jnp.zeros_like(q)
        return (jnp.dot(jnp.where(lane < DA_QK_DIM, q, zero), kt, preferred_element_type=F32),
                jnp.dot(jnp.where(lane >= DA_QK_DIM, q, zero), kt, preferred_element_type=F32))

    def finish(s, v):
        def softmax_parts(sm):
            e = jnp.exp2(sm - jnp.max(sm, axis=-1, keepdims=True))
            return e, jnp.sum(e, axis=-1, keepdims=True)

        e0, l0 = softmax_parts(s[0])
        e1, l1 = softmax_parts(s[1])
        a = e0 - (lam * l0 / l1) * e1
        o = jnp.dot(a.astype(BF16), v, preferred_element_type=F32) * (1.0 / l0)
        o = o * lax.rsqrt(jnp.mean(o * o, axis=-1, keepdims=True) + EPS) * sn_ref[...]
        return (o * (1.0 - lam_init)).astype(BF16)

    G = math.gcd(ATTN_GROUP, t_lat // QB)

    def latent_group(i, carry):
        rows = [pl.ds(pl.multiple_of((G * i + j) * QB, QB), QB) for j in range(G)]
        s_next = scores(q_ref[0, rows[0], :], kt_ref[0, 0])
        for j in range(G):
            s_cur = s_next
            if j + 1 < G:
                s_next = scores(q_ref[0, rows[j + 1], :], kt_ref[0, 0])
            o_ref[0, rows[j], :] = finish(s_cur, v_ref[0])
        return carry

    lax.fori_loop(0, t_lat // (G * QB), latent_group, 0)
    o_ref[0, t_lat:, :] = finish(scores(q_ref[0, t_lat:, :], kt_ref[0, 0, :, t_lat:]),
                                 v_ref[0, t_lat:, :])


def _attention(lam_p, q, kt, v, sub_norm, t_lat, lam_init):
    B, NT, D = q.shape
    blk = pl.BlockSpec((1, NT, LANES), lambda b, h: (b, 0, h))
    return pl.pallas_call(
        functools.partial(_attn_kernel, t_lat=t_lat, lam_init=lam_init),
        grid=(B, DA_HEADS),
        in_specs=[
            pl.BlockSpec(lam_p.shape, lambda b, h: (0, 0)),
            blk, pl.BlockSpec((1, 1, LANES, NT), lambda b, h: (b, h, 0, 0)), blk,
            pl.BlockSpec((1, LANES), lambda b, h: (0, 0)),
        ],
        out_specs=blk,
        out_shape=jax.ShapeDtypeStruct((B, NT, D), BF16),
        compiler_params=_cparams(("parallel", "arbitrary")),
        name="diff_attention",
    )(lam_p, q, kt, v, sub_norm)


def _out_router_kernel(o_ref, x_ref, c_ref, w_ref, gm_ref, g_ref, sc_ref, sh_ref, rw_ref, rb_ref,
                       xo_ref, f_ref, bk_ref, *, n_lat):
    D = D_MODEL
    out = jnp.dot(o_ref[0], w_ref[...], preferred_element_type=F32)
    x = _stream_block(x_ref, c_ref, n_lat) + gm_ref[0] * out
    xo_ref[0] = x
    f = _norm_mod(x, g_ref[...], sc_ref[0], sh_ref[0])
    f_ref[0, :, :D] = f
    def split3(a):
        a1 = a.astype(BF16)
        r = a - a1.astype(F32)
        a2 = r.astype(BF16)
        return a1, a2, (r - a2.astype(F32)).astype(BF16)

    f1, f2, f3 = split3(f)
    w1, w2, w3 = split3(rw_ref[...])
    mm = functools.partial(jnp.dot, preferred_element_type=F32)
    logits_t = ((mm(f3, w1) + mm(f2, w2) + mm(f1, w3)) + (mm(f2, w1) + mm(f1, w2))) + mm(f1, w1)
    logits = logits_t.T[:N_EXPERTS]
    aff = jax.nn.sigmoid(logits)
    biased = aff + rb_ref[...]
    G, E = EXPERTS_PER_GROUP, N_EXPERTS
    row = lax.broadcasted_iota(jnp.int32, biased.shape, 0)
    member = lax.rem(row, G)
    group = row // G

    def shifted(x, k):
        return jnp.where(member + k < G, pltpu.roll(x, E - k, 0), pltpu.roll(x, G - k, 0))

    rank = jnp.zeros_like(row)
    for k in range(1, G):
        other = shifted(biased, k)
        ahead = (other > biased) | ((other == biased) & (member + k >= G))
        rank = rank + ahead.astype(jnp.int32)
    top2 = rank < 2
    t = jnp.where(top2, biased, 0.0)
    gscore = t
    for k in range(1, G):
        gscore = gscore + shifted(t, k)
    best = jnp.ones_like(top2)
    for m in range(1, N_GROUPS):
        other = pltpu.roll(gscore, G * m, 0)
        best = best & ((gscore > other) | ((gscore == other) & (group < m)))
    chosen = top2 & best
    lowest = jnp.min(jnp.where(chosen, row, E), axis=0, keepdims=True)
    is_lo = chosen & (row == lowest)
    is_hi = chosen & (row != lowest)
    a_lo = jnp.sum(jnp.where(is_lo, aff, 0.0), axis=0, keepdims=True)
    a_hi = jnp.sum(jnp.where(is_hi, aff, 0.0), axis=0, keepdims=True)
    m_lo = lax.rem(lowest, G)
    m_hi = jnp.sum(jnp.where(is_hi, member, 0), axis=0, keepdims=True)
    pair = m_lo * (2 * G - 1 - m_lo) // 2 + (m_hi - m_lo - 1)
    tot = a_lo + a_hi
    bk_ref[0] = (lowest // G) * len(PAIRS) + pair

    tb = x.shape[0]
    eye = (lax.broadcasted_iota(jnp.int32, (tb, tb), 0)
           == lax.broadcasted_iota(jnp.int32, (tb, tb), 1))

    def column(row):
        return jnp.sum(jnp.where(eye, row, 0.0), axis=1, keepdims=True)

    lane = lax.broadcasted_iota(jnp.int32, (tb, LANES), 1)
    f_ref[0, :, D:] = jnp.where(lane < LANES // 2, column(a_lo / tot), column(a_hi / tot))


def _out_router(o, x, ctx, ctx_block0, w_o, mod3, g_ffn, rwt, rb, n_lat, ctx_row):
    B, NT, D = o.shape
    TB = TOKEN_BLOCK
    nb = NT // TB
    tok = pl.BlockSpec((1, TB, D), lambda b, t: (b, t, 0))
    row = pl.BlockSpec((1, 1, TB), lambda b, t: (b * nb + t, 0, 0))
    const2 = lambda shape: pl.BlockSpec(shape, lambda b, t: (0, 0))
    rows = lambda dt: jax.ShapeDtypeStruct((B * nb, 1, TB), dt)
    return pl.pallas_call(
        functools.partial(_out_router_kernel, n_lat=n_lat),
        grid=(B, nb),
        in_specs=[
            tok, *_stream_specs(n_lat, ctx_block0), const2((D, D)),
            _mod_spec(2, n_lat, ctx_row), const2((1, D)),
            _mod_spec(4, n_lat, ctx_row), _mod_spec(3, n_lat, ctx_row),
            const2((D, LANES)), const2((N_EXPERTS, 1)),
        ],
        out_specs=[tok, pl.BlockSpec((1, TB, D + LANES), lambda b, t: (b, t, 0)), row],
        out_shape=[jax.ShapeDtypeStruct((B, NT, D), F32),
                   jax.ShapeDtypeStruct((B, NT, D + LANES), F32), rows(jnp.int32)],
        compiler_params=_cparams(("parallel", "arbitrary")),
        name="out_proj_router",
    )(o, x, ctx, w_o, mod3, g_ffn, mod3, mod3, rwt, rb)


def _routing_tables(bucket, n_tokens):
    tm = MOE_TILE
    max_tiles = n_tokens // tm + N_BUCKETS
    n_rows = max_tiles * tm
    onehot = (bucket[:, None] == jnp.arange(N_BUCKETS, dtype=jnp.int32)[None, :]).astype(jnp.int32)
    count = jnp.sum(onehot, axis=0)
    rank = jnp.sum(jnp.cumsum(onehot, axis=0) * onehot, axis=1) - 1
    btiles = (count + tm - 1) // tm
    tile_end = jnp.cumsum(btiles)
    tile_start = tile_end - btiles
    dest = tile_start[bucket] * tm + rank
    token_at = jnp.full((n_rows,), -1, jnp.int32).at[dest].set(jnp.arange(n_tokens, dtype=jnp.int32))
    is_pad = token_at < 0
    spare = n_tokens + jnp.cumsum(is_pad.astype(jnp.int32)) - 1
    gather_idx = jnp.where(is_pad, 0, token_at)
    scatter_idx = jnp.concatenate([n_rows + jnp.arange(tm, dtype=jnp.int32),
                                   jnp.where(is_pad, spare, token_at)])
    tile = jnp.arange(max_tiles, dtype=jnp.int32)
    tile_bucket = jnp.minimum(jnp.sum((tile[:, None] >= tile_end[None, :]).astype(jnp.int32), axis=1),
                              N_BUCKETS - 1)
    in_bucket = tile - tile_start[tile_bucket]
    n_valid = jnp.clip(count[tile_bucket] - in_bucket * tm, 0, tm)
    n_valid = jnp.where(tile < tile_end[-1], n_valid, 0).astype(jnp.int32)
    grp = tile_bucket // len(PAIRS)
    pair = tile_bucket % len(PAIRS)
    lo_tab = jnp.array([p[0] for p in PAIRS], jnp.int32)
    hi_tab = jnp.array([p[1] for p in PAIRS], jnp.int32)
    tile_lo = grp * EXPERTS_PER_GROUP + lo_tab[pair]
    tile_hi = grp * EXPERTS_PER_GROUP + hi_tab[pair]
    return gather_idx, scatter_idx, n_valid, tile_lo, tile_hi


def _expert_kernel(gidx_ref, sidx_ref, nv_ref, tlo_ref, thi_ref, f_hbm, wg_lo_ref, wu_lo_ref,
                   wd_lo_ref, wg_hi_ref, wu_hi_ref, wd_hi_ref, y_hbm, xbuf0, xbuf1, ybuf, gsem, ssem):
    TM, D = MOE_TILE, D_MODEL
    NY = ybuf.shape[0]
    i = pl.program_id(0)
    n = pl.num_programs(0)

    def gather_copy(tile, r, xdst, s):
        return pltpu.make_async_copy(f_hbm.at[pl.ds(gidx_ref[tile * TM + r], 1), :],
                                     xdst.at[pl.ds(r, 1), :], gsem.at[s])

    def scatter_copy(tile, r):
        s = lax.rem(tile + NY, NY)
        return pltpu.make_async_copy(ybuf.at[s, pl.ds(r, 1), :],
                                     y_hbm.at[pl.ds(sidx_ref[(tile + 1) * TM + r], 1), :], ssem.at[s])

    def for_rows(n_rows, fn):
        U = MOE_DMA_UNROLL
        for g in range(TM // U):
            @pl.when((g + 1) * U <= n_rows)
            def _():
                for u in range(U):
                    fn(g * U + u)

        def single(r, carry):
            fn(r)
            return carry

        lax.fori_loop((n_rows // U) * U, n_rows, single, 0)

    def start_gather(tile, n_rows, xdst, s):
        for_rows(n_rows, lambda r: gather_copy(tile, r, xdst, s).start())

    def wait_gather(tile, n_rows, xdst, s):
        @pl.when(n_rows == TM)
        def _():
            pltpu.make_async_copy(f_hbm.at[pl.ds(0, TM), :], xdst, gsem.at[s]).wait()

        @pl.when(n_rows < TM)
        def _():
            for_rows(n_rows, lambda r: gather_copy(tile, r, xdst, s).wait())

    def wait_scatter(tile):
        s = lax.rem(tile + NY, NY)
        pltpu.make_async_copy(ybuf.at[s], y_hbm.at[pl.ds(0, TM), :], ssem.at[s]).wait()

    def start_scatter(tile):
        for r in range(TM):
            scatter_copy(tile, r).start(priority=r % 2)

    @pl.when(i == 0)
    def _():
        ybuf[...] = jnp.zeros_like(ybuf)
        xbuf0[...] = jnp.zeros_like(xbuf0)
        xbuf1[...] = jnp.zeros_like(xbuf1)
        start_gather(0, nv_ref[0], xbuf0, 0)

    @pl.when(i >= NY - 1)
    def _():
        wait_scatter(i - NY)

    def step(x_cur, s_cur, x_nxt, s_nxt):
        wait_gather(i, nv_ref[i], x_cur, s_cur)
        nxt = jnp.minimum(i + 1, n - 1)
        start_gather(nxt, jnp.where(i + 1 < n, nv_ref[nxt], 0), x_nxt, s_nxt)
        xe = x_cur[...]
        x = xe[:, :D].astype(BF16)

        def ffn(wg_ref, wu_ref, wd_ref):
            hid = (_silu(jnp.dot(x, wg_ref[0, 0], preferred_element_type=F32))
                   * jnp.dot(x, wu_ref[0, 0], preferred_element_type=F32))
            return jnp.dot(hid.astype(BF16), wd_ref[0, 0], preferred_element_type=F32)

        y = (xe[:, D:D + 1] * ffn(wg_lo_ref, wu_lo_ref, wd_lo_ref)
             + xe[:, D + LANES // 2:D + LANES // 2 + 1] * ffn(wg_hi_ref, wu_hi_ref, wd_hi_ref))
        start_scatter(i - 1)
        ybuf[lax.rem(i, NY)] = y

    @pl.when(lax.rem(i, 2) == 0)
    def _():
        step(xbuf0, 0, xbuf1, 1)

    @pl.when(lax.rem(i, 2) == 1)
    def _():
        step(xbuf1, 1, xbuf0, 0)

    @pl.when(i == n - 1)
    def _():
        start_scatter(i)
        for back in range(NY - 1, -1, -1):
            wait_scatter(i - back)


def _expert_ffn(tables, f_ext, layer, w_gate, w_up, w_down):
    gather_idx, scatter_idx, n_valid, tile_lo, tile_hi = tables
    N = scatter_idx.shape[0]
    D = D_MODEL
    tm = MOE_TILE
    lo = lambda i, gi, si, nv, tlo, thi: (layer, tlo[i], 0, 0)
    hi = lambda i, gi, si, nv, tlo, thi: (layer, thi[i], 0, 0)
    w_in, w_out = (1, 1, D, D_FF), (1, 1, D_FF, D)
    return pl.pallas_call(
        _expert_kernel,
        grid_spec=pltpu.PrefetchScalarGridSpec(
            num_scalar_prefetch=5,
            grid=(tile_lo.shape[0],),
            in_specs=[
                pl.BlockSpec(memory_space=pl.ANY),
                pl.BlockSpec(w_in, lo), pl.BlockSpec(w_in, lo), pl.BlockSpec(w_out, lo),
                pl.BlockSpec(w_in, hi), pl.BlockSpec(w_in, hi), pl.BlockSpec(w_out, hi),
            ],
            out_specs=pl.BlockSpec(memory_space=pl.ANY),
            scratch_shapes=[pltpu.VMEM((tm, D + LANES), F32), pltpu.VMEM((tm, D + LANES), F32),
                            pltpu.VMEM((MOE_Y_SLOTS, tm, D), F32),
                            pltpu.SemaphoreType.DMA((2,)), pltpu.SemaphoreType.DMA((MOE_Y_SLOTS,))],
        ),
        out_shape=jax.ShapeDtypeStruct((N, D), F32),
        compiler_params=_cparams(("arbitrary",)),
        name="moe_expert_ffn",
    )(gather_idx, scatter_idx, n_valid, tile_lo, tile_hi, f_ext,
      w_gate, w_up, w_down, w_gate, w_up, w_down)


def _combine_kernel(x_ref, gf_ref, y_ref, o_ref):
    o_ref[0] = x_ref[0] + gf_ref[0] * y_ref[...]


def _moe_combine(xa, mod3, y, n_lat, ctx_row, n_blocks_out):
    B, NT, D = xa.shape
    TB = TOKEN_BLOCK
    nb = NT // TB
    tok = pl.BlockSpec((1, TB, D), lambda b, t: (b, t, 0))
    return pl.pallas_call(
        _combine_kernel,
        grid=(B, n_blocks_out),
        in_specs=[tok, _mod_spec(5, n_lat, ctx_row),
                  pl.BlockSpec((TB, D), lambda b, t: (b * nb + t, 0))],
        out_specs=tok,
        out_shape=jax.ShapeDtypeStruct((B, n_blocks_out * TB, D), F32),
        compiler_params=_cparams(("parallel", "arbitrary")),
        name="moe_combine",
    )(xa, mod3, y)


def _hgrn_proj_kernel(x_ref, y_ref, gf_ref, g_ref, sc_ref, sh_ref, w_ref, lb_ref,
                      xo_ref, q_ref, v_ref, kf_ref, lff_ref, kb_ref, lfb_ref, gate_ref):
    HK = HG_HEADS * HG_DIM
    x = x_ref[0] + gf_ref[0] * y_ref[...]
    xo_ref[0] = x
    h = _norm_mod(x, g_ref[...], sc_ref[0], sh_ref[0]).astype(BF16)

    W = HGRN_PROJ_PIECE

    def proj(part, c):
        return jnp.dot(h, w_ref[:, part * HK + c:part * HK + c + W], preferred_element_type=F32)

    def forget(z, lbd, k_ref, lf_ref, cols):
        e = jnp.exp(-jnp.abs(z))
        t = 1.0 + e
        k_ref[0, :, cols] = (1.0 - lbd) * (jnp.where(z >= 0.0, e, 1.0) / t)
        a = jnp.log(lbd)
        b = jnp.log1p(-lbd) + (jnp.minimum(z, 0.0) - jnp.log(t))
        lf_ref[0, :, cols] = jnp.maximum(a, b) + jnp.log(1.0 + jnp.exp(-jnp.abs(a - b)))

    def finish(part, c, y):
        cols = slice(c, c + W)
        if part == 0:
            q_ref[0, :, cols] = _silu(y)
        elif part == 1:
            v_ref[0, :, cols] = y.astype(BF16)
        elif part == 2:
            forget(y, lb_ref[0:1, cols], kf_ref, lff_ref, cols)
        elif part == 3:
            forget(y, lb_ref[1:2, cols], kb_ref, lfb_ref, cols)
        else:
            gate_ref[0, :, cols] = y

    pieces = [(part, c) for c in range(0, HK, W) for part in (2, 0, 3, 1, 4)]
    nxt = proj(*pieces[0])
    for k, piece in enumerate(pieces):
        cur = nxt
        if k + 1 < len(pieces):
            nxt = proj(*pieces[k + 1])
        finish(*piece, cur)


def _hgrn_project(xa, y, mod_prev, g, mod3, w_in, lb, n_lat, ctx_row):
    B, NT, D = xa.shape
    TB = TOKEN_BLOCK
    nb = NT // TB
    HK = HG_HEADS * HG_DIM
    tok = pl.BlockSpec((1, TB, D), lambda b, t: (b, t, 0))
    tok_o = pl.BlockSpec((1, TB, HK), lambda b, t: (b, t, 0))
    const2 = lambda shape: pl.BlockSpec(shape, lambda b, t: (0, 0))
    out = jax.ShapeDtypeStruct((B, NT, HK), F32)
    return pl.pallas_call(
        _hgrn_proj_kernel,
        grid=(B, nb),
        in_specs=[
            tok, pl.BlockSpec((TB, D), lambda b, t: (b * nb + t, 0)), _mod_spec(5, n_lat, ctx_row),
            const2((1, D)), _mod_spec(1, n_lat, ctx_row), _mod_spec(0, n_lat, ctx_row),
            pl.BlockSpec((D, 5 * HK), lambda b, t: (0, 0), pipeline_mode=pl.Buffered(1)),
            const2((2, HK)),
        ],
        out_specs=[tok] + [tok_o] * 7,
        out_shape=[jax.ShapeDtypeStruct((B, NT, D), F32), out,
                   jax.ShapeDtypeStruct((B, NT, HK), BF16)] + [out] * 5,
        compiler_params=_cparams(("parallel", "arbitrary")),
        name="hgrn_proj",
    )(xa, y, mod_prev, g, mod3, mod3, w_in, lb)


def _scan_chunk_scores(q, k, vb, a, forward):
    C, SB = SCAN_CHUNK, SCAN_SUB
    row = lax.broadcasted_iota(jnp.int32, a.shape, 0)
    first = row < SB
    mid_row = SB // 2
    m = jnp.where(first, a[mid_row:mid_row + 1], a[SB + mid_row:SB + mid_row + 1])
    qd = q * jnp.exp(jnp.minimum(a - m, EXP_CLAMP))
    kd = k * jnp.exp(jnp.minimum(m - a, EXP_CLAMP))
    edge = a[SB - 1:SB] if forward else a[SB:SB + 1]
    e_x = jnp.exp(-jnp.abs(a - edge))
    nt = (((1,), (1,)), ((), ()))
    s_d = lax.dot_general(qd.astype(BF16), kd.astype(BF16), nt, preferred_element_type=F32)
    s_x = lax.dot_general((q * e_x).astype(BF16), (k * e_x).astype(BF16), nt,
                          preferred_element_type=F32)
    a_out = a[C - 1:C] if forward else a[0:1]
    k_out = k * jnp.exp(a_out - a)
    upd = lax.dot_general(vb, k_out.astype(BF16), (((0,), (0,)), ((), ())),
                          preferred_element_type=F32)
    return s_d, s_x, q * jnp.exp(a), upd, jnp.exp(a_out)


def _scan_chunk_intra(s_d, s_x, vb, forward):
    SB = SCAN_SUB
    t_i = lax.broadcasted_iota(jnp.int32, s_d.shape, 0)
    s_i = lax.broadcasted_iota(jnp.int32, s_d.shape, 1)
    same = (t_i < SB) == (s_i < SB)
    if forward:
        causal, cross = s_i <= t_i, (t_i >= SB) & (s_i < SB)
    else:
        causal, cross = s_i >= t_i, (t_i < SB) & (s_i >= SB)
    scores = jnp.where(same & causal, s_d, jnp.where(cross, s_x, 0.0))
    return jnp.dot(scores.astype(BF16), vb, preferred_element_type=F32)


def _hgrn_scan_kernel(q_ref, v_ref, kf_ref, lff_ref, kb_ref, lfb_ref, gate_ref, gain_ref,
                      o_ref, acc_ref, qe_ref, upd_ref, dec_ref, inter_ref, *, n_chunks, n_lat_chunks):
    C, G = SCAN_CHUNK, SCAN_GROUP
    R = C * G
    pos = lax.rem(lax.broadcasted_iota(jnp.int32, (R, HG_DIM), 0), C)

    def cumsum(lf, forward):
        x = lf
        step = 1
        while step < C:
            if forward:
                x = x + jnp.where(pos >= step, pltpu.roll(x, step, 0), 0.0)
            else:
                x = x + jnp.where(pos < C - step, pltpu.roll(x, R - step, 0), 0.0)
            step *= 2
        return x

    def group_scores(g):
        rows = pl.ds(pl.multiple_of(g * R, R), R)
        q = q_ref[0, rows, :]
        vb = v_ref[0, rows, :].astype(BF16)
        kf, kb = kf_ref[0, rows, :], kb_ref[0, rows, :]
        a_f = cumsum(lff_ref[0, rows, :], True)
        a_b = cumsum(lfb_ref[0, rows, :], False)
        res = []
        for ci in range(G):
            sl = slice(ci * C, (ci + 1) * C)
            res.append((vb[sl], _scan_chunk_scores(q[sl], kf[sl], vb[sl], a_f[sl], True),
                        _scan_chunk_scores(q[sl], kb[sl], vb[sl], a_b[sl], False)))
        return res

    def group_finish(g, res):
        o_sum = [_scan_chunk_intra(fw[0], fw[1], vb, True) + _scan_chunk_intra(bw[0], bw[1], vb, False)
                 for vb, fw, bw in res]
        for d in range(2):
            p = l = None
            for ci in (range(G) if d == 0 else reversed(range(G))):
                _, _, qe, upd, dec = res[ci][1 + d]
                crow = pl.ds(pl.multiple_of((g * G + ci) * C, C), C)
                if p is None:
                    qe_ref[d, crow, :] = qe.astype(BF16)
                    p, l = dec, upd
                else:
                    qe_ref[d, crow, :] = (qe * p).astype(BF16)
                    o_sum[ci] = o_sum[ci] + lax.dot_general(qe.astype(BF16), l.astype(BF16), nt,
                                                            preferred_element_type=F32)
                    p, l = p * dec, l * dec + upd
            upd_ref[d, g] = l
            dec_ref[d, pl.ds(g, 1), :] = p
        for ci in range(G):
            acc_ref[pl.ds(pl.multiple_of((g * G + ci) * C, C), C), :] = o_sum[ci]

    nt = (((1,), (1,)), ((), ()))
    n_groups = n_chunks // G
    n_trip = math.gcd(SCAN_TRIP, n_groups)

    def local(i, carry):
        nxt = group_scores(i * n_trip)
        for j in range(n_trip):
            cur = nxt
            if j + 1 < n_trip:
                nxt = group_scores(i * n_trip + j + 1)
            group_finish(i * n_trip + j, cur)
        return carry

    lax.fori_loop(0, n_groups // n_trip, local, 0)

    def carry_state(j, carry):
        st_f, st_b = carry
        gf = lax.rem(j + n_lat_chunks // G, n_groups)
        gb = n_groups - 1 - j
        rf = pl.ds(pl.multiple_of(gf * R, R), R)
        rb = pl.ds(pl.multiple_of(gb * R, R), R)
        inter_ref[0, rf, :] = lax.dot_general(qe_ref[0, rf, :], st_f.astype(BF16), nt,
                                              preferred_element_type=F32)
        inter_ref[1, rb, :] = lax.dot_general(qe_ref[1, rb, :], st_b.astype(BF16), nt,
                                              preferred_element_type=F32)
        st_f = st_f * dec_ref[0, pl.ds(gf, 1), :] + upd_ref[0, gf]
        st_b = st_b * dec_ref[1, pl.ds(gb, 1), :] + upd_ref[1, gb]
        return st_f, st_b

    zero = jnp.zeros((HG_DIM, HG_DIM), F32)
    lax.fori_loop(0, n_groups, carry_state, (zero, zero))
    o = acc_ref[...] + inter_ref[0] + inter_ref[1]
    o = o * lax.rsqrt(jnp.mean(o * o, axis=-1, keepdims=True) + EPS) * gain_ref[...]
    o_ref[0] = (o * _silu(gate_ref[0])).astype(BF16)


def _hgrn_scan(q, v, kf, lff, kb, lfb, gate, gain, t_lat):
    B, NT, HK = q.shape
    blk = pl.BlockSpec((1, NT, HG_DIM), lambda b, h: (b, 0, h))
    return pl.pallas_call(
        functools.partial(_hgrn_scan_kernel, n_chunks=NT // SCAN_CHUNK,
                          n_lat_chunks=t_lat // SCAN_CHUNK),
        grid=(B, HG_HEADS),
        in_specs=[blk] * 7 + [pl.BlockSpec((1, HG_DIM), lambda b, h: (0, 0))],
        out_specs=blk,
        out_shape=jax.ShapeDtypeStruct((B, NT, HK), BF16),
        scratch_shapes=[pltpu.VMEM((NT, HG_DIM), F32),
                        pltpu.VMEM((2, NT, HG_DIM), BF16),
                        pltpu.VMEM((2, NT // (SCAN_CHUNK * SCAN_GROUP), HG_DIM, HG_DIM), F32),
                        pltpu.VMEM((2, NT // (SCAN_CHUNK * SCAN_GROUP), HG_DIM), F32),
                        pltpu.VMEM((2, NT, HG_DIM), F32)],
        compiler_params=_cparams(("parallel", "arbitrary")),
        name="hgrn_scan",
    )(q, v, kf, lff, kb, lfb, gate, gain)


def _rope_tables(t_lat, t_ctx):
    rows = t_lat // GRID_W
    r = jnp.repeat(jnp.arange(rows, dtype=F32), GRID_W)
    col = jnp.tile(jnp.arange(GRID_W, dtype=F32), rows)
    n_pairs = DA_QK_DIM // 4
    inv = ROPE_BASE ** (-jnp.arange(n_pairs, dtype=F32) / n_pairs)
    ang = jnp.concatenate([r[:, None] * inv, col[:, None] * inv], axis=-1)
    cos, sin = jnp.cos(ang), jnp.sin(ang)
    cos_l = jnp.tile(cos, (1, LANES // cos.shape[1]))
    sin_l = jnp.tile(jnp.concatenate([-sin, sin], axis=-1), (1, LANES // (2 * sin.shape[1])))
    cos_l = jnp.concatenate([cos_l, jnp.ones((t_ctx, LANES), F32)], axis=0)
    sin_l = jnp.concatenate([sin_l, jnp.zeros((t_ctx, LANES), F32)], axis=0)
    return cos_l, sin_l


def kernel(x, c, ctx, c_ctx, ada_w, ada_b, norm_mix, norm_ffn, attn_w_qkv, attn_w_o, attn_q_norm,
           attn_k_norm, attn_sub_norm, attn_lambda, hgrn_w_in, hgrn_w_o, hgrn_out_norm,
           hgrn_lb_gamma, router_w, router_bias, moe_w_gate, moe_w_up, moe_w_down):
    B, T, D = x.shape
    Tc = ctx.shape[1]
    TB = TOKEN_BLOCK
    assert D == D_MODEL and T % TB == 0 and Tc % TB == 0 and T % GRID_W == 0
    assert ada_w.shape[0] == DEPTH == 2
    assert T % (SCAN_CHUNK * SCAN_GROUP) == 0 and Tc % (SCAN_CHUNK * SCAN_GROUP) == 0
    assert T % ATTN_Q_BLOCK == 0
    NT = T + Tc
    n_lat = T // TB
    n_tokens = B * NT
    ctx_row = B

    n_rows = -(-(B + 1) // 8) * 8
    cvec = jnp.concatenate([c, c_ctx[None, :], jnp.zeros((n_rows - B - 1, D), F32)], axis=0)
    mod = _modulation(cvec, ada_w, ada_b)

    wg_all, wu_all, wd_all = (w.astype(BF16) for w in (moe_w_gate, moe_w_up, moe_w_down))
    rwt = jnp.pad(router_w.astype(F32), ((0, 0), (0, LANES - N_EXPERTS)))
    rb = router_bias.reshape(N_EXPERTS, 1)

    p = jax.nn.softmax(hgrn_lb_gamma.astype(F32), axis=1)
    cum = jnp.cumsum(p, axis=1)
    lb_all = cum - cum[:, :1]

    cos_t, sin_t = _rope_tables(T, Tc)
    lane = jnp.arange(LANES)
    bd = (lane[:, None] // DA_QK_DIM == lane[None, :] // DA_QK_DIM).astype(BF16)

    xa = None
    pending = None
    for i in range(DEPTH):
        mod3 = mod[i].reshape(n_rows, 1, 6 * D)
        j = i // 2
        g_mix = norm_mix[i].reshape(1, D)
        if i % 2 == 0:
            assert i == 0
            lam_init = 0.8 - 0.6 * math.exp(-0.3 * i)
            q, k, v = _attn_project(
                x, ctx, g_mix, mod3, attn_w_qkv[j].astype(BF16),
                jnp.tile(attn_q_norm[j], LANES // DA_QK_DIM).reshape(1, LANES),
                jnp.tile(attn_k_norm[j], LANES // DA_QK_DIM).reshape(1, LANES),
                cos_t, sin_t, bd, n_lat, ctx_row)
            o = _attention(attn_lambda[j], q, k, v, attn_sub_norm[j].reshape(1, DA_V_DIM),
                           T, lam_init)
            w_o = attn_w_o[j]
        else:
            xa, *parts = _hgrn_project(xa, *pending, g_mix, mod3, hgrn_w_in[j].astype(BF16),
                                       lb_all[:, i], n_lat, ctx_row)
            o = _hgrn_scan(*parts, hgrn_out_norm[j].reshape(1, HG_DIM), T)
            w_o = hgrn_w_o[j]
        stream = (x, ctx, 0) if xa is None else (xa, xa, n_lat)
        xa, f_ext, bucket = _out_router(
            o, *stream, w_o.astype(BF16), mod3, norm_ffn[i].reshape(1, D), rwt, rb, n_lat, ctx_row)
        tables = _routing_tables(bucket.reshape(n_tokens), n_tokens)
        y = _expert_ffn(tables, f_ext.reshape(n_tokens, D + LANES), i, wg_all, wu_all, wd_all)
        pending = (y, mod3)
    assert DEPTH % 2 == 0
    y, mod3 = pending
    return _moe_combine(xa, mod3, y, n_lat, ctx_row, n_lat)
```

```python
import functools
import math

import jax
import jax.numpy as jnp
from jax import lax
from jax.experimental import pallas as pl
from jax.experimental.pallas import tpu as pltpu

F32 = jnp.float32
BF16 = jnp.bfloat16
HIGHEST = lax.Precision.HIGHEST

D_MODEL = 1024
DEPTH = 2
GRID_W = 64
DA_HEADS = 8
DA_QK_DIM = 64
DA_V_DIM = 128
ROPE_BASE = 10000.0
HG_HEADS = 8
HG_DIM = 128
N_EXPERTS = 16
N_GROUPS = 4
EXPERTS_PER_GROUP = 4
D_FF = 512
EPS = 1e-6

LANES = 128
TOKEN_BLOCK = 256
ATTN_Q_BLOCK = 256
ATTN_GROUP = 8
MOE_TILE = 256
MOE_DMA_UNROLL = 32
PAIRS = [(i, j) for i in range(EXPERTS_PER_GROUP) for j in range(i + 1, EXPERTS_PER_GROUP)]
N_BUCKETS = N_GROUPS * len(PAIRS)
HGRN_PROJ_PIECE = 256
SCAN_CHUNK = 64
SCAN_SUB = SCAN_CHUNK // 2
SCAN_GROUP = 4
SCAN_TRIP = 3
EXP_CLAMP = 80.0
VMEM_LIMIT = 56 * 1024 * 1024


def _cparams(sem):
    return pltpu.CompilerParams(dimension_semantics=sem, vmem_limit_bytes=VMEM_LIMIT)


def _silu(x):
    return x * jax.nn.sigmoid(x)


def _norm_mod(x, g, sc, sh):
    y = x * lax.rsqrt(jnp.mean(x * x, axis=-1, keepdims=True) + EPS)
    return (y * g) * (1.0 + sc) + sh


def _mod_kernel(c_ref, w_ref, b_ref, o_ref):
    o_ref[0] = jnp.dot(_silu(c_ref[...]), w_ref[0], preferred_element_type=F32,
                       precision=HIGHEST) + b_ref[0]


def _modulation(cvec, ada_w, ada_b):
    R, D = cvec.shape
    depth, _, n6 = ada_w.shape
    tn = 1024
    return pl.pallas_call(
        _mod_kernel,
        grid=(depth, n6 // tn),
        in_specs=[
            pl.BlockSpec((R, D), lambda i, j: (0, 0)),
            pl.BlockSpec((1, D, tn), lambda i, j: (i, 0, j)),
            pl.BlockSpec((1, 1, tn), lambda i, j: (i, 0, j)),
        ],
        out_specs=pl.BlockSpec((1, R, tn), lambda i, j: (i, 0, j)),
        out_shape=jax.ShapeDtypeStruct((depth, R, n6), F32),
        compiler_params=_cparams(("arbitrary", "arbitrary")),
        name="adaln_mod",
    )(cvec, ada_w, ada_b.reshape(depth, 1, n6))


def _mod_spec(col, n_lat, ctx_row):
    return pl.BlockSpec((1, 1, D_MODEL), lambda b, t: (jnp.where(t < n_lat, b, ctx_row), 0, col))


def _stream_specs(n_lat, ctx_block0):
    blk = (1, TOKEN_BLOCK, D_MODEL)
    return (pl.BlockSpec(blk, lambda b, t: (b, jnp.minimum(t, n_lat - 1), 0)),
            pl.BlockSpec(blk, lambda b, t: (b, jnp.maximum(t - n_lat, 0) + ctx_block0, 0)))


def _stream_block(lat_ref, ctx_ref, n_lat):
    return jnp.where(pl.program_id(1) < n_lat, lat_ref[0], ctx_ref[0])


def _attn_proj_kernel(x_ref, c_ref, g_ref, sc_ref, sh_ref, w_ref, qg_ref, kg_ref, cos_ref, sin_ref,
                      bd_ref, q_ref, k_ref, v_ref, *, n_lat):
    D = D_MODEL
    h = _norm_mod(_stream_block(x_ref, c_ref, n_lat), g_ref[...], sc_ref[0], sh_ref[0])
    qkv = jnp.dot(h.astype(BF16), w_ref[...], preferred_element_type=F32)
    cos, sin, bd = cos_ref[...], sin_ref[...], bd_ref[...]
    lane = lax.broadcasted_iota(jnp.int32, cos.shape, 1)
    upper = (lane & (DA_QK_DIM // 2)) != 0

    def norm_rope(t, gain, scale):
        sq = t * t
        sq_hi = sq.astype(BF16)
        sq_lo = (sq - sq_hi.astype(F32)).astype(BF16)
        ss = (jnp.dot(sq_hi, bd, preferred_element_type=F32)
              + jnp.dot(sq_lo, bd, preferred_element_type=F32))
        tn = t * lax.rsqrt(ss * (1.0 / DA_QK_DIM) + EPS) * gain
        partner = jnp.where(upper, pltpu.roll(tn, DA_QK_DIM // 2, 1),
                            pltpu.roll(tn, LANES - DA_QK_DIM // 2, 1))
        return (tn * cos + partner * sin) * scale

    for j in range(D // LANES):
        sl = slice(j * LANES, (j + 1) * LANES)
        q_ref[0, :, sl] = norm_rope(qkv[:, j * LANES:(j + 1) * LANES], qg_ref[...],
                                    math.log2(math.e) / math.sqrt(DA_QK_DIM)).astype(BF16)
        k_ref[0, j] = norm_rope(qkv[:, D + j * LANES:D + (j + 1) * LANES], kg_ref[...],
                                1.0).T.astype(BF16)
    v_ref[0] = qkv[:, 2 * D:].astype(BF16)


def _attn_project(x, ctx, g, mod3, w_qkv, qg, kg, cos_t, sin_t, bd, n_lat, ctx_row):
    B, T, D = x.shape
    NT = T + ctx.shape[1]
    TB = TOKEN_BLOCK
    tok = pl.BlockSpec((1, TB, D), lambda b, t: (b, t, 0))
    const2 = lambda shape: pl.BlockSpec(shape, lambda b, t: (0, 0))
    out = jax.ShapeDtypeStruct((B, NT, D), BF16)
    return pl.pallas_call(
        functools.partial(_attn_proj_kernel, n_lat=n_lat),
        grid=(B, NT // TB),
        in_specs=[
            *_stream_specs(n_lat, 0), const2((1, D)),
            _mod_spec(1, n_lat, ctx_row), _mod_spec(0, n_lat, ctx_row),
            const2((D, 3 * D)), const2((1, LANES)), const2((1, LANES)),
            pl.BlockSpec((TB, LANES), lambda b, t: (t, 0)),
            pl.BlockSpec((TB, LANES), lambda b, t: (t, 0)),
            const2((LANES, LANES)),
        ],
        out_specs=[tok, pl.BlockSpec((1, DA_HEADS, LANES, TB), lambda b, t: (b, 0, 0, t)), tok],
        out_shape=[out, jax.ShapeDtypeStruct((B, DA_HEADS, LANES, NT), BF16), out],
        compiler_params=_cparams(("parallel", "arbitrary")),
        name="attn_qkv_proj",
    )(x, ctx, g, mod3, mod3, w_qkv, qg, kg, cos_t, sin_t, bd)


def _attn_kernel(lam_ref, q_ref, kt_ref, v_ref, sn_ref, o_ref, *, t_lat, lam_init):
    QB = ATTN_Q_BLOCK
    lp = lam_ref[...]
    lam = (jnp.exp(jnp.sum(lp[0:1] * lp[1:2], keepdims=True))
           - jnp.exp(jnp.sum(lp[2:3] * lp[3:4], keepdims=True)) + lam_init)

    def scores(q, kt):
        lane = lax.broadcasted_iota(jnp.int32, q.shape, 1)
        zero = jnp.zeros_like(q)
        return (jnp.dot(jnp.where(lane < DA_QK_DIM, q, zero), kt, preferred_element_type=F32),
                jnp.dot(jnp.where(lane >= DA_QK_DIM, q, zero), kt, preferred_element_type=F32))

    def finish(s, v):
        def softmax_parts(sm):
            e = jnp.exp2(sm - jnp.max(sm, axis=-1, keepdims=True))
            return e, jnp.sum(e, axis=-1, keepdims=True)

        e0, l0 = softmax_parts(s[0])
        e1, l1 = softmax_parts(s[1])
        a = e0 - (lam * l0 / l1) * e1
        o = jnp.dot(a.astype(BF16), v, preferred_element_type=F32) * (1.0 / l0)
        o = o * lax.rsqrt(jnp.mean(o * o, axis=-1, keepdims=True) + EPS) * sn_ref[...]
        return (o * (1.0 - lam_init)).astype(BF16)

    G = math.gcd(ATTN_GROUP, t_lat // QB)

    def latent_group(i, carry):
        rows = [pl.ds(pl.multiple_of((G * i + j) * QB, QB), QB) for j in range(G)]
        s_next = scores(q_ref[0, rows[0], :], kt_ref[0, 0])
        for j in range(G):
            s_cur = s_next
            if j + 1 < G:
                s_next = scores(q_ref[0, rows[j + 1], :], kt_ref[0, 0])
            o_ref[0, rows[j], :] = finish(s_cur, v_ref[0])
        return carry

    lax.fori_loop(0, t_lat // (G * QB), latent_group, 0)
    o_ref[0, t_lat:, :] = finish(scores(q_ref[0, t_lat:, :], kt_ref[0, 0, :, t_lat:]),
                                 v_ref[0, t_lat:, :])


def _attention(lam_p, q, kt, v, sub_norm, t_lat, lam_init):
    B, NT, D = q.shape
    blk = pl.BlockSpec((1, NT, LANES), lambda b, h: (b, 0, h))
    return pl.pallas_call(
        functools.partial(_attn_kernel, t_lat=t_lat, lam_init=lam_init),
        grid=(B, DA_HEADS),
        in_specs=[
            pl.BlockSpec(lam_p.shape, lambda b, h: (0, 0)),
            blk, pl.BlockSpec((1, 1, LANES, NT), lambda b, h: (b, h, 0, 0)), blk,
            pl.BlockSpec((1, LANES), lambda b, h: (0, 0)),
        ],
        out_specs=blk,
        out_shape=jax.ShapeDtypeStruct((B, NT, D), BF16),
        compiler_params=_cparams(("parallel", "arbitrary")),
        name="diff_attention",
    )(lam_p, q, kt, v, sub_norm)


def _out_router_kernel(o_ref, x_ref, c_ref, w_ref, gm_ref, g_ref, sc_ref, sh_ref, rw_ref, rb_ref,
                       xo_ref, f_ref, bk_ref, *, n_lat):
    D = D_MODEL
    out = jnp.dot(o_ref[0], w_ref[...], preferred_element_type=F32)
    x = _stream_block(x_ref, c_ref, n_lat) + gm_ref[0] * out
    xo_ref[0] = x
    f = _norm_mod(x, g_ref[...], sc_ref[0], sh_ref[0])
    f_ref[0, :, :D] = f
    def split3(a):
        a1 = a.astype(BF16)
        r = a - a1.astype(F32)
        a2 = r.astype(BF16)
        return a1, a2, (r - a2.astype(F32)).astype(BF16)

    f1, f2, f3 = split3(f)
    w1, w2, w3 = split3(rw_ref[...])
    mm = functools.partial(jnp.dot, preferred_element_type=F32)
    logits_t = ((mm(f3, w1) + mm(f2, w2) + mm(f1, w3)) + (mm(f2, w1) + mm(f1, w2))) + mm(f1, w1)
    logits = logits_t.T[:N_EXPERTS]
    aff = jax.nn.sigmoid(logits)
    biased = aff + rb_ref[...]
    G, E = EXPERTS_PER_GROUP, N_EXPERTS
    row = lax.broadcasted_iota(jnp.int32, biased.shape, 0)
    member = lax.rem(row, G)
    group = row // G

    def shifted(x, k):
        return jnp.where(member + k < G, pltpu.roll(x, E - k, 0), pltpu.roll(x, G - k, 0))

    rank = jnp.zeros_like(row)
    for k in range(1, G):
        other = shifted(biased, k)
        ahead = (other > biased) | ((other == biased) & (member + k >= G))
        rank = rank + ahead.astype(jnp.int32)
    top2 = rank < 2
    t = jnp.where(top2, biased, 0.0)
    gscore = t
    for k in range(1, G):
        gscore = gscore + shifted(t, k)
    best = jnp.ones_like(top2)
    for m in range(1, N_GROUPS):
        other = pltpu.roll(gscore, G * m, 0)
        best = best & ((gscore > other) | ((gscore == other) & (group < m)))
    chosen = top2 & best
    lowest = jnp.min(jnp.where(chosen, row, E), axis=0, keepdims=True)
    is_lo = chosen & (row == lowest)
    is_hi = chosen & (row != lowest)
    a_lo = jnp.sum(jnp.where(is_lo, aff, 0.0), axis=0, keepdims=True)
    a_hi = jnp.sum(jnp.where(is_hi, aff, 0.0), axis=0, keepdims=True)
    m_lo = lax.rem(lowest, G)
    m_hi = jnp.sum(jnp.where(is_hi, member, 0), axis=0, keepdims=True)
    pair = m_lo * (2 * G - 1 - m_lo) // 2 + (m_hi - m_lo - 1)
    tot = a_lo + a_hi
    bk_ref[0] = (lowest // G) * len(PAIRS) + pair

    tb = x.shape[0]
    eye = (lax.broadcasted_iota(jnp.int32, (tb, tb), 0)
           == lax.broadcasted_iota(jnp.int32, (tb, tb), 1))

    def column(row):
        return jnp.sum(jnp.where(eye, row, 0.0), axis=1, keepdims=True)

    lane = lax.broadcasted_iota(jnp.int32, (tb, LANES), 1)
    f_ref[0, :, D:] = jnp.where(lane < LANES // 2, column(a_lo / tot), column(a_hi / tot))


def _out_router(o, x, ctx, ctx_block0, w_o, mod3, g_ffn, rwt, rb, n_lat, ctx_row):
    B, NT, D = o.shape
    TB = TOKEN_BLOCK
    nb = NT // TB
    tok = pl.BlockSpec((1, TB, D), lambda b, t: (b, t, 0))
    row = pl.BlockSpec((1, 1, TB), lambda b, t: (b * nb + t, 0, 0))
    const2 = lambda shape: pl.BlockSpec(shape, lambda b, t: (0, 0))
    rows = lambda dt: jax.ShapeDtypeStruct((B * nb, 1, TB), dt)
    return pl.pallas_call(
        functools.partial(_out_router_kernel, n_lat=n_lat),
        grid=(B, nb),
        in_specs=[
            tok, *_stream_specs(n_lat, ctx_block0), const2((D, D)),
            _mod_spec(2, n_lat, ctx_row), const2((1, D)),
            _mod_spec(4, n_lat, ctx_row), _mod_spec(3, n_lat, ctx_row),
            const2((D, LANES)), const2((N_EXPERTS, 1)),
        ],
        out_specs=[tok, pl.BlockSpec((1, TB, D + LANES), lambda b, t: (b, t, 0)), row],
        out_shape=[jax.ShapeDtypeStruct((B, NT, D), F32),
                   jax.ShapeDtypeStruct((B, NT, D + LANES), F32), rows(jnp.int32)],
        compiler_params=_cparams(("parallel", "arbitrary")),
        name="out_proj_router",
    )(o, x, ctx, w_o, mod3, g_ffn, mod3, mod3, rwt, rb)


def _routing_tables(bucket, n_tokens):
    tm = MOE_TILE
    max_tiles = n_tokens // tm + N_BUCKETS
    onehot = (bucket[:, None] == jnp.arange(N_BUCKETS, dtype=jnp.int32)[None, :]).astype(jnp.int32)
    count = jnp.sum(onehot, axis=0)
    rank = jnp.sum(jnp.cumsum(onehot, axis=0) * onehot, axis=1) - 1
    btiles = (count + tm - 1) // tm
    tile_end = jnp.cumsum(btiles)
    tile_start = tile_end - btiles
    dest = tile_start[bucket] * tm + rank
    tile = jnp.arange(max_tiles, dtype=jnp.int32)
    tile_bucket = jnp.minimum(jnp.sum((tile[:, None] >= tile_end[None, :]).astype(jnp.int32), axis=1),
                              N_BUCKETS - 1)
    in_bucket = tile - tile_start[tile_bucket]
    n_valid = jnp.clip(count[tile_bucket] - in_bucket * tm, 0, tm)
    n_valid = jnp.where(tile < tile_end[-1], n_valid, 0).astype(jnp.int32)
    grp = tile_bucket // len(PAIRS)
    pair = tile_bucket % len(PAIRS)
    lo_tab = jnp.array([p[0] for p in PAIRS], jnp.int32)
    hi_tab = jnp.array([p[1] for p in PAIRS], jnp.int32)
    tile_lo = grp * EXPERTS_PER_GROUP + lo_tab[pair]
    tile_hi = grp * EXPERTS_PER_GROUP + hi_tab[pair]
    return dest.astype(jnp.int32), n_valid, tile_lo, tile_hi


def _expert_kernel(dest_ref, nv_ref, tlo_ref, thi_ref, f_hbm, wg_lo_ref, wu_lo_ref, wd_lo_ref,
                   wg_hi_ref, wu_hi_ref, wd_hi_ref, y_hbm, xbuf, ybuf, tok_ref, gsem, ssem):
    TM, D = MOE_TILE, D_MODEL
    i = pl.program_id(0)
    n = pl.num_programs(0)
    slot = lax.rem(i, 2)
    other = 1 - slot

    @pl.when(i == 0)
    def _():
        U = MOE_DMA_UNROLL

        def invert(g, carry):
            for u in range(U):
                t = g * U + u
                tok_ref[dest_ref[t]] = t
            return carry

        lax.fori_loop(0, dest_ref.shape[0] // U, invert, 0)

    def gather_copy(tile, r, s):
        return pltpu.make_async_copy(f_hbm.at[pl.ds(tok_ref[tile * TM + r], 1), :],
                                     xbuf.at[s, pl.ds(r, 1), :], gsem.at[s])

    def scatter_copy(tile, r, s):
        return pltpu.make_async_copy(ybuf.at[s, pl.ds(r, 1), :],
                                     y_hbm.at[pl.ds(tok_ref[tile * TM + r], 1), :], ssem.at[s])

    def for_rows(n_rows, fn):
        U = MOE_DMA_UNROLL
        for g in range(TM // U):
            @pl.when((g + 1) * U <= n_rows)
            def _():
                for u in range(U):
                    fn(g * U + u)

        def single(r, carry):
            fn(r)
            return carry

        lax.fori_loop((n_rows // U) * U, n_rows, single, 0)

    def wait_rows(n_rows, whole_tile_copy, row_copy):
        @pl.when(n_rows == TM)
        def _():
            whole_tile_copy.wait()

        @pl.when(n_rows < TM)
        def _():
            for_rows(n_rows, lambda r: row_copy(r).wait())

    def start_gather(tile, s):
        for_rows(nv_ref[tile], lambda r: gather_copy(tile, r, s).start())

    def wait_gather(tile, s):
        wait_rows(nv_ref[tile],
                  pltpu.make_async_copy(f_hbm.at[pl.ds(0, TM), :], xbuf.at[s], gsem.at[s]),
                  lambda r: gather_copy(tile, r, s))

    def start_scatter(tile, s):
        for_rows(nv_ref[tile], lambda r: scatter_copy(tile, r, s).start())

    def wait_scatter(tile, s):
        wait_rows(nv_ref[tile],
                  pltpu.make_async_copy(ybuf.at[s], y_hbm.at[pl.ds(0, TM), :], ssem.at[s]),
                  lambda r: scatter_copy(tile, r, s))

    @pl.when(i == 0)
    def _():
        xbuf[...] = jnp.zeros_like(xbuf)
        start_gather(0, 0)

    wait_gather(i, slot)

    @pl.when(i >= 2)
    def _():
        wait_scatter(i - 2, slot)

    @pl.when(i + 1 < n)
    def _():
        start_gather(i + 1, other)

    @pl.when(nv_ref[i] > 0)
    def _():
        xe = xbuf[slot]
        x = xe[:, :D].astype(BF16)

        def ffn(wg_ref, wu_ref, wd_ref):
            hid = (_silu(jnp.dot(x, wg_ref[0, 0], preferred_element_type=F32))
                   * jnp.dot(x, wu_ref[0, 0], preferred_element_type=F32))
            return jnp.dot(hid.astype(BF16), wd_ref[0, 0], preferred_element_type=F32)

        ybuf[slot] = (xe[:, D:D + 1] * ffn(wg_lo_ref, wu_lo_ref, wd_lo_ref)
                      + xe[:, D + LANES // 2:D + LANES // 2 + 1] * ffn(wg_hi_ref, wu_hi_ref, wd_hi_ref))
        start_scatter(i, slot)

    @pl.when(i == n - 1)
    def _():
        wait_scatter(i - 1, other)
        wait_scatter(i, slot)


def _expert_ffn(tables, f_ext, layer, w_gate, w_up, w_down):
    dest, n_valid, tile_lo, tile_hi = tables
    N = f_ext.shape[0]
    assert N % MOE_DMA_UNROLL == 0
    D = D_MODEL
    tm = MOE_TILE
    lo = lambda i, dst, nv, tlo, thi: (layer, tlo[i], 0, 0)
    hi = lambda i, dst, nv, tlo, thi: (layer, thi[i], 0, 0)
    w_in, w_out = (1, 1, D, D_FF), (1, 1, D_FF, D)
    return pl.pallas_call(
        _expert_kernel,
        grid_spec=pltpu.PrefetchScalarGridSpec(
            num_scalar_prefetch=4,
            grid=(n_valid.shape[0],),
            in_specs=[
                pl.BlockSpec(memory_space=pl.ANY),
                pl.BlockSpec(w_in, lo), pl.BlockSpec(w_in, lo), pl.BlockSpec(w_out, lo),
                pl.BlockSpec(w_in, hi), pl.BlockSpec(w_in, hi), pl.BlockSpec(w_out, hi),
            ],
            out_specs=pl.BlockSpec(memory_space=pl.ANY),
            scratch_shapes=[pltpu.VMEM((2, tm, D + LANES), F32), pltpu.VMEM((2, tm, D), F32),
                            pltpu.SMEM((n_valid.shape[0] * tm,), jnp.int32),
                            pltpu.SemaphoreType.DMA((2,)), pltpu.SemaphoreType.DMA((2,))],
        ),
        out_shape=jax.ShapeDtypeStruct((N, D), F32),
        compiler_params=_cparams(("arbitrary",)),
        name="moe_expert_ffn",
    )(dest, n_valid, tile_lo, tile_hi, f_ext, w_gate, w_up, w_down, w_gate, w_up, w_down)


def _combine_kernel(x_ref, gf_ref, y_ref, o_ref):
    o_ref[0] = x_ref[0] + gf_ref[0] * y_ref[...]


def _moe_combine(xa, mod3, y, n_lat, ctx_row, n_blocks_out):
    B, NT, D = xa.shape
    TB = TOKEN_BLOCK
    nb = NT // TB
    tok = pl.BlockSpec((1, TB, D), lambda b, t: (b, t, 0))
    return pl.pallas_call(
        _combine_kernel,
        grid=(B, n_blocks_out),
        in_specs=[tok, _mod_spec(5, n_lat, ctx_row),
                  pl.BlockSpec((TB, D), lambda b, t: (b * nb + t, 0))],
        out_specs=tok,
        out_shape=jax.ShapeDtypeStruct((B, n_blocks_out * TB, D), F32),
        compiler_params=_cparams(("parallel", "arbitrary")),
        name="moe_combine",
    )(xa, mod3, y)


def _hgrn_proj_kernel(x_ref, y_ref, gf_ref, g_ref, sc_ref, sh_ref, w_ref, lb_ref,
                      xo_ref, q_ref, v_ref, kf_ref, lff_ref, kb_ref, lfb_ref, gate_ref):
    HK = HG_HEADS * HG_DIM
    x = x_ref[0] + gf_ref[0] * y_ref[...]
    xo_ref[0] = x
    h = _norm_mod(x, g_ref[...], sc_ref[0], sh_ref[0]).astype(BF16)

    W = HGRN_PROJ_PIECE

    def proj(part, c):
        return jnp.dot(h, w_ref[:, part * HK + c:part * HK + c + W], preferred_element_type=F32)

    def forget(z, lbd, k_ref, lf_ref, cols):
        e = jnp.exp(-jnp.abs(z))
        t = 1.0 + e
        k_ref[0, :, cols] = (1.0 - lbd) * (jnp.where(z >= 0.0, e, 1.0) / t)
        a = jnp.log(lbd)
        b = jnp.log1p(-lbd) + (jnp.minimum(z, 0.0) - jnp.log(t))
        lf_ref[0, :, cols] = jnp.maximum(a, b) + jnp.log(1.0 + jnp.exp(-jnp.abs(a - b)))

    def finish(part, c, y):
        cols = slice(c, c + W)
        if part == 0:
            q_ref[0, :, cols] = _silu(y)
        elif part == 1:
            v_ref[0, :, cols] = y.astype(BF16)
        elif part == 2:
            forget(y, lb_ref[0:1, cols], kf_ref, lff_ref, cols)
        elif part == 3:
            forget(y, lb_ref[1:2, cols], kb_ref, lfb_ref, cols)
        else:
            gate_ref[0, :, cols] = y

    pieces = [(part, c) for c in range(0, HK, W) for part in (2, 0, 3, 1, 4)]
    nxt = proj(*pieces[0])
    for k, piece in enumerate(pieces):
        cur = nxt
        if k + 1 < len(pieces):
            nxt = proj(*pieces[k + 1])
        finish(*piece, cur)


def _hgrn_project(xa, y, mod_prev, g, mod3, w_in, lb, n_lat, ctx_row):
    B, NT, D = xa.shape
    TB = TOKEN_BLOCK
    nb = NT // TB
    HK = HG_HEADS * HG_DIM
    tok = pl.BlockSpec((1, TB, D), lambda b, t: (b, t, 0))
    tok_o = pl.BlockSpec((1, TB, HK), lambda b, t: (b, t, 0))
    const2 = lambda shape: pl.BlockSpec(shape, lambda b, t: (0, 0))
    out = jax.ShapeDtypeStruct((B, NT, HK), F32)
    return pl.pallas_call(
        _hgrn_proj_kernel,
        grid=(B, nb),
        in_specs=[
            tok, pl.BlockSpec((TB, D), lambda b, t: (b * nb + t, 0)), _mod_spec(5, n_lat, ctx_row),
            const2((1, D)), _mod_spec(1, n_lat, ctx_row), _mod_spec(0, n_lat, ctx_row),
            pl.BlockSpec((D, 5 * HK), lambda b, t: (0, 0), pipeline_mode=pl.Buffered(1)),
            const2((2, HK)),
        ],
        out_specs=[tok] + [tok_o] * 7,
        out_shape=[jax.ShapeDtypeStruct((B, NT, D), F32), out,
                   jax.ShapeDtypeStruct((B, NT, HK), BF16)] + [out] * 5,
        compiler_params=_cparams(("parallel", "arbitrary")),
        name="hgrn_proj",
    )(xa, y, mod_prev, g, mod3, mod3, w_in, lb)


def _scan_chunk_scores(q, k, vb, a, forward):
    C, SB = SCAN_CHUNK, SCAN_SUB
    row = lax.broadcasted_iota(jnp.int32, a.shape, 0)
    first = row < SB
    mid_row = SB // 2
    m = jnp.where(first, a[mid_row:mid_row + 1], a[SB + mid_row:SB + mid_row + 1])
    qd = q * jnp.exp(jnp.minimum(a - m, EXP_CLAMP))
    kd = k * jnp.exp(jnp.minimum(m - a, EXP_CLAMP))
    edge = a[SB - 1:SB] if forward else a[SB:SB + 1]
    e_x = jnp.exp(-jnp.abs(a - edge))
    nt = (((1,), (1,)), ((), ()))
    s_d = lax.dot_general(qd.astype(BF16), kd.astype(BF16), nt, preferred_element_type=F32)
    s_x = lax.dot_general((q * e_x).astype(BF16), (k * e_x).astype(BF16), nt,
                          preferred_element_type=F32)
    a_out = a[C - 1:C] if forward else a[0:1]
    k_out = k * jnp.exp(a_out - a)
    upd = lax.dot_general(vb, k_out.astype(BF16), (((0,), (0,)), ((), ())),
                          preferred_element_type=F32)
    return s_d, s_x, q * jnp.exp(a), upd, jnp.exp(a_out)


def _scan_chunk_intra(s_d, s_x, vb, forward):
    SB = SCAN_SUB
    t_i = lax.broadcasted_iota(jnp.int32, s_d.shape, 0)
    s_i = lax.broadcasted_iota(jnp.int32, s_d.shape, 1)
    same = (t_i < SB) == (s_i < SB)
    if forward:
        causal, cross = s_i <= t_i, (t_i >= SB) & (s_i < SB)
    else:
        causal, cross = s_i >= t_i, (t_i < SB) & (s_i >= SB)
    scores = jnp.where(same & causal, s_d, jnp.where(cross, s_x, 0.0))
    return jnp.dot(scores.astype(BF16), vb, preferred_element_type=F32)


def _hgrn_scan_kernel(q_ref, v_ref, kf_ref, lff_ref, kb_ref, lfb_ref, gate_ref, gain_ref,
                      o_ref, acc_ref, qe_ref, upd_ref, dec_ref, inter_ref, *, n_chunks, n_lat_chunks):
    C, G = SCAN_CHUNK, SCAN_GROUP
    R = C * G
    pos = lax.rem(lax.broadcasted_iota(jnp.int32, (R, HG_DIM), 0), C)

    def cumsum(lf, forward):
        x = lf
        step = 1
        while step < C:
            if forward:
                x = x + jnp.where(pos >= step, pltpu.roll(x, step, 0), 0.0)
            else:
                x = x + jnp.where(pos < C - step, pltpu.roll(x, R - step, 0), 0.0)
            step *= 2
        return x

    def group_scores(g):
        rows = pl.ds(pl.multiple_of(g * R, R), R)
        q = q_ref[0, rows, :]
        vb = v_ref[0, rows, :].astype(BF16)
        kf, kb = kf_ref[0, rows, :], kb_ref[0, rows, :]
        a_f = cumsum(lff_ref[0, rows, :], True)
        a_b = cumsum(lfb_ref[0, rows, :], False)
        res = []
        for ci in range(G):
            sl = slice(ci * C, (ci + 1) * C)
            res.append((vb[sl], _scan_chunk_scores(q[sl], kf[sl], vb[sl], a_f[sl], True),
                        _scan_chunk_scores(q[sl], kb[sl], vb[sl], a_b[sl], False)))
        return res

    def group_finish(g, res):
        o_sum = [_scan_chunk_intra(fw[0], fw[1], vb, True) + _scan_chunk_intra(bw[0], bw[1], vb, False)
                 for vb, fw, bw in res]
        for d in range(2):
            p = l = None
            for ci in (range(G) if d == 0 else reversed(range(G))):
                _, _, qe, upd, dec = res[ci][1 + d]
                crow = pl.ds(pl.multiple_of((g * G + ci) * C, C), C)
                if p is None:
                    qe_ref[d, crow, :] = qe.astype(BF16)
                    p, l = dec, upd
                else:
                    qe_ref[d, crow, :] = (qe * p).astype(BF16)
                    o_sum[ci] = o_sum[ci] + lax.dot_general(qe.astype(BF16), l.astype(BF16), nt,
                                                            preferred_element_type=F32)
                    p, l = p * dec, l * dec + upd
            upd_ref[d, g] = l
            dec_ref[d, pl.ds(g, 1), :] = p
        for ci in range(G):
            acc_ref[pl.ds(pl.multiple_of((g * G + ci) * C, C), C), :] = o_sum[ci]

    nt = (((1,), (1,)), ((), ()))
    n_groups = n_chunks // G
    n_trip = math.gcd(SCAN_TRIP, n_groups)

    def local(i, carry):
        nxt = group_scores(i * n_trip)
        for j in range(n_trip):
            cur = nxt
            if j + 1 < n_trip:
                nxt = group_scores(i * n_trip + j + 1)
            group_finish(i * n_trip + j, cur)
        return carry

    lax.fori_loop(0, n_groups // n_trip, local, 0)

    def carry_state(j, carry):
        st_f, st_b = carry
        gf = lax.rem(j + n_lat_chunks // G, n_groups)
        gb = n_groups - 1 - j
        rf = pl.ds(pl.multiple_of(gf * R, R), R)
        rb = pl.ds(pl.multiple_of(gb * R, R), R)
        inter_ref[0, rf, :] = lax.dot_general(qe_ref[0, rf, :], st_f.astype(BF16), nt,
                                              preferred_element_type=F32)
        inter_ref[1, rb, :] = lax.dot_general(qe_ref[1, rb, :], st_b.astype(BF16), nt,
                                              preferred_element_type=F32)
        st_f = st_f * dec_ref[0, pl.ds(gf, 1), :] + upd_ref[0, gf]
        st_b = st_b * dec_ref[1, pl.ds(gb, 1), :] + upd_ref[1, gb]
        return st_f, st_b

    zero = jnp.zeros((HG_DIM, HG_DIM), F32)
    lax.fori_loop(0, n_groups, carry_state, (zero, zero))
    o = acc_ref[...] + inter_ref[0] + inter_ref[1]
    o = o * lax.rsqrt(jnp.mean(o * o, axis=-1, keepdims=True) + EPS) * gain_ref[...]
    o_ref[0] = (o * _silu(gate_ref[0])).astype(BF16)


def _hgrn_scan(q, v, kf, lff, kb, lfb, gate, gain, t_lat):
    B, NT, HK = q.shape
    blk = pl.BlockSpec((1, NT, HG_DIM), lambda b, h: (b, 0, h))
    return pl.pallas_call(
        functools.partial(_hgrn_scan_kernel, n_chunks=NT // SCAN_CHUNK,
                          n_lat_chunks=t_lat // SCAN_CHUNK),
        grid=(B, HG_HEADS),
        in_specs=[blk] * 7 + [pl.BlockSpec((1, HG_DIM), lambda b, h: (0, 0))],
        out_specs=blk,
        out_shape=jax.ShapeDtypeStruct((B, NT, HK), BF16),
        scratch_shapes=[pltpu.VMEM((NT, HG_DIM), F32),
                        pltpu.VMEM((2, NT, HG_DIM), BF16),
                        pltpu.VMEM((2, NT // (SCAN_CHUNK * SCAN_GROUP), HG_DIM, HG_DIM), F32),
                        pltpu.VMEM((2, NT // (SCAN_CHUNK * SCAN_GROUP), HG_DIM), F32),
                        pltpu.VMEM((2, NT, HG_DIM), F32)],
        compiler_params=_cparams(("parallel", "arbitrary")),
        name="hgrn_scan",
    )(q, v, kf, lff, kb, lfb, gate, gain)


def _rope_tables(t_lat, t_ctx):
    rows = t_lat // GRID_W
    r = jnp.repeat(jnp.arange(rows, dtype=F32), GRID_W)
    col = jnp.tile(jnp.arange(GRID_W, dtype=F32), rows)
    n_pairs = DA_QK_DIM // 4
    inv = ROPE_BASE ** (-jnp.arange(n_pairs, dtype=F32) / n_pairs)
    ang = jnp.concatenate([r[:, None] * inv, col[:, None] * inv], axis=-1)
    cos, sin = jnp.cos(ang), jnp.sin(ang)
    cos_l = jnp.tile(cos, (1, LANES // cos.shape[1]))
    sin_l = jnp.tile(jnp.concatenate([-sin, sin], axis=-1), (1, LANES // (2 * sin.shape[1])))
    cos_l = jnp.concatenate([cos_l, jnp.ones((t_ctx, LANES), F32)], axis=0)
    sin_l = jnp.concatenate([sin_l, jnp.zeros((t_ctx, LANES), F32)], axis=0)
    return cos_l, sin_l


def kernel(x, c, ctx, c_ctx, ada_w, ada_b, norm_mix, norm_ffn, attn_w_qkv, attn_w_o, attn_q_norm,
           attn_k_norm, attn_sub_norm, attn_lambda, hgrn_w_in, hgrn_w_o, hgrn_out_norm,
           hgrn_lb_gamma, router_w, router_bias, moe_w_gate, moe_w_up, moe_w_down):
    B, T, D = x.shape
    Tc = ctx.shape[1]
    TB = TOKEN_BLOCK
    assert D == D_MODEL and T % TB == 0 and Tc % TB == 0 and T % GRID_W == 0
    assert ada_w.shape[0] == DEPTH == 2
    assert T % (SCAN_CHUNK * SCAN_GROUP) == 0 and Tc % (SCAN_CHUNK * SCAN_GROUP) == 0
    assert T % ATTN_Q_BLOCK == 0
    NT = T + Tc
    n_lat = T // TB
    n_tokens = B * NT
    ctx_row = B

    n_rows = -(-(B + 1) // 8) * 8
    cvec = jnp.concatenate([c, c_ctx[None, :], jnp.zeros((n_rows - B - 1, D), F32)], axis=0)
    mod = _modulation(cvec, ada_w, ada_b)

    wg_all, wu_all, wd_all = (w.astype(BF16) for w in (moe_w_gate, moe_w_up, moe_w_down))
    rwt = jnp.pad(router_w.astype(F32), ((0, 0), (0, LANES - N_EXPERTS)))
    rb = router_bias.reshape(N_EXPERTS, 1)

    p = jax.nn.softmax(hgrn_lb_gamma.astype(F32), axis=1)
    cum = jnp.cumsum(p, axis=1)
    lb_all = cum - cum[:, :1]

    cos_t, sin_t = _rope_tables(T, Tc)
    lane = jnp.arange(LANES)
    bd = (lane[:, None] // DA_QK_DIM == lane[None, :] // DA_QK_DIM).astype(BF16)

    xa = None
    pending = None
    for i in range(DEPTH):
        mod3 = mod[i].reshape(n_rows, 1, 6 * D)
        j = i // 2
        g_mix = norm_mix[i].reshape(1, D)
        if i % 2 == 0:
            assert i == 0
            lam_init = 0.8 - 0.6 * math.exp(-0.3 * i)
            q, k, v = _attn_project(
                x, ctx, g_mix, mod3, attn_w_qkv[j].astype(BF16),
                jnp.tile(attn_q_norm[j], LANES // DA_QK_DIM).reshape(1, LANES),
                jnp.tile(attn_k_norm[j], LANES // DA_QK_DIM).reshape(1, LANES),
                cos_t, sin_t, bd, n_lat, ctx_row)
            o = _attention(attn_lambda[j], q, k, v, attn_sub_norm[j].reshape(1, DA_V_DIM),
                           T, lam_init)
            w_o = attn_w_o[j]
        else:
            xa, *parts = _hgrn_project(xa, *pending, g_mix, mod3, hgrn_w_in[j].astype(BF16),
                                       lb_all[:, i], n_lat, ctx_row)
            o = _hgrn_scan(*parts, hgrn_out_norm[j].reshape(1, HG_DIM), T)
            w_o = hgrn_w_o[j]
        stream = (x, ctx, 0) if xa is None else (xa, xa, n_lat)
        xa, f_ext, bucket = _out_router(
            o, *stream, w_o.astype(BF16), mod3, norm_ffn[i].reshape(1, D), rwt, rb, n_lat, ctx_row)
        tables = _routing_tables(bucket.reshape(n_tokens), n_tokens)
        y = _expert_ffn(tables, f_ext.reshape(n_tokens, D + LANES), i, wg_all, wu_all, wd_all)
        pending = (y, mod3)
    assert DEPTH % 2 == 0
    y, mod3 = pending
    return _moe_combine(xa, mod3, y, n_lat, ctx_row, n_lat)
```

```python
import functools
import math

import jax
import jax.numpy as jnp
from jax import lax
from jax.experimental import pallas as pl
from jax.experimental.pallas import tpu as pltpu

F32 = jnp.float32
BF16 = jnp.bfloat16
HIGHEST = lax.Precision.HIGHEST

D_MODEL = 1024
DEPTH = 2
GRID_W = 64
DA_HEADS = 8
DA_QK_DIM = 64
DA_V_DIM = 128
ROPE_BASE = 10000.0
HG_HEADS = 8
HG_DIM = 128
N_EXPERTS = 16
N_GROUPS = 4
EXPERTS_PER_GROUP = 4
D_FF = 512
EPS = 1e-6

LANES = 128
TOKEN_BLOCK = 256
ATTN_Q_BLOCK = 256
ATTN_GROUP = 8
MOE_TILE = 256
MOE_DMA_UNROLL = 32
PAIRS = [(i, j) for i in range(EXPERTS_PER_GROUP) for j in range(i + 1, EXPERTS_PER_GROUP)]
N_BUCKETS = N_GROUPS * len(PAIRS)
HGRN_PROJ_PIECE = 256
SCAN_CHUNK = 64
SCAN_SUB = SCAN_CHUNK // 2
SCAN_GROUP = 4
SCAN_TRIP = 3
EXP_CLAMP = 80.0
VMEM_LIMIT = 56 * 1024 * 1024


def _cparams(sem):
    return pltpu.CompilerParams(dimension_semantics=sem, vmem_limit_bytes=VMEM_LIMIT)


def _silu(x):
    return x * jax.nn.sigmoid(x)


def _norm_mod(x, g, sc, sh):
    y = x * lax.rsqrt(jnp.mean(x * x, axis=-1, keepdims=True) + EPS)
    return (y * g) * (1.0 + sc) + sh


def _mod_kernel(c_ref, w_ref, b_ref, o_ref):
    o_ref[0] = jnp.dot(_silu(c_ref[...]), w_ref[0], preferred_element_type=F32,
                       precision=HIGHEST) + b_ref[0]


def _modulation(cvec, ada_w, ada_b):
    R, D = cvec.shape
    depth, _, n6 = ada_w.shape
    tn = 1024
    return pl.pallas_call(
        _mod_kernel,
        grid=(depth, n6 // tn),
        in_specs=[
            pl.BlockSpec((R, D), lambda i, j: (0, 0)),
            pl.BlockSpec((1, D, tn), lambda i, j: (i, 0, j)),
            pl.BlockSpec((1, 1, tn), lambda i, j: (i, 0, j)),
        ],
        out_specs=pl.BlockSpec((1, R, tn), lambda i, j: (i, 0, j)),
        out_shape=jax.ShapeDtypeStruct((depth, R, n6), F32),
        compiler_params=_cparams(("arbitrary", "arbitrary")),
        name="adaln_mod",
    )(cvec, ada_w, ada_b.reshape(depth, 1, n6))


def _mod_spec(col, n_lat, ctx_row):
    return pl.BlockSpec((1, 1, D_MODEL), lambda b, t: (jnp.where(t < n_lat, b, ctx_row), 0, col))


def _stream_specs(n_lat, ctx_block0):
    blk = (1, TOKEN_BLOCK, D_MODEL)
    return (pl.BlockSpec(blk, lambda b, t: (b, jnp.minimum(t, n_lat - 1), 0)),
            pl.BlockSpec(blk, lambda b, t: (b, jnp.maximum(t - n_lat, 0) + ctx_block0, 0)))


def _stream_block(lat_ref, ctx_ref, n_lat):
    return jnp.where(pl.program_id(1) < n_lat, lat_ref[0], ctx_ref[0])


def _attn_proj_kernel(x_ref, c_ref, g_ref, sc_ref, sh_ref, w_ref, qg_ref, kg_ref, cos_ref, sin_ref,
                      bd_ref, q_ref, k_ref, v_ref, *, n_lat):
    D = D_MODEL
    h = _norm_mod(_stream_block(x_ref, c_ref, n_lat), g_ref[...], sc_ref[0], sh_ref[0])
    qkv = jnp.dot(h.astype(BF16), w_ref[...], preferred_element_type=F32)
    cos, sin, bd = cos_ref[...], sin_ref[...], bd_ref[...]
    lane = lax.broadcasted_iota(jnp.int32, cos.shape, 1)
    upper = (lane & (DA_QK_DIM // 2)) != 0

    def norm_rope(t, gain, scale):
        sq = t * t
        sq_hi = sq.astype(BF16)
        sq_lo = (sq - sq_hi.astype(F32)).astype(BF16)
        ss = (jnp.dot(sq_hi, bd, preferred_element_type=F32)
              + jnp.dot(sq_lo, bd, preferred_element_type=F32))
        tn = t * lax.rsqrt(ss * (1.0 / DA_QK_DIM) + EPS) * gain
        partner = jnp.where(upper, pltpu.roll(tn, DA_QK_DIM // 2, 1),
                            pltpu.roll(tn, LANES - DA_QK_DIM // 2, 1))
        return (tn * cos + partner * sin) * scale

    for j in range(D // LANES):
        sl = slice(j * LANES, (j + 1) * LANES)
        q_ref[0, :, sl] = norm_rope(qkv[:, j * LANES:(j + 1) * LANES], qg_ref[...],
                                    math.log2(math.e) / math.sqrt(DA_QK_DIM)).astype(BF16)
        k_ref[0, j] = norm_rope(qkv[:, D + j * LANES:D + (j + 1) * LANES], kg_ref[...],
                                1.0).T.astype(BF16)
    v_ref[0] = qkv[:, 2 * D:].astype(BF16)


def _attn_project(x, ctx, g, mod3, w_qkv, qg, kg, cos_t, sin_t, bd, n_lat, ctx_row):
    B, T, D = x.shape
    NT = T + ctx.shape[1]
    TB = TOKEN_BLOCK
    tok = pl.BlockSpec((1, TB, D), lambda b, t: (b, t, 0))
    const2 = lambda shape: pl.BlockSpec(shape, lambda b, t: (0, 0))
    out = jax.ShapeDtypeStruct((B, NT, D), BF16)
    return pl.pallas_call(
        functools.partial(_attn_proj_kernel, n_lat=n_lat),
        grid=(B, NT // TB),
        in_specs=[
            *_stream_specs(n_lat, 0), const2((1, D)),
            _mod_spec(1, n_lat, ctx_row), _mod_spec(0, n_lat, ctx_row),
            const2((D, 3 * D)), const2((1, LANES)), const2((1, LANES)),
            pl.BlockSpec((TB, LANES), lambda b, t: (t, 0)),
            pl.BlockSpec((TB, LANES), lambda b, t: (t, 0)),
            const2((LANES, LANES)),
        ],
        out_specs=[tok, pl.BlockSpec((1, DA_HEADS, LANES, TB), lambda b, t: (b, 0, 0, t)), tok],
        out_shape=[out, jax.ShapeDtypeStruct((B, DA_HEADS, LANES, NT), BF16), out],
        compiler_params=_cparams(("parallel", "arbitrary")),
        name="attn_qkv_proj",
    )(x, ctx, g, mod3, mod3, w_qkv, qg, kg, cos_t, sin_t, bd)


def _attn_kernel(lam_ref, q_ref, kt_ref, v_ref, sn_ref, o_ref, *, t_lat, lam_init):
    QB = ATTN_Q_BLOCK
    lp = lam_ref[...]
    lam = (jnp.exp(jnp.sum(lp[0:1] * lp[1:2], keepdims=True))
           - jnp.exp(jnp.sum(lp[2:3] * lp[3:4], keepdims=True)) + lam_init)

    def scores(q, kt):
        lane = lax.broadcasted_iota(jnp.int32, q.shape, 1)
        zero = jnp.zeros_like(q)
        return (jnp.dot(jnp.where(lane < DA_QK_DIM, q, zero), kt, preferred_element_type=F32),
                jnp.dot(jnp.where(lane >= DA_QK_DIM, q, zero), kt, preferred_element_type=F32))

    def finish(s, v):
        def softmax_parts(sm):
            e = jnp.exp2(sm - jnp.max(sm, axis=-1, keepdims=True))
            return e, jnp.sum(e, axis=-1, keepdims=True)

        e0, l0 = softmax_parts(s[0])
        e1, l1 = softmax_parts(s[1])
        a = e0 - (lam * l0 / l1) * e1
        o = jnp.dot(a.astype(BF16), v, preferred_element_type=F32) * (1.0 / l0)
        o = o * lax.rsqrt(jnp.mean(o * o, axis=-1, keepdims=True) + EPS) * sn_ref[...]
        return (o * (1.0 - lam_init)).astype(BF16)

    G = math.gcd(ATTN_GROUP, t_lat // QB)

    def latent_group(i, carry):
        rows = [pl.ds(pl.multiple_of((G * i + j) * QB, QB), QB) for j in range(G)]
        s_next = scores(q_ref[0, rows[0], :], kt_ref[0, 0])
        for j in range(G):
            s_cur = s_next
            if j + 1 < G:
                s_next = scores(q_ref[0, rows[j + 1], :], kt_ref[0, 0])
            o_ref[0, rows[j], :] = finish(s_cur, v_ref[0])
        return carry

    lax.fori_loop(0, t_lat // (G * QB), latent_group, 0)
    o_ref[0, t_lat:, :] = finish(scores(q_ref[0, t_lat:, :], kt_ref[0, 0, :, t_lat:]),
                                 v_ref[0, t_lat:, :])


def _attention(lam_p, q, kt, v, sub_norm, t_lat, lam_init):
    B, NT, D = q.shape
    blk = pl.BlockSpec((1, NT, LANES), lambda b, h: (b, 0, h))
    return pl.pallas_call(
        functools.partial(_attn_kernel, t_lat=t_lat, lam_init=lam_init),
        grid=(B, DA_HEADS),
        in_specs=[
            pl.BlockSpec(lam_p.shape, lambda b, h: (0, 0)),
            blk, pl.BlockSpec((1, 1, LANES, NT), lambda b, h: (b, h, 0, 0)), blk,
            pl.BlockSpec((1, LANES), lambda b, h: (0, 0)),
        ],
        out_specs=blk,
        out_shape=jax.ShapeDtypeStruct((B, NT, D), BF16),
        compiler_params=_cparams(("parallel", "arbitrary")),
        name="diff_attention",
    )(lam_p, q, kt, v, sub_norm)


def _out_router_kernel(o_ref, x_ref, c_ref, w_ref, gm_ref, g_ref, sc_ref, sh_ref, rw_ref, rb_ref,
                       xo_ref, f_ref, bk_ref, *, n_lat):
    D = D_MODEL
    out = jnp.dot(o_ref[0], w_ref[...], preferred_element_type=F32)
    x = _stream_block(x_ref, c_ref, n_lat) + gm_ref[0] * out
    xo_ref[0] = x
    f = _norm_mod(x, g_ref[...], sc_ref[0], sh_ref[0])
    f_ref[0, :, :D] = f
    def split3(a):
        a1 = a.astype(BF16)
        r = a - a1.astype(F32)
        a2 = r.astype(BF16)
        return a1, a2, (r - a2.astype(F32)).astype(BF16)

    f1, f2, f3 = split3(f)
    w1, w2, w3 = split3(rw_ref[...])
    mm = functools.partial(jnp.dot, preferred_element_type=F32)
    logits_t = ((mm(f3, w1) + mm(f2, w2) + mm(f1, w3)) + (mm(f2, w1) + mm(f1, w2))) + mm(f1, w1)
    logits = logits_t.T[:N_EXPERTS]
    aff = jax.nn.sigmoid(logits)
    biased = aff + rb_ref[...]
    G, E = EXPERTS_PER_GROUP, N_EXPERTS
    row = lax.broadcasted_iota(jnp.int32, biased.shape, 0)
    member = lax.rem(row, G)
    group = row // G

    def shifted(x, k):
        return jnp.where(member + k < G, pltpu.roll(x, E - k, 0), pltpu.roll(x, G - k, 0))

    rank = jnp.zeros_like(row)
    for k in range(1, G):
        other = shifted(biased, k)
        ahead = (other > biased) | ((other == biased) & (member + k >= G))
        rank = rank + ahead.astype(jnp.int32)
    top2 = rank < 2
    t = jnp.where(top2, biased, 0.0)
    gscore = t
    for k in range(1, G):
        gscore = gscore + shifted(t, k)
    best = jnp.ones_like(top2)
    for m in range(1, N_GROUPS):
        other = pltpu.roll(gscore, G * m, 0)
        best = best & ((gscore > other) | ((gscore == other) & (group < m)))
    chosen = top2 & best
    lowest = jnp.min(jnp.where(chosen, row, E), axis=0, keepdims=True)
    is_lo = chosen & (row == lowest)
    is_hi = chosen & (row != lowest)
    a_lo = jnp.sum(jnp.where(is_lo, aff, 0.0), axis=0, keepdims=True)
    a_hi = jnp.sum(jnp.where(is_hi, aff, 0.0), axis=0, keepdims=True)
    m_lo = lax.rem(lowest, G)
    m_hi = jnp.sum(jnp.where(is_hi, member, 0), axis=0, keepdims=True)
    pair = m_lo * (2 * G - 1 - m_lo) // 2 + (m_hi - m_lo - 1)
    tot = a_lo + a_hi
    bk_ref[0] = (lowest // G) * len(PAIRS) + pair

    tb = x.shape[0]
    eye = (lax.broadcasted_iota(jnp.int32, (tb, tb), 0)
           == lax.broadcasted_iota(jnp.int32, (tb, tb), 1))

    def column(row):
        return jnp.sum(jnp.where(eye, row, 0.0), axis=1, keepdims=True)

    lane = lax.broadcasted_iota(jnp.int32, (tb, LANES), 1)
    f_ref[0, :, D:] = jnp.where(lane < LANES // 2, column(a_lo / tot), column(a_hi / tot))


def _out_router(o, x, ctx, ctx_block0, w_o, mod3, g_ffn, rwt, rb, n_lat, ctx_row):
    B, NT, D = o.shape
    TB = TOKEN_BLOCK
    nb = NT // TB
    tok = pl.BlockSpec((1, TB, D), lambda b, t: (b, t, 0))
    row = pl.BlockSpec((1, 1, TB), lambda b, t: (b * nb + t, 0, 0))
    const2 = lambda shape: pl.BlockSpec(shape, lambda b, t: (0, 0))
    rows = lambda dt: jax.ShapeDtypeStruct((B * nb, 1, TB), dt)
    return pl.pallas_call(
        functools.partial(_out_router_kernel, n_lat=n_lat),
        grid=(B, nb),
        in_specs=[
            tok, *_stream_specs(n_lat, ctx_block0), const2((D, D)),
            _mod_spec(2, n_lat, ctx_row), const2((1, D)),
            _mod_spec(4, n_lat, ctx_row), _mod_spec(3, n_lat, ctx_row),
            const2((D, LANES)), const2((N_EXPERTS, 1)),
        ],
        out_specs=[tok, pl.BlockSpec((1, TB, D + LANES), lambda b, t: (b, t, 0)), row],
        out_shape=[jax.ShapeDtypeStruct((B, NT, D), F32),
                   jax.ShapeDtypeStruct((B, NT, D + LANES), F32), rows(jnp.int32)],
        compiler_params=_cparams(("parallel", "arbitrary")),
        name="out_proj_router",
    )(o, x, ctx, w_o, mod3, g_ffn, mod3, mod3, rwt, rb)


def _routing_tables(bucket, n_tokens):
    tm = MOE_TILE
    max_tiles = n_tokens // tm + N_BUCKETS
    onehot = (bucket[:, None] == jnp.arange(N_BUCKETS, dtype=jnp.int32)[None, :]).astype(jnp.int32)
    count = jnp.sum(onehot, axis=0)
    rank = jnp.sum(jnp.cumsum(onehot, axis=0) * onehot, axis=1) - 1
    btiles = (count + tm - 1) // tm
    tile_end = jnp.cumsum(btiles)
    tile_start = tile_end - btiles
    dest = tile_start[bucket] * tm + rank
    tile = jnp.arange(max_tiles, dtype=jnp.int32)
    tile_bucket = jnp.minimum(jnp.sum((tile[:, None] >= tile_end[None, :]).astype(jnp.int32), axis=1),
                              N_BUCKETS - 1)
    in_bucket = tile - tile_start[tile_bucket]
    n_valid = jnp.clip(count[tile_bucket] - in_bucket * tm, 0, tm)
    n_valid = jnp.where(tile < tile_end[-1], n_valid, 0).astype(jnp.int32)
    grp = tile_bucket // len(PAIRS)
    pair = tile_bucket % len(PAIRS)
    lo_tab = jnp.array([p[0] for p in PAIRS], jnp.int32)
    hi_tab = jnp.array([p[1] for p in PAIRS], jnp.int32)
    tile_lo = grp * EXPERTS_PER_GROUP + lo_tab[pair]
    tile_hi = grp * EXPERTS_PER_GROUP + hi_tab[pair]
    return dest.astype(jnp.int32), n_valid, tile_lo, tile_hi


def _expert_kernel(dest_ref, nv_ref, tlo_ref, thi_ref, f_hbm, wg_lo_ref, wu_lo_ref, wd_lo_ref,
                   wg_hi_ref, wu_hi_ref, wd_hi_ref, y_hbm, xbuf, ybuf, tok_ref, gsem, ssem):
    TM, D = MOE_TILE, D_MODEL
    i = pl.program_id(0)
    n = pl.num_programs(0)
    slot = lax.rem(i, 2)
    other = 1 - slot

    @pl.when(i == 0)
    def _():
        U = MOE_DMA_UNROLL

        def invert(g, carry):
            for u in range(U):
                t = g * U + u
                tok_ref[dest_ref[t]] = t
            return carry

        lax.fori_loop(0, dest_ref.shape[0] // U, invert, 0)

    def gather_copy(tile, r, s):
        return pltpu.make_async_copy(f_hbm.at[pl.ds(tok_ref[tile * TM + r], 1), :],
                                     xbuf.at[s, pl.ds(r, 1), :], gsem.at[s])

    def scatter_copy(tile, r, s):
        return pltpu.make_async_copy(ybuf.at[s, pl.ds(r, 1), :],
                                     y_hbm.at[pl.ds(tok_ref[tile * TM + r], 1), :], ssem.at[s])

    def for_rows(n_rows, fn):
        U = MOE_DMA_UNROLL
        for g in range(TM // U):
            @pl.when((g + 1) * U <= n_rows)
            def _():
                for u in range(U):
                    fn(g * U + u)

        def single(r, carry):
            fn(r)
            return carry

        lax.fori_loop((n_rows // U) * U, n_rows, single, 0)

    def wait_rows(n_rows, whole_tile_copy, row_copy):
        @pl.when(n_rows == TM)
        def _():
            whole_tile_copy.wait()

        @pl.when(n_rows < TM)
        def _():
            for_rows(n_rows, lambda r: row_copy(r).wait())

    def start_gather(tile, s):
        for_rows(nv_ref[tile], lambda r: gather_copy(tile, r, s).start())

    def wait_gather(tile, s):
        wait_rows(nv_ref[tile],
                  pltpu.make_async_copy(f_hbm.at[pl.ds(0, TM), :], xbuf.at[s], gsem.at[s]),
                  lambda r: gather_copy(tile, r, s))

    def start_scatter(tile, s):
        for_rows(nv_ref[tile], lambda r: scatter_copy(tile, r, s).start())

    def wait_scatter(tile, s):
        wait_rows(nv_ref[tile],
                  pltpu.make_async_copy(ybuf.at[s], y_hbm.at[pl.ds(0, TM), :], ssem.at[s]),
                  lambda r: scatter_copy(tile, r, s))

    @pl.when(i == 0)
    def _():
        xbuf[...] = jnp.zeros_like(xbuf)
        start_gather(0, 0)

    wait_gather(i, slot)

    @pl.when(i >= 2)
    def _():
        wait_scatter(i - 2, slot)

    @pl.when(i + 1 < n)
    def _():
        start_gather(i + 1, other)

    @pl.when(nv_ref[i] > 0)
    def _():
        xe = xbuf[slot]
        x = xe[:, :D].astype(BF16)

        def ffn(wg_ref, wu_ref, wd_ref):
            hid = (_silu(jnp.dot(x, wg_ref[0, 0], preferred_element_type=F32))
                   * jnp.dot(x, wu_ref[0, 0], preferred_element_type=F32))
            return jnp.dot(hid.astype(BF16), wd_ref[0, 0], preferred_element_type=F32)

        ybuf[slot] = (xe[:, D:D + 1] * ffn(wg_lo_ref, wu_lo_ref, wd_lo_ref)
                      + xe[:, D + LANES // 2:D + LANES // 2 + 1] * ffn(wg_hi_ref, wu_hi_ref, wd_hi_ref))
        start_scatter(i, slot)

    @pl.when(i == n - 1)
    def _():
        wait_scatter(i - 1, other)
        wait_scatter(i, slot)


def _expert_ffn(tables, f_ext, layer, w_gate, w_up, w_down):
    dest, n_valid, tile_lo, tile_hi = tables
    N = f_ext.shape[0]
    assert N % MOE_DMA_UNROLL == 0
    D = D_MODEL
    tm = MOE_TILE
    lo = lambda i, dst, nv, tlo, thi: (layer, tlo[i], 0, 0)
    hi = lambda i, dst, nv, tlo, thi: (layer, thi[i], 0, 0)
    w_in, w_out = (1, 1, D, D_FF), (1, 1, D_FF, D)
    return pl.pallas_call(
        _expert_kernel,
        grid_spec=pltpu.PrefetchScalarGridSpec(
            num_scalar_prefetch=4,
            grid=(n_valid.shape[0],),
            in_specs=[
                pl.BlockSpec(memory_space=pl.ANY),
                pl.BlockSpec(w_in, lo), pl.BlockSpec(w_in, lo), pl.BlockSpec(w_out, lo),
                pl.BlockSpec(w_in, hi), pl.BlockSpec(w_in, hi), pl.BlockSpec(w_out, hi),
            ],
            out_specs=pl.BlockSpec(memory_space=pl.ANY),
            scratch_shapes=[pltpu.VMEM((2, tm, D + LANES), F32), pltpu.VMEM((2, tm, D), F32),
                            pltpu.SMEM((n_valid.shape[0] * tm,), jnp.int32),
                            pltpu.SemaphoreType.DMA((2,)), pltpu.SemaphoreType.DMA((2,))],
        ),
        out_shape=jax.ShapeDtypeStruct((N, D), F32),
        compiler_params=_cparams(("arbitrary",)),
        name="moe_expert_ffn",
    )(dest, n_valid, tile_lo, tile_hi, f_ext, w_gate, w_up, w_down, w_gate, w_up, w_down)


def _combine_kernel(x_ref, gf_ref, y_ref, o_ref):
    o_ref[0] = x_ref[0] + gf_ref[0] * y_ref[...]


def _moe_combine(xa, mod3, y, n_lat, ctx_row, n_blocks_out):
    B, NT, D = xa.shape
    TB = TOKEN_BLOCK
    nb = NT // TB
    tok = pl.BlockSpec((1, TB, D), lambda b, t: (b, t, 0))
    return pl.pallas_call(
        _combine_kernel,
        grid=(B, n_blocks_out),
        in_specs=[tok, _mod_spec(5, n_lat, ctx_row),
                  pl.BlockSpec((TB, D), lambda b, t: (b * nb + t, 0))],
        out_specs=tok,
        out_shape=jax.ShapeDtypeStruct((B, n_blocks_out * TB, D), F32),
        compiler_params=_cparams(("parallel", "arbitrary")),
        name="moe_combine",
    )(xa, mod3, y)


def _hgrn_proj_kernel(x_ref, y_ref, gf_ref, g_ref, sc_ref, sh_ref, w_ref, lb_ref,
                      xo_ref, q_ref, v_ref, kf_ref, lff_ref, kb_ref, lfb_ref, gate_ref):
    HK = HG_HEADS * HG_DIM
    x = x_ref[0] + gf_ref[0] * y_ref[...]
    xo_ref[0] = x
    h = _norm_mod(x, g_ref[...], sc_ref[0], sh_ref[0]).astype(BF16)

    W = HGRN_PROJ_PIECE

    def proj(part, c):
        return jnp.dot(h, w_ref[:, part * HK + c:part * HK + c + W], preferred_element_type=F32)

    def forget(z, lbd, k_ref, lf_ref, cols):
        e = jnp.exp(-jnp.abs(z))
        t = 1.0 + e
        k_ref[0, :, cols] = (1.0 - lbd) * (jnp.where(z >= 0.0, e, 1.0) / t)
        a = jnp.log(lbd)
        b = jnp.log1p(-lbd) + (jnp.minimum(z, 0.0) - jnp.log(t))
        lf_ref[0, :, cols] = jnp.maximum(a, b) + jnp.log(1.0 + jnp.exp(-jnp.abs(a - b)))

    def finish(part, c, y):
        cols = slice(c, c + W)
        if part == 0:
            q_ref[0, :, cols] = _silu(y)
        elif part == 1:
            v_ref[0, :, cols] = y.astype(BF16)
        elif part == 2:
            forget(y, lb_ref[0:1, cols], kf_ref, lff_ref, cols)
        elif part == 3:
            forget(y, lb_ref[1:2, cols], kb_ref, lfb_ref, cols)
        else:
            gate_ref[0, :, cols] = y

    pieces = [(part, c) for c in range(0, HK, W) for part in (2, 0, 3, 1, 4)]
    nxt = proj(*pieces[0])
    for k, piece in enumerate(pieces):
        cur = nxt
        if k + 1 < len(pieces):
            nxt = proj(*pieces[k + 1])
        finish(*piece, cur)


def _hgrn_project(xa, y, mod_prev, g, mod3, w_in, lb, n_lat, ctx_row):
    B, NT, D = xa.shape
    TB = TOKEN_BLOCK
    nb = NT // TB
    HK = HG_HEADS * HG_DIM
    tok = pl.BlockSpec((1, TB, D), lambda b, t: (b, t, 0))
    tok_o = pl.BlockSpec((1, TB, HK), lambda b, t: (b, t, 0))
    const2 = lambda shape: pl.BlockSpec(shape, lambda b, t: (0, 0))
    out = jax.ShapeDtypeStruct((B, NT, HK), F32)
    return pl.pallas_call(
        _hgrn_proj_kernel,
        grid=(B, nb),
        in_specs=[
            tok, pl.BlockSpec((TB, D), lambda b, t: (b * nb + t, 0)), _mod_spec(5, n_lat, ctx_row),
            const2((1, D)), _mod_spec(1, n_lat, ctx_row), _mod_spec(0, n_lat, ctx_row),
            pl.BlockSpec((D, 5 * HK), lambda b, t: (0, 0), pipeline_mode=pl.Buffered(1)),
            const2((2, HK)),
        ],
        out_specs=[tok] + [tok_o] * 7,
        out_shape=[jax.ShapeDtypeStruct((B, NT, D), F32), out,
                   jax.ShapeDtypeStruct((B, NT, HK), BF16)] + [out] * 5,
        compiler_params=_cparams(("parallel", "arbitrary")),
        name="hgrn_proj",
    )(xa, y, mod_prev, g, mod3, mod3, w_in, lb)


def _scan_chunk_scores(q, k, vb, a, forward):
    C, SB = SCAN_CHUNK, SCAN_SUB
    row = lax.broadcasted_iota(jnp.int32, a.shape, 0)
    first = row < SB
    mid_row = SB // 2
    m = jnp.where(first, a[mid_row:mid_row + 1], a[SB + mid_row:SB + mid_row + 1])
    qd = q * jnp.exp(jnp.minimum(a - m, EXP_CLAMP))
    kd = k * jnp.exp(jnp.minimum(m - a, EXP_CLAMP))
    edge = a[SB - 1:SB] if forward else a[SB:SB + 1]
    e_x = jnp.exp(-jnp.abs(a - edge))
    nt = (((1,), (1,)), ((), ()))
    s_d = lax.dot_general(qd.astype(BF16), kd.astype(BF16), nt, preferred_element_type=F32)
    s_x = lax.dot_general((q * e_x).astype(BF16), (k * e_x).astype(BF16), nt,
                          preferred_element_type=F32)
    a_out = a[C - 1:C] if forward else a[0:1]
    k_out = k * jnp.exp(a_out - a)
    upd = lax.dot_general(vb, k_out.astype(BF16), (((0,), (0,)), ((), ())),
                          preferred_element_type=F32)
    return s_d, s_x, q * jnp.exp(a), upd, jnp.exp(a_out)


def _scan_chunk_intra(s_d, s_x, vb, forward):
    SB = SCAN_SUB
    t_i = lax.broadcasted_iota(jnp.int32, s_d.shape, 0)
    s_i = lax.broadcasted_iota(jnp.int32, s_d.shape, 1)
    same = (t_i < SB) == (s_i < SB)
    if forward:
        causal, cross = s_i <= t_i, (t_i >= SB) & (s_i < SB)
    else:
        causal, cross = s_i >= t_i, (t_i < SB) & (s_i >= SB)
    scores = jnp.where(same & causal, s_d, jnp.where(cross, s_x, 0.0))
    return jnp.dot(scores.astype(BF16), vb, preferred_element_type=F32)


def _hgrn_scan_kernel(q_ref, v_ref, kf_ref, lff_ref, kb_ref, lfb_ref, gate_ref, gain_ref,
                      o_ref, acc_ref, qe_ref, upd_ref, dec_ref, inter_ref, *, n_chunks, n_lat_chunks):
    C, G = SCAN_CHUNK, SCAN_GROUP
    R = C * G
    pos = lax.rem(lax.broadcasted_iota(jnp.int32, (R, HG_DIM), 0), C)

    def cumsum(lf, forward):
        x = lf
        step = 1
        while step < C:
            if forward:
                x = x + jnp.where(pos >= step, pltpu.roll(x, step, 0), 0.0)
            else:
                x = x + jnp.where(pos < C - step, pltpu.roll(x, R - step, 0), 0.0)
            step *= 2
        return x

    def group_scores(g):
        rows = pl.ds(pl.multiple_of(g * R, R), R)
        q = q_ref[0, rows, :]
        vb = v_ref[0, rows, :].astype(BF16)
        kf, kb = kf_ref[0, rows, :], kb_ref[0, rows, :]
        a_f = cumsum(lff_ref[0, rows, :], True)
        a_b = cumsum(lfb_ref[0, rows, :], False)
        res = []
        for ci in range(G):
            sl = slice(ci * C, (ci + 1) * C)
            res.append((vb[sl], _scan_chunk_scores(q[sl], kf[sl], vb[sl], a_f[sl], True),
                        _scan_chunk_scores(q[sl], kb[sl], vb[sl], a_b[sl], False)))
        return res

    def group_finish(g, res, st_f):
        o_sum = [_scan_chunk_intra(fw[0], fw[1], vb, True) + _scan_chunk_intra(bw[0], bw[1], vb, False)
                 for vb, fw, bw in res]
        rows = pl.ds(pl.multiple_of(g * R, R), R)
        for d in range(2):
            p = l = None
            scaled = [None] * G
            for ci in (range(G) if d == 0 else reversed(range(G))):
                _, _, qe, upd, dec = res[ci][1 + d]
                if p is None:
                    scaled[ci] = qe.astype(BF16)
                    p, l = dec, upd
                else:
                    scaled[ci] = (qe * p).astype(BF16)
                    o_sum[ci] = o_sum[ci] + lax.dot_general(qe.astype(BF16), l.astype(BF16), nt,
                                                            preferred_element_type=F32)
                    p, l = p * dec, l * dec + upd
            qe_group = jnp.concatenate(scaled, axis=0)
            if d == 0:
                inter_ref[0, rows, :] = lax.dot_general(qe_group, st_f.astype(BF16), nt,
                                                        preferred_element_type=F32)
                st_f = st_f * p + l
            else:
                qe_ref[rows, :] = qe_group
                upd_ref[g] = l
                dec_ref[pl.ds(g, 1), :] = p
        for ci in range(G):
            acc_ref[pl.ds(pl.multiple_of((g * G + ci) * C, C), C), :] = o_sum[ci]
        return st_f

    nt = (((1,), (1,)), ((), ()))
    n_groups = n_chunks // G
    n_trip = math.gcd(SCAN_TRIP, n_groups)
    zero = jnp.zeros((HG_DIM, HG_DIM), F32)

    def forward_group(k):
        return lax.rem(k + n_lat_chunks // G, n_groups)

    def local(i, st_f):
        nxt = group_scores(forward_group(i * n_trip))
        for j in range(n_trip):
            cur = nxt
            if j + 1 < n_trip:
                nxt = group_scores(forward_group(i * n_trip + j + 1))
            st_f = group_finish(forward_group(i * n_trip + j), cur, st_f)
        return st_f

    lax.fori_loop(0, n_groups // n_trip, local, zero)

    st_b = zero
    for gb in reversed(range(n_groups)):
        inter_ref[1, gb * R:(gb + 1) * R, :] = lax.dot_general(
            qe_ref[gb * R:(gb + 1) * R, :], st_b.astype(BF16), nt, preferred_element_type=F32)
        st_b = st_b * dec_ref[gb:gb + 1, :] + upd_ref[gb]
    o = acc_ref[...] + inter_ref[0] + inter_ref[1]
    o = o * lax.rsqrt(jnp.mean(o * o, axis=-1, keepdims=True) + EPS) * gain_ref[...]
    o_ref[0] = (o * _silu(gate_ref[0])).astype(BF16)


def _hgrn_scan(q, v, kf, lff, kb, lfb, gate, gain, t_lat):
    B, NT, HK = q.shape
    blk = pl.BlockSpec((1, NT, HG_DIM), lambda b, h: (b, 0, h))
    return pl.pallas_call(
        functools.partial(_hgrn_scan_kernel, n_chunks=NT // SCAN_CHUNK,
                          n_lat_chunks=t_lat // SCAN_CHUNK),
        grid=(B, HG_HEADS),
        in_specs=[blk] * 7 + [pl.BlockSpec((1, HG_DIM), lambda b, h: (0, 0))],
        out_specs=blk,
        out_shape=jax.ShapeDtypeStruct((B, NT, HK), BF16),
        scratch_shapes=[pltpu.VMEM((NT, HG_DIM), F32),
                        pltpu.VMEM((NT, HG_DIM), BF16),
                        pltpu.VMEM((NT // (SCAN_CHUNK * SCAN_GROUP), HG_DIM, HG_DIM), F32),
                        pltpu.VMEM((NT // (SCAN_CHUNK * SCAN_GROUP), HG_DIM), F32),
                        pltpu.VMEM((2, NT, HG_DIM), F32)],
        compiler_params=_cparams(("parallel", "arbitrary")),
        name="hgrn_scan",
    )(q, v, kf, lff, kb, lfb, gate, gain)


def _rope_tables(t_lat, t_ctx):
    rows = t_lat // GRID_W
    r = jnp.repeat(jnp.arange(rows, dtype=F32), GRID_W)
    col = jnp.tile(jnp.arange(GRID_W, dtype=F32), rows)
    n_pairs = DA_QK_DIM // 4
    inv = ROPE_BASE ** (-jnp.arange(n_pairs, dtype=F32) / n_pairs)
    ang = jnp.concatenate([r[:, None] * inv, col[:, None] * inv], axis=-1)
    cos, sin = jnp.cos(ang), jnp.sin(ang)
    cos_l = jnp.tile(cos, (1, LANES // cos.shape[1]))
    sin_l = jnp.tile(jnp.concatenate([-sin, sin], axis=-1), (1, LANES // (2 * sin.shape[1])))
    cos_l = jnp.concatenate([cos_l, jnp.ones((t_ctx, LANES), F32)], axis=0)
    sin_l = jnp.concatenate([sin_l, jnp.zeros((t_ctx, LANES), F32)], axis=0)
    return cos_l, sin_l


def kernel(x, c, ctx, c_ctx, ada_w, ada_b, norm_mix, norm_ffn, attn_w_qkv, attn_w_o, attn_q_norm,
           attn_k_norm, attn_sub_norm, attn_lambda, hgrn_w_in, hgrn_w_o, hgrn_out_norm,
           hgrn_lb_gamma, router_w, router_bias, moe_w_gate, moe_w_up, moe_w_down):
    B, T, D = x.shape
    Tc = ctx.shape[1]
    TB = TOKEN_BLOCK
    assert D == D_MODEL and T % TB == 0 and Tc % TB == 0 and T % GRID_W == 0
    assert ada_w.shape[0] == DEPTH == 2
    assert T % (SCAN_CHUNK * SCAN_GROUP) == 0 and Tc % (SCAN_CHUNK * SCAN_GROUP) == 0
    assert T % ATTN_Q_BLOCK == 0
    NT = T + Tc
    n_lat = T // TB
    n_tokens = B * NT
    ctx_row = B

    n_rows = -(-(B + 1) // 8) * 8
    cvec = jnp.concatenate([c, c_ctx[None, :], jnp.zeros((n_rows - B - 1, D), F32)], axis=0)
    mod = _modulation(cvec, ada_w, ada_b)

    wg_all, wu_all, wd_all = (w.astype(BF16) for w in (moe_w_gate, moe_w_up, moe_w_down))
    rwt = jnp.pad(router_w.astype(F32), ((0, 0), (0, LANES - N_EXPERTS)))
    rb = router_bias.reshape(N_EXPERTS, 1)

    p = jax.nn.softmax(hgrn_lb_gamma.astype(F32), axis=1)
    cum = jnp.cumsum(p, axis=1)
    lb_all = cum - cum[:, :1]

    cos_t, sin_t = _rope_tables(T, Tc)
    lane = jnp.arange(LANES)
    bd = (lane[:, None] // DA_QK_DIM == lane[None, :] // DA_QK_DIM).astype(BF16)

    xa = None
    pending = None
    for i in range(DEPTH):
        mod3 = mod[i].reshape(n_rows, 1, 6 * D)
        j = i // 2
        g_mix = norm_mix[i].reshape(1, D)
        if i % 2 == 0:
            assert i == 0
            lam_init = 0.8 - 0.6 * math.exp(-0.3 * i)
            q, k, v = _attn_project(
                x, ctx, g_mix, mod3, attn_w_qkv[j].astype(BF16),
                jnp.tile(attn_q_norm[j], LANES // DA_QK_DIM).reshape(1, LANES),
                jnp.tile(attn_k_norm[j], LANES // DA_QK_DIM).reshape(1, LANES),
                cos_t, sin_t, bd, n_lat, ctx_row)
            o = _attention(attn_lambda[j], q, k, v, attn_sub_norm[j].reshape(1, DA_V_DIM),
                           T, lam_init)
            w_o = attn_w_o[j]
        else:
            xa, *parts = _hgrn_project(xa, *pending, g_mix, mod3, hgrn_w_in[j].astype(BF16),
                                       lb_all[:, i], n_lat, ctx_row)
            o = _hgrn_scan(*parts, hgrn_out_norm[j].reshape(1, HG_DIM), T)
            w_o = hgrn_w_o[j]
        stream = (x, ctx, 0) if xa is None else (xa, xa, n_lat)
        xa, f_ext, bucket = _out_router(
            o, *stream, w_o.astype(BF16), mod3, norm_ffn[i].reshape(1, D), rwt, rb, n_lat, ctx_row)
        tables = _routing_tables(bucket.reshape(n_tokens), n_tokens)
        y = _expert_ffn(tables, f_ext.reshape(n_tokens, D + LANES), i, wg_all, wu_all, wd_all)
        pending = (y, mod3)
    assert DEPTH % 2 == 0
    y, mod3 = pending
    return _moe_combine(xa, mod3, y, n_lat, ctx_row, n_lat)
```

```python
import functools
import math

import jax
import jax.numpy as jnp
from jax import lax
from jax.experimental import pallas as pl
from jax.experimental.pallas import tpu as pltpu

F32 = jnp.float32
BF16 = jnp.bfloat16
HIGHEST = lax.Precision.HIGHEST

D_MODEL = 1024
DEPTH = 2
GRID_W = 64
DA_HEADS = 8
DA_QK_DIM = 64
DA_V_DIM = 128
ROPE_BASE = 10000.0
HG_HEADS = 8
HG_DIM = 128
N_EXPERTS = 16
N_GROUPS = 4
EXPERTS_PER_GROUP = 4
D_FF = 512
EPS = 1e-6

LANES = 128
TOKEN_BLOCK = 256
ATTN_Q_BLOCK = 256
ATTN_GROUP = 8
MOE_TILE = 256
MOE_DMA_UNROLL = 32
PAIRS = [(i, j) for i in range(EXPERTS_PER_GROUP) for j in range(i + 1, EXPERTS_PER_GROUP)]
N_BUCKETS = N_GROUPS * len(PAIRS)
HGRN_PROJ_PIECE = 256
SCAN_CHUNK = 64
SCAN_SUB = SCAN_CHUNK // 2
SCAN_GROUP = 4
SCAN_TRIP = 9
EXP_CLAMP = 80.0
VMEM_LIMIT = 56 * 1024 * 1024


def _cparams(sem):
    return pltpu.CompilerParams(dimension_semantics=sem, vmem_limit_bytes=VMEM_LIMIT)


def _silu(x):
    return x * jax.nn.sigmoid(x)


def _norm_mod(x, g, sc, sh):
    y = x * lax.rsqrt(jnp.mean(x * x, axis=-1, keepdims=True) + EPS)
    return (y * g) * (1.0 + sc) + sh


def _mod_kernel(c_ref, w_ref, b_ref, o_ref):
    o_ref[0] = jnp.dot(_silu(c_ref[...]), w_ref[0], preferred_element_type=F32,
                       precision=HIGHEST) + b_ref[0]


def _modulation(cvec, ada_w, ada_b):
    R, D = cvec.shape
    depth, _, n6 = ada_w.shape
    tn = 1024
    return pl.pallas_call(
        _mod_kernel,
        grid=(depth, n6 // tn),
        in_specs=[
            pl.BlockSpec((R, D), lambda i, j: (0, 0)),
            pl.BlockSpec((1, D, tn), lambda i, j: (i, 0, j)),
            pl.BlockSpec((1, 1, tn), lambda i, j: (i, 0, j)),
        ],
        out_specs=pl.BlockSpec((1, R, tn), lambda i, j: (i, 0, j)),
        out_shape=jax.ShapeDtypeStruct((depth, R, n6), F32),
        compiler_params=_cparams(("arbitrary", "arbitrary")),
        name="adaln_mod",
    )(cvec, ada_w, ada_b.reshape(depth, 1, n6))


def _mod_spec(col, n_lat, ctx_row):
    return pl.BlockSpec((1, 1, D_MODEL), lambda b, t: (jnp.where(t < n_lat, b, ctx_row), 0, col))


def _stream_specs(n_lat, ctx_block0):
    blk = (1, TOKEN_BLOCK, D_MODEL)
    return (pl.BlockSpec(blk, lambda b, t: (b, jnp.minimum(t, n_lat - 1), 0)),
            pl.BlockSpec(blk, lambda b, t: (b, jnp.maximum(t - n_lat, 0) + ctx_block0, 0)))


def _stream_block(lat_ref, ctx_ref, n_lat):
    return jnp.where(pl.program_id(1) < n_lat, lat_ref[0], ctx_ref[0])


def _attn_proj_kernel(x_ref, c_ref, g_ref, sc_ref, sh_ref, w_ref, qg_ref, kg_ref, cos_ref, sin_ref,
                      bd_ref, q_ref, k_ref, v_ref, *, n_lat):
    D = D_MODEL
    h = _norm_mod(_stream_block(x_ref, c_ref, n_lat), g_ref[...], sc_ref[0], sh_ref[0])
    qkv = jnp.dot(h.astype(BF16), w_ref[...], preferred_element_type=F32)
    cos, sin, bd = cos_ref[...], sin_ref[...], bd_ref[...]
    lane = lax.broadcasted_iota(jnp.int32, cos.shape, 1)
    upper = (lane & (DA_QK_DIM // 2)) != 0

    def norm_rope(t, gain, scale):
        sq = t * t
        sq_hi = sq.astype(BF16)
        sq_lo = (sq - sq_hi.astype(F32)).astype(BF16)
        ss = (jnp.dot(sq_hi, bd, preferred_element_type=F32)
              + jnp.dot(sq_lo, bd, preferred_element_type=F32))
        tn = t * lax.rsqrt(ss * (1.0 / DA_QK_DIM) + EPS) * gain
        partner = jnp.where(upper, pltpu.roll(tn, DA_QK_DIM // 2, 1),
                            pltpu.roll(tn, LANES - DA_QK_DIM // 2, 1))
        return (tn * cos + partner * sin) * scale

    for j in range(D // LANES):
        sl = slice(j * LANES, (j + 1) * LANES)
        q_ref[0, :, sl] = norm_rope(qkv[:, j * LANES:(j + 1) * LANES], qg_ref[...],
                                    math.log2(math.e) / math.sqrt(DA_QK_DIM)).astype(BF16)
        k_ref[0, j] = norm_rope(qkv[:, D + j * LANES:D + (j + 1) * LANES], kg_ref[...],
                                1.0).T.astype(BF16)
    v_ref[0] = qkv[:, 2 * D:].astype(BF16)


def _attn_project(x, ctx, g, mod3, w_qkv, qg, kg, cos_t, sin_t, bd, n_lat, ctx_row):
    B, T, D = x.shape
    NT = T + ctx.shape[1]
    TB = TOKEN_BLOCK
    tok = pl.BlockSpec((1, TB, D), lambda b, t: (b, t, 0))
    const2 = lambda shape: pl.BlockSpec(shape, lambda b, t: (0, 0))
    out = jax.ShapeDtypeStruct((B, NT, D), BF16)
    return pl.pallas_call(
        functools.partial(_attn_proj_kernel, n_lat=n_lat),
        grid=(B, NT // TB),
        in_specs=[
            *_stream_specs(n_lat, 0), const2((1, D)),
            _mod_spec(1, n_lat, ctx_row), _mod_spec(0, n_lat, ctx_row),
            const2((D, 3 * D)), const2((1, LANES)), const2((1, LANES)),
            pl.BlockSpec((TB, LANES), lambda b, t: (t, 0)),
            pl.BlockSpec((TB, LANES), lambda b, t: (t, 0)),
            const2((LANES, LANES)),
        ],
        out_specs=[tok, pl.BlockSpec((1, DA_HEADS, LANES, TB), lambda b, t: (b, 0, 0, t)), tok],
        out_shape=[out, jax.ShapeDtypeStruct((B, DA_HEADS, LANES, NT), BF16), out],
        compiler_params=_cparams(("parallel", "arbitrary")),
        name="attn_qkv_proj",
    )(x, ctx, g, mod3, mod3, w_qkv, qg, kg, cos_t, sin_t, bd)


def _attn_kernel(lam_ref, q_ref, kt_ref, v_ref, sn_ref, o_ref, *, t_lat, lam_init):
    QB = ATTN_Q_BLOCK
    lp = lam_ref[...]
    lam = (jnp.exp(jnp.sum(lp[0:1] * lp[1:2], keepdims=True))
           - jnp.exp(jnp.sum(lp[2:3] * lp[3:4], keepdims=True)) + lam_init)

    def scores(q, kt):
        lane = lax.broadcasted_iota(jnp.int32, q.shape, 1)
        zero = jnp.zeros_like(q)
        return (jnp.dot(jnp.where(lane < DA_QK_DIM, q, zero), kt, preferred_element_type=F32),
                jnp.dot(jnp.where(lane >= DA_QK_DIM, q, zero), kt, preferred_element_type=F32))

    def finish(s, v):
        def softmax_parts(sm):
            e = jnp.exp2(sm - jnp.max(sm, axis=-1, keepdims=True))
            return e, jnp.sum(e, axis=-1, keepdims=True)

        e0, l0 = softmax_parts(s[0])
        e1, l1 = softmax_parts(s[1])
        a = e0 - (lam * l0 / l1) * e1
        o = jnp.dot(a.astype(BF16), v, preferred_element_type=F32) * (1.0 / l0)
        o = o * lax.rsqrt(jnp.mean(o * o, axis=-1, keepdims=True) + EPS) * sn_ref[...]
        return (o * (1.0 - lam_init)).astype(BF16)

    G = math.gcd(ATTN_GROUP, t_lat // QB)

    def latent_group(i, carry):
        rows = [pl.ds(pl.multiple_of((G * i + j) * QB, QB), QB) for j in range(G)]
        s_next = scores(q_ref[0, rows[0], :], kt_ref[0, 0])
        for j in range(G):
            s_cur = s_next
            if j + 1 < G:
                s_next = scores(q_ref[0, rows[j + 1], :], kt_ref[0, 0])
            o_ref[0, rows[j], :] = finish(s_cur, v_ref[0])
        return carry

    lax.fori_loop(0, t_lat // (G * QB), latent_group, 0)
    o_ref[0, t_lat:, :] = finish(scores(q_ref[0, t_lat:, :], kt_ref[0, 0, :, t_lat:]),
                                 v_ref[0, t_lat:, :])


def _attention(lam_p, q, kt, v, sub_norm, t_lat, lam_init):
    B, NT, D = q.shape
    blk = pl.BlockSpec((1, NT, LANES), lambda b, h: (b, 0, h))
    return pl.pallas_call(
        functools.partial(_attn_kernel, t_lat=t_lat, lam_init=lam_init),
        grid=(B, DA_HEADS),
        in_specs=[
            pl.BlockSpec(lam_p.shape, lambda b, h: (0, 0)),
            blk, pl.BlockSpec((1, 1, LANES, NT), lambda b, h: (b, h, 0, 0)), blk,
            pl.BlockSpec((1, LANES), lambda b, h: (0, 0)),
        ],
        out_specs=blk,
        out_shape=jax.ShapeDtypeStruct((B, NT, D), BF16),
        compiler_params=_cparams(("parallel", "arbitrary")),
        name="diff_attention",
    )(lam_p, q, kt, v, sub_norm)


def _out_router_kernel(o_ref, x_ref, c_ref, w_ref, gm_ref, g_ref, sc_ref, sh_ref, rw_ref, rb_ref,
                       xo_ref, f_ref, bk_ref, *, n_lat):
    D = D_MODEL
    out = jnp.dot(o_ref[0], w_ref[...], preferred_element_type=F32)
    x = _stream_block(x_ref, c_ref, n_lat) + gm_ref[0] * out
    xo_ref[0] = x
    f = _norm_mod(x, g_ref[...], sc_ref[0], sh_ref[0])
    f_ref[0, :, :D] = f
    def split3(a):
        a1 = a.astype(BF16)
        r = a - a1.astype(F32)
        a2 = r.astype(BF16)
        return a1, a2, (r - a2.astype(F32)).astype(BF16)

    f1, f2, f3 = split3(f)
    w1, w2, w3 = split3(rw_ref[...])
    mm = functools.partial(jnp.dot, preferred_element_type=F32)
    logits_t = ((mm(f3, w1) + mm(f2, w2) + mm(f1, w3)) + (mm(f2, w1) + mm(f1, w2))) + mm(f1, w1)
    logits = logits_t.T[:N_EXPERTS]
    aff = jax.nn.sigmoid(logits)
    biased = aff + rb_ref[...]
    G, E = EXPERTS_PER_GROUP, N_EXPERTS
    row = lax.broadcasted_iota(jnp.int32, biased.shape, 0)
    member = lax.rem(row, G)
    group = row // G

    def shifted(x, k):
        return jnp.where(member + k < G, pltpu.roll(x, E - k, 0), pltpu.roll(x, G - k, 0))

    rank = jnp.zeros_like(row)
    for k in range(1, G):
        other = shifted(biased, k)
        ahead = (other > biased) | ((other == biased) & (member + k >= G))
        rank = rank + ahead.astype(jnp.int32)
    top2 = rank < 2
    t = jnp.where(top2, biased, 0.0)
    gscore = t
    for k in range(1, G):
        gscore = gscore + shifted(t, k)
    best = jnp.ones_like(top2)
    for m in range(1, N_GROUPS):
        other = pltpu.roll(gscore, G * m, 0)
        best = best & ((gscore > other) | ((gscore == other) & (group < m)))
    chosen = top2 & best
    lowest = jnp.min(jnp.where(chosen, row, E), axis=0, keepdims=True)
    is_lo = chosen & (row == lowest)
    is_hi = chosen & (row != lowest)
    a_lo = jnp.sum(jnp.where(is_lo, aff, 0.0), axis=0, keepdims=True)
    a_hi = jnp.sum(jnp.where(is_hi, aff, 0.0), axis=0, keepdims=True)
    m_lo = lax.rem(lowest, G)
    m_hi = jnp.sum(jnp.where(is_hi, member, 0), axis=0, keepdims=True)
    pair = m_lo * (2 * G - 1 - m_lo) // 2 + (m_hi - m_lo - 1)
    tot = a_lo + a_hi
    bk_ref[0] = (lowest // G) * len(PAIRS) + pair

    tb = x.shape[0]
    eye = (lax.broadcasted_iota(jnp.int32, (tb, tb), 0)
           == lax.broadcasted_iota(jnp.int32, (tb, tb), 1))

    def column(row):
        return jnp.sum(jnp.where(eye, row, 0.0), axis=1, keepdims=True)

    lane = lax.broadcasted_iota(jnp.int32, (tb, LANES), 1)
    f_ref[0, :, D:] = jnp.where(lane < LANES // 2, column(a_lo / tot), column(a_hi / tot))


def _out_router(o, x, ctx, ctx_block0, w_o, mod3, g_ffn, rwt, rb, n_lat, ctx_row):
    B, NT, D = o.shape
    TB = TOKEN_BLOCK
    nb = NT // TB
    tok = pl.BlockSpec((1, TB, D), lambda b, t: (b, t, 0))
    row = pl.BlockSpec((1, 1, TB), lambda b, t: (b * nb + t, 0, 0))
    const2 = lambda shape: pl.BlockSpec(shape, lambda b, t: (0, 0))
    rows = lambda dt: jax.ShapeDtypeStruct((B * nb, 1, TB), dt)
    return pl.pallas_call(
        functools.partial(_out_router_kernel, n_lat=n_lat),
        grid=(B, nb),
        in_specs=[
            tok, *_stream_specs(n_lat, ctx_block0), const2((D, D)),
            _mod_spec(2, n_lat, ctx_row), const2((1, D)),
            _mod_spec(4, n_lat, ctx_row), _mod_spec(3, n_lat, ctx_row),
            const2((D, LANES)), const2((N_EXPERTS, 1)),
        ],
        out_specs=[tok, pl.BlockSpec((1, TB, D + LANES), lambda b, t: (b, t, 0)), row],
        out_shape=[jax.ShapeDtypeStruct((B, NT, D), F32),
                   jax.ShapeDtypeStruct((B, NT, D + LANES), F32), rows(jnp.int32)],
        compiler_params=_cparams(("parallel", "arbitrary")),
        name="out_proj_router",
    )(o, x, ctx, w_o, mod3, g_ffn, mod3, mod3, rwt, rb)


def _routing_tables(bucket, n_tokens):
    tm = MOE_TILE
    max_tiles = n_tokens // tm + N_BUCKETS
    onehot = (bucket[:, None] == jnp.arange(N_BUCKETS, dtype=jnp.int32)[None, :]).astype(jnp.int32)
    count = jnp.sum(onehot, axis=0)
    rank = jnp.sum(jnp.cumsum(onehot, axis=0) * onehot, axis=1) - 1
    btiles = (count + tm - 1) // tm
    tile_end = jnp.cumsum(btiles)
    tile_start = tile_end - btiles
    dest = tile_start[bucket] * tm + rank
    tile = jnp.arange(max_tiles, dtype=jnp.int32)
    tile_bucket = jnp.minimum(jnp.sum((tile[:, None] >= tile_end[None, :]).astype(jnp.int32), axis=1),
                              N_BUCKETS - 1)
    in_bucket = tile - tile_start[tile_bucket]
    n_valid = jnp.clip(count[tile_bucket] - in_bucket * tm, 0, tm)
    n_valid = jnp.where(tile < tile_end[-1], n_valid, 0).astype(jnp.int32)
    grp = tile_bucket // len(PAIRS)
    pair = tile_bucket % len(PAIRS)
    lo_tab = jnp.array([p[0] for p in PAIRS], jnp.int32)
    hi_tab = jnp.array([p[1] for p in PAIRS], jnp.int32)
    tile_lo = grp * EXPERTS_PER_GROUP + lo_tab[pair]
    tile_hi = grp * EXPERTS_PER_GROUP + hi_tab[pair]
    return dest.astype(jnp.int32), n_valid, tile_lo, tile_hi


def _expert_kernel(dest_ref, nv_ref, tlo_ref, thi_ref, f_hbm, wg_lo_ref, wu_lo_ref, wd_lo_ref,
                   wg_hi_ref, wu_hi_ref, wd_hi_ref, y_hbm, xbuf, ybuf, tok_ref, gsem, ssem):
    TM, D = MOE_TILE, D_MODEL
    i = pl.program_id(0)
    n = pl.num_programs(0)
    slot = lax.rem(i, 2)
    other = 1 - slot

    @pl.when(i == 0)
    def _():
        U = MOE_DMA_UNROLL

        def invert(g, carry):
            for u in range(U):
                t = g * U + u
                tok_ref[dest_ref[t]] = t
            return carry

        lax.fori_loop(0, dest_ref.shape[0] // U, invert, 0)

    def gather_copy(tile, r, s):
        return pltpu.make_async_copy(f_hbm.at[pl.ds(tok_ref[tile * TM + r], 1), :],
                                     xbuf.at[s, pl.ds(r, 1), :], gsem.at[s])

    def scatter_copy(tile, r, s):
        return pltpu.make_async_copy(ybuf.at[s, pl.ds(r, 1), :],
                                     y_hbm.at[pl.ds(tok_ref[tile * TM + r], 1), :], ssem.at[s])

    def for_rows(n_rows, fn):
        U = MOE_DMA_UNROLL
        for g in range(TM // U):
            @pl.when((g + 1) * U <= n_rows)
            def _():
                for u in range(U):
                    fn(g * U + u)

        def single(r, carry):
            fn(r)
            return carry

        lax.fori_loop((n_rows // U) * U, n_rows, single, 0)

    def wait_rows(n_rows, whole_tile_copy, row_copy):
        @pl.when(n_rows == TM)
        def _():
            whole_tile_copy.wait()

        @pl.when(n_rows < TM)
        def _():
            for_rows(n_rows, lambda r: row_copy(r).wait())

    def start_gather(tile, s):
        for_rows(nv_ref[tile], lambda r: gather_copy(tile, r, s).start())

    def wait_gather(tile, s):
        wait_rows(nv_ref[tile],
                  pltpu.make_async_copy(f_hbm.at[pl.ds(0, TM), :], xbuf.at[s], gsem.at[s]),
                  lambda r: gather_copy(tile, r, s))

    def start_scatter(tile, s):
        for_rows(nv_ref[tile], lambda r: scatter_copy(tile, r, s).start())

    def wait_scatter(tile, s):
        wait_rows(nv_ref[tile],
                  pltpu.make_async_copy(ybuf.at[s], y_hbm.at[pl.ds(0, TM), :], ssem.at[s]),
                  lambda r: scatter_copy(tile, r, s))

    @pl.when(i == 0)
    def _():
        xbuf[...] = jnp.zeros_like(xbuf)
        start_gather(0, 0)

    wait_gather(i, slot)

    @pl.when(i >= 2)
    def _():
        wait_scatter(i - 2, slot)

    @pl.when(i + 1 < n)
    def _():
        start_gather(i + 1, other)

    @pl.when(nv_ref[i] > 0)
    def _():
        xe = xbuf[slot]
        x = xe[:, :D].astype(BF16)

        def ffn(wg_ref, wu_ref, wd_ref):
            hid = (_silu(jnp.dot(x, wg_ref[0, 0], preferred_element_type=F32))
                   * jnp.dot(x, wu_ref[0, 0], preferred_element_type=F32))
            return jnp.dot(hid.astype(BF16), wd_ref[0, 0], preferred_element_type=F32)

        ybuf[slot] = (xe[:, D:D + 1] * ffn(wg_lo_ref, wu_lo_ref, wd_lo_ref)
                      + xe[:, D + LANES // 2:D + LANES // 2 + 1] * ffn(wg_hi_ref, wu_hi_ref, wd_hi_ref))
        start_scatter(i, slot)

    @pl.when(i == n - 1)
    def _():
        wait_scatter(i - 1, other)
        wait_scatter(i, slot)


def _expert_ffn(tables, f_ext, layer, w_gate, w_up, w_down):
    dest, n_valid, tile_lo, tile_hi = tables
    N = f_ext.shape[0]
    assert N % MOE_DMA_UNROLL == 0
    D = D_MODEL
    tm = MOE_TILE
    lo = lambda i, dst, nv, tlo, thi: (layer, tlo[i], 0, 0)
    hi = lambda i, dst, nv, tlo, thi: (layer, thi[i], 0, 0)
    w_in, w_out = (1, 1, D, D_FF), (1, 1, D_FF, D)
    return pl.pallas_call(
        _expert_kernel,
        grid_spec=pltpu.PrefetchScalarGridSpec(
            num_scalar_prefetch=4,
            grid=(n_valid.shape[0],),
            in_specs=[
                pl.BlockSpec(memory_space=pl.ANY),
                pl.BlockSpec(w_in, lo), pl.BlockSpec(w_in, lo), pl.BlockSpec(w_out, lo),
                pl.BlockSpec(w_in, hi), pl.BlockSpec(w_in, hi), pl.BlockSpec(w_out, hi),
            ],
            out_specs=pl.BlockSpec(memory_space=pl.ANY),
            scratch_shapes=[pltpu.VMEM((2, tm, D + LANES), F32), pltpu.VMEM((2, tm, D), F32),
                            pltpu.SMEM((n_valid.shape[0] * tm,), jnp.int32),
                            pltpu.SemaphoreType.DMA((2,)), pltpu.SemaphoreType.DMA((2,))],
        ),
        out_shape=jax.ShapeDtypeStruct((N, D), F32),
        compiler_params=_cparams(("arbitrary",)),
        name="moe_expert_ffn",
    )(dest, n_valid, tile_lo, tile_hi, f_ext, w_gate, w_up, w_down, w_gate, w_up, w_down)


def _combine_kernel(x_ref, gf_ref, y_ref, o_ref):
    o_ref[0] = x_ref[0] + gf_ref[0] * y_ref[...]


def _moe_combine(xa, mod3, y, n_lat, ctx_row, n_blocks_out):
    B, NT, D = xa.shape
    TB = TOKEN_BLOCK
    nb = NT // TB
    tok = pl.BlockSpec((1, TB, D), lambda b, t: (b, t, 0))
    return pl.pallas_call(
        _combine_kernel,
        grid=(B, n_blocks_out),
        in_specs=[tok, _mod_spec(5, n_lat, ctx_row),
                  pl.BlockSpec((TB, D), lambda b, t: (b * nb + t, 0))],
        out_specs=tok,
        out_shape=jax.ShapeDtypeStruct((B, n_blocks_out * TB, D), F32),
        compiler_params=_cparams(("parallel", "arbitrary")),
        name="moe_combine",
    )(xa, mod3, y)


def _hgrn_proj_kernel(x_ref, y_ref, gf_ref, g_ref, sc_ref, sh_ref, w_ref, lb_ref,
                      xo_ref, q_ref, v_ref, kf_ref, lff_ref, kb_ref, lfb_ref, gate_ref):
    HK = HG_HEADS * HG_DIM
    x = x_ref[0] + gf_ref[0] * y_ref[...]
    xo_ref[0] = x
    h = _norm_mod(x, g_ref[...], sc_ref[0], sh_ref[0]).astype(BF16)

    W = HGRN_PROJ_PIECE

    def proj(part, c):
        return jnp.dot(h, w_ref[:, part * HK + c:part * HK + c + W], preferred_element_type=F32)

    def forget(z, lbd, k_ref, lf_ref, cols):
        e = jnp.exp(-jnp.abs(z))
        t = 1.0 + e
        k_ref[0, :, cols] = (1.0 - lbd) * (jnp.where(z >= 0.0, e, 1.0) / t)
        a = jnp.log(lbd)
        b = jnp.log1p(-lbd) + (jnp.minimum(z, 0.0) - jnp.log(t))
        lf_ref[0, :, cols] = jnp.maximum(a, b) + jnp.log(1.0 + jnp.exp(-jnp.abs(a - b)))

    def finish(part, c, y):
        cols = slice(c, c + W)
        if part == 0:
            q_ref[0, :, cols] = _silu(y)
        elif part == 1:
            v_ref[0, :, cols] = y.astype(BF16)
        elif part == 2:
            forget(y, lb_ref[0:1, cols], kf_ref, lff_ref, cols)
        elif part == 3:
            forget(y, lb_ref[1:2, cols], kb_ref, lfb_ref, cols)
        else:
            gate_ref[0, :, cols] = y

    pieces = [(part, c) for c in range(0, HK, W) for part in (2, 0, 3, 1, 4)]
    nxt = proj(*pieces[0])
    for k, piece in enumerate(pieces):
        cur = nxt
        if k + 1 < len(pieces):
            nxt = proj(*pieces[k + 1])
        finish(*piece, cur)


def _hgrn_project(xa, y, mod_prev, g, mod3, w_in, lb, n_lat, ctx_row):
    B, NT, D = xa.shape
    TB = TOKEN_BLOCK
    nb = NT // TB
    HK = HG_HEADS * HG_DIM
    tok = pl.BlockSpec((1, TB, D), lambda b, t: (b, t, 0))
    tok_o = pl.BlockSpec((1, TB, HK), lambda b, t: (b, t, 0))
    const2 = lambda shape: pl.BlockSpec(shape, lambda b, t: (0, 0))
    out = jax.ShapeDtypeStruct((B, NT, HK), F32)
    return pl.pallas_call(
        _hgrn_proj_kernel,
        grid=(B, nb),
        in_specs=[
            tok, pl.BlockSpec((TB, D), lambda b, t: (b * nb + t, 0)), _mod_spec(5, n_lat, ctx_row),
            const2((1, D)), _mod_spec(1, n_lat, ctx_row), _mod_spec(0, n_lat, ctx_row),
            pl.BlockSpec((D, 5 * HK), lambda b, t: (0, 0), pipeline_mode=pl.Buffered(1)),
            const2((2, HK)),
        ],
        out_specs=[tok] + [tok_o] * 7,
        out_shape=[jax.ShapeDtypeStruct((B, NT, D), F32), out,
                   jax.ShapeDtypeStruct((B, NT, HK), BF16)] + [out] * 5,
        compiler_params=_cparams(("parallel", "arbitrary")),
        name="hgrn_proj",
    )(xa, y, mod_prev, g, mod3, mod3, w_in, lb)


def _scan_chunk_scores(q, k, vb, a, forward):
    C, SB = SCAN_CHUNK, SCAN_SUB
    row = lax.broadcasted_iota(jnp.int32, a.shape, 0)
    first = row < SB
    mid_row = SB // 2
    m = jnp.where(first, a[mid_row:mid_row + 1], a[SB + mid_row:SB + mid_row + 1])
    qd = q * jnp.exp(jnp.minimum(a - m, EXP_CLAMP))
    kd = k * jnp.exp(jnp.minimum(m - a, EXP_CLAMP))
    edge = a[SB - 1:SB] if forward else a[SB:SB + 1]
    e_x = jnp.exp(-jnp.abs(a - edge))
    nt = (((1,), (1,)), ((), ()))
    s_d = lax.dot_general(qd.astype(BF16), kd.astype(BF16), nt, preferred_element_type=F32)
    s_x = lax.dot_general((q * e_x).astype(BF16), (k * e_x).astype(BF16), nt,
                          preferred_element_type=F32)
    a_out = a[C - 1:C] if forward else a[0:1]
    k_out = k * jnp.exp(a_out - a)
    upd = lax.dot_general(vb, k_out.astype(BF16), (((0,), (0,)), ((), ())),
                          preferred_element_type=F32)
    return s_d, s_x, q * jnp.exp(a), upd, jnp.exp(a_out)


def _scan_chunk_intra(s_d, s_x, vb, forward):
    SB = SCAN_SUB
    t_i = lax.broadcasted_iota(jnp.int32, s_d.shape, 0)
    s_i = lax.broadcasted_iota(jnp.int32, s_d.shape, 1)
    same = (t_i < SB) == (s_i < SB)
    if forward:
        causal, cross = s_i <= t_i, (t_i >= SB) & (s_i < SB)
    else:
        causal, cross = s_i >= t_i, (t_i < SB) & (s_i >= SB)
    scores = jnp.where(same & causal, s_d, jnp.where(cross, s_x, 0.0))
    return jnp.dot(scores.astype(BF16), vb, preferred_element_type=F32)


def _hgrn_scan_kernel(q_ref, v_ref, kf_ref, lff_ref, kb_ref, lfb_ref, gate_ref, gain_ref,
                      o_ref, acc_ref, qe_ref, upd_ref, dec_ref, inter_ref, *, n_chunks, n_lat_chunks):
    C, G = SCAN_CHUNK, SCAN_GROUP
    R = C * G
    pos = lax.rem(lax.broadcasted_iota(jnp.int32, (R, HG_DIM), 0), C)

    def cumsum(lf, forward):
        x = lf
        step = 1
        while step < C:
            if forward:
                x = x + jnp.where(pos >= step, pltpu.roll(x, step, 0), 0.0)
            else:
                x = x + jnp.where(pos < C - step, pltpu.roll(x, R - step, 0), 0.0)
            step *= 2
        return x

    def group_scores(g):
        rows = pl.ds(pl.multiple_of(g * R, R), R)
        q = q_ref[0, rows, :]
        vb = v_ref[0, rows, :].astype(BF16)
        kf, kb = kf_ref[0, rows, :], kb_ref[0, rows, :]
        a_f = cumsum(lff_ref[0, rows, :], True)
        a_b = cumsum(lfb_ref[0, rows, :], False)
        res = []
        for ci in range(G):
            sl = slice(ci * C, (ci + 1) * C)
            res.append((vb[sl], _scan_chunk_scores(q[sl], kf[sl], vb[sl], a_f[sl], True),
                        _scan_chunk_scores(q[sl], kb[sl], vb[sl], a_b[sl], False)))
        return res

    def group_finish(g, res, st_f):
        o_sum = [_scan_chunk_intra(fw[0], fw[1], vb, True) + _scan_chunk_intra(bw[0], bw[1], vb, False)
                 for vb, fw, bw in res]
        rows = pl.ds(pl.multiple_of(g * R, R), R)
        for d in range(2):
            p = l = None
            scaled = [None] * G
            for ci in (range(G) if d == 0 else reversed(range(G))):
                _, _, qe, upd, dec = res[ci][1 + d]
                if p is None:
                    scaled[ci] = qe.astype(BF16)
                    p, l = dec, upd
                else:
                    scaled[ci] = (qe * p).astype(BF16)
                    o_sum[ci] = o_sum[ci] + lax.dot_general(qe.astype(BF16), l.astype(BF16), nt,
                                                            preferred_element_type=F32)
                    p, l = p * dec, l * dec + upd
            qe_group = jnp.concatenate(scaled, axis=0)
            if d == 0:
                inter_ref[0, rows, :] = lax.dot_general(qe_group, st_f.astype(BF16), nt,
                                                        preferred_element_type=F32)
                st_f = st_f * p + l
            else:
                qe_ref[rows, :] = qe_group
                upd_ref[g] = l
                dec_ref[pl.ds(g, 1), :] = p
        for ci in range(G):
            acc_ref[pl.ds(pl.multiple_of((g * G + ci) * C, C), C), :] = o_sum[ci]
        return st_f

    nt = (((1,), (1,)), ((), ()))
    n_groups = n_chunks // G
    n_trip = math.gcd(SCAN_TRIP, n_groups)
    zero = jnp.zeros((HG_DIM, HG_DIM), F32)

    def forward_group(k):
        return lax.rem(k + n_lat_chunks // G, n_groups)

    def local(i, st_f):
        nxt = group_scores(forward_group(i * n_trip))
        for j in range(n_trip):
            cur = nxt
            if j + 1 < n_trip:
                nxt = group_scores(forward_group(i * n_trip + j + 1))
            st_f = group_finish(forward_group(i * n_trip + j), cur, st_f)
        return st_f

    lax.fori_loop(0, n_groups // n_trip, local, zero)

    st_b = zero
    for gb in reversed(range(n_groups)):
        inter_ref[1, gb * R:(gb + 1) * R, :] = lax.dot_general(
            qe_ref[gb * R:(gb + 1) * R, :], st_b.astype(BF16), nt, preferred_element_type=F32)
        st_b = st_b * dec_ref[gb:gb + 1, :] + upd_ref[gb]
    o = acc_ref[...] + inter_ref[0] + inter_ref[1]
    o = o * lax.rsqrt(jnp.mean(o * o, axis=-1, keepdims=True) + EPS) * gain_ref[...]
    o_ref[0] = (o * _silu(gate_ref[0])).astype(BF16)


def _hgrn_scan(q, v, kf, lff, kb, lfb, gate, gain, t_lat):
    B, NT, HK = q.shape
    blk = pl.BlockSpec((1, NT, HG_DIM), lambda b, h: (b, 0, h))
    return pl.pallas_call(
        functools.partial(_hgrn_scan_kernel, n_chunks=NT // SCAN_CHUNK,
                          n_lat_chunks=t_lat // SCAN_CHUNK),
        grid=(B, HG_HEADS),
        in_specs=[blk] * 7 + [pl.BlockSpec((1, HG_DIM), lambda b, h: (0, 0))],
        out_specs=blk,
        out_shape=jax.ShapeDtypeStruct((B, NT, HK), BF16),
        scratch_shapes=[pltpu.VMEM((NT, HG_DIM), F32),
                        pltpu.VMEM((NT, HG_DIM), BF16),
                        pltpu.VMEM((NT // (SCAN_CHUNK * SCAN_GROUP), HG_DIM, HG_DIM), F32),
                        pltpu.VMEM((NT // (SCAN_CHUNK * SCAN_GROUP), HG_DIM), F32),
                        pltpu.VMEM((2, NT, HG_DIM), F32)],
        compiler_params=_cparams(("parallel", "arbitrary")),
        name="hgrn_scan",
    )(q, v, kf, lff, kb, lfb, gate, gain)


def _rope_tables(t_lat, t_ctx):
    rows = t_lat // GRID_W
    r = jnp.repeat(jnp.arange(rows, dtype=F32), GRID_W)
    col = jnp.tile(jnp.arange(GRID_W, dtype=F32), rows)
    n_pairs = DA_QK_DIM // 4
    inv = ROPE_BASE ** (-jnp.arange(n_pairs, dtype=F32) / n_pairs)
    ang = jnp.concatenate([r[:, None] * inv, col[:, None] * inv], axis=-1)
    cos, sin = jnp.cos(ang), jnp.sin(ang)
    cos_l = jnp.tile(cos, (1, LANES // cos.shape[1]))
    sin_l = jnp.tile(jnp.concatenate([-sin, sin], axis=-1), (1, LANES // (2 * sin.shape[1])))
    cos_l = jnp.concatenate([cos_l, jnp.ones((t_ctx, LANES), F32)], axis=0)
    sin_l = jnp.concatenate([sin_l, jnp.zeros((t_ctx, LANES), F32)], axis=0)
    return cos_l, sin_l


def kernel(x, c, ctx, c_ctx, ada_w, ada_b, norm_mix, norm_ffn, attn_w_qkv, attn_w_o, attn_q_norm,
           attn_k_norm, attn_sub_norm, attn_lambda, hgrn_w_in, hgrn_w_o, hgrn_out_norm,
           hgrn_lb_gamma, router_w, router_bias, moe_w_gate, moe_w_up, moe_w_down):
    B, T, D = x.shape
    Tc = ctx.shape[1]
    TB = TOKEN_BLOCK
    assert D == D_MODEL and T % TB == 0 and Tc % TB == 0 and T % GRID_W == 0
    assert ada_w.shape[0] == DEPTH == 2
    assert T % (SCAN_CHUNK * SCAN_GROUP) == 0 and Tc % (SCAN_CHUNK * SCAN_GROUP) == 0
    assert T % ATTN_Q_BLOCK == 0
    NT = T + Tc
    n_lat = T // TB
    n_tokens = B * NT
    ctx_row = B

    n_rows = -(-(B + 1) // 8) * 8
    cvec = jnp.concatenate([c, c_ctx[None, :], jnp.zeros((n_rows - B - 1, D), F32)], axis=0)
    mod = _modulation(cvec, ada_w, ada_b)

    wg_all, wu_all, wd_all = (w.astype(BF16) for w in (moe_w_gate, moe_w_up, moe_w_down))
    rwt = jnp.pad(router_w.astype(F32), ((0, 0), (0, LANES - N_EXPERTS)))
    rb = router_bias.reshape(N_EXPERTS, 1)

    p = jax.nn.softmax(hgrn_lb_gamma.astype(F32), axis=1)
    cum = jnp.cumsum(p, axis=1)
    lb_all = cum - cum[:, :1]

    cos_t, sin_t = _rope_tables(T, Tc)
    lane = jnp.arange(LANES)
    bd = (lane[:, None] // DA_QK_DIM == lane[None, :] // DA_QK_DIM).astype(BF16)

    xa = None
    pending = None
    for i in range(DEPTH):
        mod3 = mod[i].reshape(n_rows, 1, 6 * D)
        j = i // 2
        g_mix = norm_mix[i].reshape(1, D)
        if i % 2 == 0:
            assert i == 0
            lam_init = 0.8 - 0.6 * math.exp(-0.3 * i)
            q, k, v = _attn_project(
                x, ctx, g_mix, mod3, attn_w_qkv[j].astype(BF16),
                jnp.tile(attn_q_norm[j], LANES // DA_QK_DIM).reshape(1, LANES),
                jnp.tile(attn_k_norm[j], LANES // DA_QK_DIM).reshape(1, LANES),
                cos_t, sin_t, bd, n_lat, ctx_row)
            o = _attention(attn_lambda[j], q, k, v, attn_sub_norm[j].reshape(1, DA_V_DIM),
                           T, lam_init)
            w_o = attn_w_o[j]
        else:
            xa, *parts = _hgrn_project(xa, *pending, g_mix, mod3, hgrn_w_in[j].astype(BF16),
                                       lb_all[:, i], n_lat, ctx_row)
            o = _hgrn_scan(*parts, hgrn_out_norm[j].reshape(1, HG_DIM), T)
            w_o = hgrn_w_o[j]
        stream = (x, ctx, 0) if xa is None else (xa, xa, n_lat)
        xa, f_ext, bucket = _out_router(
            o, *stream, w_o.astype(BF16), mod3, norm_ffn[i].reshape(1, D), rwt, rb, n_lat, ctx_row)
        tables = _routing_tables(bucket.reshape(n_tokens), n_tokens)
        y = _expert_ffn(tables, f_ext.reshape(n_tokens, D + LANES), i, wg_all, wu_all, wd_all)
        pending = (y, mod3)
    assert DEPTH % 2 == 0
    y, mod3 = pending
    return _moe_combine(xa, mod3, y, n_lat, ctx_row, n_lat)
```

```python
import functools
import math

import jax
import jax.numpy as jnp
from jax import lax
from jax.experimental import pallas as pl
from jax.experimental.pallas import tpu as pltpu

F32 = jnp.float32
BF16 = jnp.bfloat16
HIGHEST = lax.Precision.HIGHEST

D_MODEL = 1024
DEPTH = 2
GRID_W = 64
DA_HEADS = 8
DA_QK_DIM = 64
DA_V_DIM = 128
ROPE_BASE = 10000.0
HG_HEADS = 8
HG_DIM = 128
N_EXPERTS = 16
N_GROUPS = 4
EXPERTS_PER_GROUP = 4
D_FF = 512
EPS = 1e-6

LANES = 128
TOKEN_BLOCK = 256
ATTN_Q_BLOCK = 256
ATTN_GROUP = 8
MOE_TILE = 256
MOE_DMA_UNROLL = 32
PAIRS = [(i, j) for i in range(EXPERTS_PER_GROUP) for j in range(i + 1, EXPERTS_PER_GROUP)]
N_BUCKETS = N_GROUPS * len(PAIRS)
HGRN_PROJ_PIECE = 256
SCAN_CHUNK = 64
SCAN_SUB = SCAN_CHUNK // 2
SCAN_GROUP = 4
SCAN_TRIP = 9
EXP_CLAMP = 80.0
VMEM_LIMIT = 56 * 1024 * 1024


def _cparams(sem):
    return pltpu.CompilerParams(dimension_semantics=sem, vmem_limit_bytes=VMEM_LIMIT)


def _silu(x):
    return x * jax.nn.sigmoid(x)


def _norm_mod(x, g, sc, sh):
    y = x * lax.rsqrt(jnp.mean(x * x, axis=-1, keepdims=True) + EPS)
    return (y * g) * (1.0 + sc) + sh


def _mod_kernel(c_ref, w_ref, b_ref, o_ref):
    o_ref[0] = jnp.dot(_silu(c_ref[...]), w_ref[0], preferred_element_type=F32,
                       precision=HIGHEST) + b_ref[0]


def _modulation(cvec, ada_w, ada_b):
    R, D = cvec.shape
    depth, _, n6 = ada_w.shape
    tn = 1024
    return pl.pallas_call(
        _mod_kernel,
        grid=(depth, n6 // tn),
        in_specs=[
            pl.BlockSpec((R, D), lambda i, j: (0, 0)),
            pl.BlockSpec((1, D, tn), lambda i, j: (i, 0, j)),
            pl.BlockSpec((1, 1, tn), lambda i, j: (i, 0, j)),
        ],
        out_specs=pl.BlockSpec((1, R, tn), lambda i, j: (i, 0, j)),
        out_shape=jax.ShapeDtypeStruct((depth, R, n6), F32),
        compiler_params=_cparams(("arbitrary", "arbitrary")),
        name="adaln_mod",
    )(cvec, ada_w, ada_b.reshape(depth, 1, n6))


def _mod_spec(col, n_lat, ctx_row):
    return pl.BlockSpec((1, 1, D_MODEL), lambda b, t: (jnp.where(t < n_lat, b, ctx_row), 0, col))


def _stream_specs(n_lat, ctx_block0):
    blk = (1, TOKEN_BLOCK, D_MODEL)
    return (pl.BlockSpec(blk, lambda b, t: (b, jnp.minimum(t, n_lat - 1), 0)),
            pl.BlockSpec(blk, lambda b, t: (b, jnp.maximum(t - n_lat, 0) + ctx_block0, 0)))


def _stream_block(lat_ref, ctx_ref, n_lat):
    return jnp.where(pl.program_id(1) < n_lat, lat_ref[0], ctx_ref[0])


def _attn_proj_kernel(x_ref, c_ref, g_ref, sc_ref, sh_ref, w_ref, qg_ref, kg_ref, cos_ref, sin_ref,
                      bd_ref, q_ref, k_ref, v_ref, *, n_lat):
    D = D_MODEL
    h = _norm_mod(_stream_block(x_ref, c_ref, n_lat), g_ref[...], sc_ref[0], sh_ref[0])
    qkv = jnp.dot(h.astype(BF16), w_ref[...], preferred_element_type=F32)
    cos, sin, bd = cos_ref[...], sin_ref[...], bd_ref[...]
    lane = lax.broadcasted_iota(jnp.int32, cos.shape, 1)
    upper = (lane & (DA_QK_DIM // 2)) != 0

    def norm_rope(t, gain, scale):
        sq = t * t
        sq_hi = sq.astype(BF16)
        sq_lo = (sq - sq_hi.astype(F32)).astype(BF16)
        ss = (jnp.dot(sq_hi, bd, preferred_element_type=F32)
              + jnp.dot(sq_lo, bd, preferred_element_type=F32))
        tn = t * lax.rsqrt(ss * (1.0 / DA_QK_DIM) + EPS) * gain
        partner = jnp.where(upper, pltpu.roll(tn, DA_QK_DIM // 2, 1),
                            pltpu.roll(tn, LANES - DA_QK_DIM // 2, 1))
        return (tn * cos + partner * sin) * scale

    for j in range(D // LANES):
        sl = slice(j * LANES, (j + 1) * LANES)
        q_ref[0, :, sl] = norm_rope(qkv[:, j * LANES:(j + 1) * LANES], qg_ref[...],
                                    math.log2(math.e) / math.sqrt(DA_QK_DIM)).astype(BF16)
        k_ref[0, j] = norm_rope(qkv[:, D + j * LANES:D + (j + 1) * LANES], kg_ref[...],
                                1.0).T.astype(BF16)
    v_ref[0] = qkv[:, 2 * D:].astype(BF16)


def _attn_project(x, ctx, g, mod3, w_qkv, qg, kg, cos_t, sin_t, bd, n_lat, ctx_row):
    B, T, D = x.shape
    NT = T + ctx.shape[1]
    TB = TOKEN_BLOCK
    tok = pl.BlockSpec((1, TB, D), lambda b, t: (b, t, 0))
    const2 = lambda shape: pl.BlockSpec(shape, lambda b, t: (0, 0))
    out = jax.ShapeDtypeStruct((B, NT, D), BF16)
    return pl.pallas_call(
        functools.partial(_attn_proj_kernel, n_lat=n_lat),
        grid=(B, NT // TB),
        in_specs=[
            *_stream_specs(n_lat, 0), const2((1, D)),
            _mod_spec(1, n_lat, ctx_row), _mod_spec(0, n_lat, ctx_row),
            const2((D, 3 * D)), const2((1, LANES)), const2((1, LANES)),
            pl.BlockSpec((TB, LANES), lambda b, t: (t, 0)),
            pl.BlockSpec((TB, LANES), lambda b, t: (t, 0)),
            const2((LANES, LANES)),
        ],
        out_specs=[tok, pl.BlockSpec((1, DA_HEADS, LANES, TB), lambda b, t: (b, 0, 0, t)), tok],
        out_shape=[out, jax.ShapeDtypeStruct((B, DA_HEADS, LANES, NT), BF16), out],
        compiler_params=_cparams(("parallel", "arbitrary")),
        name="attn_qkv_proj",
    )(x, ctx, g, mod3, mod3, w_qkv, qg, kg, cos_t, sin_t, bd)


def _attn_kernel(lam_ref, q_ref, kt_ref, v_ref, sn_ref, o_ref, *, t_lat, lam_init):
    QB = ATTN_Q_BLOCK
    lp = lam_ref[...]
    lam = (jnp.exp(jnp.sum(lp[0:1] * lp[1:2], keepdims=True))
           - jnp.exp(jnp.sum(lp[2:3] * lp[3:4], keepdims=True)) + lam_init)

    def scores(q, kt):
        lane = lax.broadcasted_iota(jnp.int32, q.shape, 1)
        zero = jnp.zeros_like(q)
        return (jnp.dot(jnp.where(lane < DA_QK_DIM, q, zero), kt, preferred_element_type=F32),
                jnp.dot(jnp.where(lane >= DA_QK_DIM, q, zero), kt, preferred_element_type=F32))

    def finish(s, v):
        def softmax_parts(sm):
            e = jnp.exp2(sm - jnp.max(sm, axis=-1, keepdims=True))
            return e, jnp.sum(e, axis=-1, keepdims=True)

        lane_v = lax.broadcasted_iota(jnp.int32, v.shape, 1)
        vx = jnp.concatenate([v, jnp.where(lane_v == 0, 1.0, 0.0).astype(BF16)], axis=1)

        def softmax_av(sm):
            e = jnp.exp2(sm - jnp.max(sm, axis=-1, keepdims=True)).astype(BF16)
            ov = jnp.dot(e, vx, preferred_element_type=F32)
            return ov[:, :DA_V_DIM] * (1.0 / ov[:, DA_V_DIM:DA_V_DIM + 1])

        o = softmax_av(s[0]) - lam * softmax_av(s[1])
        o = o * lax.rsqrt(jnp.mean(o * o, axis=-1, keepdims=True) + EPS) * sn_ref[...]
        return (o * (1.0 - lam_init)).astype(BF16)

    G = math.gcd(ATTN_GROUP, t_lat // QB)

    def latent_group(i, carry):
        rows = [pl.ds(pl.multiple_of((G * i + j) * QB, QB), QB) for j in range(G)]
        s_next = scores(q_ref[0, rows[0], :], kt_ref[0, 0])
        for j in range(G):
            s_cur = s_next
            if j + 1 < G:
                s_next = scores(q_ref[0, rows[j + 1], :], kt_ref[0, 0])
            o_ref[0, rows[j], :] = finish(s_cur, v_ref[0])
        return carry

    lax.fori_loop(0, t_lat // (G * QB), latent_group, 0)
    o_ref[0, t_lat:, :] = finish(scores(q_ref[0, t_lat:, :], kt_ref[0, 0, :, t_lat:]),
                                 v_ref[0, t_lat:, :])


def _attention(lam_p, q, kt, v, sub_norm, t_lat, lam_init):
    B, NT, D = q.shape
    blk = pl.BlockSpec((1, NT, LANES), lambda b, h: (b, 0, h))
    return pl.pallas_call(
        functools.partial(_attn_kernel, t_lat=t_lat, lam_init=lam_init),
        grid=(B, DA_HEADS),
        in_specs=[
            pl.BlockSpec(lam_p.shape, lambda b, h: (0, 0)),
            blk, pl.BlockSpec((1, 1, LANES, NT), lambda b, h: (b, h, 0, 0)), blk,
            pl.BlockSpec((1, LANES), lambda b, h: (0, 0)),
        ],
        out_specs=blk,
        out_shape=jax.ShapeDtypeStruct((B, NT, D), BF16),
        compiler_params=_cparams(("parallel", "arbitrary")),
        name="diff_attention",
    )(lam_p, q, kt, v, sub_norm)


def _out_router_kernel(o_ref, x_ref, c_ref, w_ref, gm_ref, g_ref, sc_ref, sh_ref, rw_ref, rb_ref,
                       xo_ref, f_ref, bk_ref, *, n_lat):
    D = D_MODEL
    out = jnp.dot(o_ref[0], w_ref[...], preferred_element_type=F32)
    x = _stream_block(x_ref, c_ref, n_lat) + gm_ref[0] * out
    xo_ref[0] = x
    f = _norm_mod(x, g_ref[...], sc_ref[0], sh_ref[0])
    f_ref[0, :, :D] = f
    def split3(a):
        a1 = a.astype(BF16)
        r = a - a1.astype(F32)
        a2 = r.astype(BF16)
        return a1, a2, (r - a2.astype(F32)).astype(BF16)

    f1, f2, f3 = split3(f)
    w1, w2, w3 = split3(rw_ref[...])
    mm = functools.partial(jnp.dot, preferred_element_type=F32)
    logits_t = ((mm(f3, w1) + mm(f2, w2) + mm(f1, w3)) + (mm(f2, w1) + mm(f1, w2))) + mm(f1, w1)
    logits = logits_t.T[:N_EXPERTS]
    aff = jax.nn.sigmoid(logits)
    biased = aff + rb_ref[...]
    G, E = EXPERTS_PER_GROUP, N_EXPERTS
    row = lax.broadcasted_iota(jnp.int32, biased.shape, 0)
    member = lax.rem(row, G)
    group = row // G

    def shifted(x, k):
        return jnp.where(member + k < G, pltpu.roll(x, E - k, 0), pltpu.roll(x, G - k, 0))

    rank = jnp.zeros_like(row)
    for k in range(1, G):
        other = shifted(biased, k)
        ahead = (other > biased) | ((other == biased) & (member + k >= G))
        rank = rank + ahead.astype(jnp.int32)
    top2 = rank < 2
    t = jnp.where(top2, biased, 0.0)
    gscore = t
    for k in range(1, G):
        gscore = gscore + shifted(t, k)
    best = jnp.ones_like(top2)
    for m in range(1, N_GROUPS):
        other = pltpu.roll(gscore, G * m, 0)
        best = best & ((gscore > other) | ((gscore == other) & (group < m)))
    chosen = top2 & best
    lowest = jnp.min(jnp.where(chosen, row, E), axis=0, keepdims=True)
    is_lo = chosen & (row == lowest)
    is_hi = chosen & (row != lowest)
    a_lo = jnp.sum(jnp.where(is_lo, aff, 0.0), axis=0, keepdims=True)
    a_hi = jnp.sum(jnp.where(is_hi, aff, 0.0), axis=0, keepdims=True)
    m_lo = lax.rem(lowest, G)
    m_hi = jnp.sum(jnp.where(is_hi, member, 0), axis=0, keepdims=True)
    pair = m_lo * (2 * G - 1 - m_lo) // 2 + (m_hi - m_lo - 1)
    tot = a_lo + a_hi
    bk_ref[0] = (lowest // G) * len(PAIRS) + pair

    tb = x.shape[0]
    eye = (lax.broadcasted_iota(jnp.int32, (tb, tb), 0)
           == lax.broadcasted_iota(jnp.int32, (tb, tb), 1))

    def column(row):
        return jnp.sum(jnp.where(eye, row, 0.0), axis=1, keepdims=True)

    lane = lax.broadcasted_iota(jnp.int32, (tb, LANES), 1)
    f_ref[0, :, D:] = jnp.where(lane < LANES // 2, column(a_lo / tot), column(a_hi / tot))


def _out_router(o, x, ctx, ctx_block0, w_o, mod3, g_ffn, rwt, rb, n_lat, ctx_row):
    B, NT, D = o.shape
    TB = TOKEN_BLOCK
    nb = NT // TB
    tok = pl.BlockSpec((1, TB, D), lambda b, t: (b, t, 0))
    row = pl.BlockSpec((1, 1, TB), lambda b, t: (b * nb + t, 0, 0))
    const2 = lambda shape: pl.BlockSpec(shape, lambda b, t: (0, 0))
    rows = lambda dt: jax.ShapeDtypeStruct((B * nb, 1, TB), dt)
    return pl.pallas_call(
        functools.partial(_out_router_kernel, n_lat=n_lat),
        grid=(B, nb),
        in_specs=[
            tok, *_stream_specs(n_lat, ctx_block0), const2((D, D)),
            _mod_spec(2, n_lat, ctx_row), const2((1, D)),
            _mod_spec(4, n_lat, ctx_row), _mod_spec(3, n_lat, ctx_row),
            const2((D, LANES)), const2((N_EXPERTS, 1)),
        ],
        out_specs=[tok, pl.BlockSpec((1, TB, D + LANES), lambda b, t: (b, t, 0)), row],
        out_shape=[jax.ShapeDtypeStruct((B, NT, D), F32),
                   jax.ShapeDtypeStruct((B, NT, D + LANES), F32), rows(jnp.int32)],
        compiler_params=_cparams(("parallel", "arbitrary")),
        name="out_proj_router",
    )(o, x, ctx, w_o, mod3, g_ffn, mod3, mod3, rwt, rb)


def _routing_tables(bucket, n_tokens):
    tm = MOE_TILE
    max_tiles = n_tokens // tm + N_BUCKETS
    onehot = (bucket[:, None] == jnp.arange(N_BUCKETS, dtype=jnp.int32)[None, :]).astype(jnp.int32)
    count = jnp.sum(onehot, axis=0)
    rank = jnp.sum(jnp.cumsum(onehot, axis=0) * onehot, axis=1) - 1
    btiles = (count + tm - 1) // tm
    tile_end = jnp.cumsum(btiles)
    tile_start = tile_end - btiles
    dest = tile_start[bucket] * tm + rank
    tile = jnp.arange(max_tiles, dtype=jnp.int32)
    tile_bucket = jnp.minimum(jnp.sum((tile[:, None] >= tile_end[None, :]).astype(jnp.int32), axis=1),
                              N_BUCKETS - 1)
    in_bucket = tile - tile_start[tile_bucket]
    n_valid = jnp.clip(count[tile_bucket] - in_bucket * tm, 0, tm)
    n_valid = jnp.where(tile < tile_end[-1], n_valid, 0).astype(jnp.int32)
    grp = tile_bucket // len(PAIRS)
    pair = tile_bucket % len(PAIRS)
    lo_tab = jnp.array([p[0] for p in PAIRS], jnp.int32)
    hi_tab = jnp.array([p[1] for p in PAIRS], jnp.int32)
    tile_lo = grp * EXPERTS_PER_GROUP + lo_tab[pair]
    tile_hi = grp * EXPERTS_PER_GROUP + hi_tab[pair]
    return dest.astype(jnp.int32), n_valid, tile_lo, tile_hi


def _expert_kernel(dest_ref, nv_ref, tlo_ref, thi_ref, f_hbm, wg_lo_ref, wu_lo_ref, wd_lo_ref,
                   wg_hi_ref, wu_hi_ref, wd_hi_ref, y_hbm, xbuf, ybuf, tok_ref, gsem, ssem):
    TM, D = MOE_TILE, D_MODEL
    i = pl.program_id(0)
    n = pl.num_programs(0)
    slot = lax.rem(i, 2)
    other = 1 - slot

    @pl.when(i == 0)
    def _():
        U = MOE_DMA_UNROLL

        def invert(g, carry):
            for u in range(U):
                t = g * U + u
                tok_ref[dest_ref[t]] = t
            return carry

        lax.fori_loop(0, dest_ref.shape[0] // U, invert, 0)

    def gather_copy(tile, r, s):
        return pltpu.make_async_copy(f_hbm.at[pl.ds(tok_ref[tile * TM + r], 1), :],
                                     xbuf.at[s, pl.ds(r, 1), :], gsem.at[s])

    def scatter_copy(tile, r, s):
        return pltpu.make_async_copy(ybuf.at[s, pl.ds(r, 1), :],
                                     y_hbm.at[pl.ds(tok_ref[tile * TM + r], 1), :], ssem.at[s])

    def for_rows(n_rows, fn):
        U = MOE_DMA_UNROLL
        for g in range(TM // U):
            @pl.when((g + 1) * U <= n_rows)
            def _():
                for u in range(U):
                    fn(g * U + u)

        def single(r, carry):
            fn(r)
            return carry

        lax.fori_loop((n_rows // U) * U, n_rows, single, 0)

    def wait_rows(n_rows, whole_tile_copy, row_copy):
        @pl.when(n_rows == TM)
        def _():
            whole_tile_copy.wait()

        @pl.when(n_rows < TM)
        def _():
            for_rows(n_rows, lambda r: row_copy(r).wait())

    def start_gather(tile, s):
        for_rows(nv_ref[tile], lambda r: gather_copy(tile, r, s).start())

    def wait_gather(tile, s):
        wait_rows(nv_ref[tile],
                  pltpu.make_async_copy(f_hbm.at[pl.ds(0, TM), :], xbuf.at[s], gsem.at[s]),
                  lambda r: gather_copy(tile, r, s))

    def start_scatter(tile, s):
        for_rows(nv_ref[tile], lambda r: scatter_copy(tile, r, s).start())

    def wait_scatter(tile, s):
        wait_rows(nv_ref[tile],
                  pltpu.make_async_copy(ybuf.at[s], y_hbm.at[pl.ds(0, TM), :], ssem.at[s]),
                  lambda r: scatter_copy(tile, r, s))

    @pl.when(i == 0)
    def _():
        xbuf[...] = jnp.zeros_like(xbuf)
        start_gather(0, 0)

    wait_gather(i, slot)

    @pl.when(i >= 2)
    def _():
        wait_scatter(i - 2, slot)

    @pl.when(i + 1 < n)
    def _():
        start_gather(i + 1, other)

    @pl.when(nv_ref[i] > 0)
    def _():
        xe = xbuf[slot]
        x = xe[:, :D].astype(BF16)

        def ffn(wg_ref, wu_ref, wd_ref):
            hid = (_silu(jnp.dot(x, wg_ref[0, 0], preferred_element_type=F32))
                   * jnp.dot(x, wu_ref[0, 0], preferred_element_type=F32))
            return jnp.dot(hid.astype(BF16), wd_ref[0, 0], preferred_element_type=F32)

        ybuf[slot] = (xe[:, D:D + 1] * ffn(wg_lo_ref, wu_lo_ref, wd_lo_ref)
                      + xe[:, D + LANES // 2:D + LANES // 2 + 1] * ffn(wg_hi_ref, wu_hi_ref, wd_hi_ref))
        start_scatter(i, slot)

    @pl.when(i == n - 1)
    def _():
        wait_scatter(i - 1, other)
        wait_scatter(i, slot)


def _expert_ffn(tables, f_ext, layer, w_gate, w_up, w_down):
    dest, n_valid, tile_lo, tile_hi = tables
    N = f_ext.shape[0]
    assert N % MOE_DMA_UNROLL == 0
    D = D_MODEL
    tm = MOE_TILE
    lo = lambda i, dst, nv, tlo, thi: (layer, tlo[i], 0, 0)
    hi = lambda i, dst, nv, tlo, thi: (layer, thi[i], 0, 0)
    w_in, w_out = (1, 1, D, D_FF), (1, 1, D_FF, D)
    return pl.pallas_call(
        _expert_kernel,
        grid_spec=pltpu.PrefetchScalarGridSpec(
            num_scalar_prefetch=4,
            grid=(n_valid.shape[0],),
            in_specs=[
                pl.BlockSpec(memory_space=pl.ANY),
                pl.BlockSpec(w_in, lo), pl.BlockSpec(w_in, lo), pl.BlockSpec(w_out, lo),
                pl.BlockSpec(w_in, hi), pl.BlockSpec(w_in, hi), pl.BlockSpec(w_out, hi),
            ],
            out_specs=pl.BlockSpec(memory_space=pl.ANY),
            scratch_shapes=[pltpu.VMEM((2, tm, D + LANES), F32), pltpu.VMEM((2, tm, D), F32),
                            pltpu.SMEM((n_valid.shape[0] * tm,), jnp.int32),
                            pltpu.SemaphoreType.DMA((2,)), pltpu.SemaphoreType.DMA((2,))],
        ),
        out_shape=jax.ShapeDtypeStruct((N, D), F32),
        compiler_params=_cparams(("arbitrary",)),
        name="moe_expert_ffn",
    )(dest, n_valid, tile_lo, tile_hi, f_ext, w_gate, w_up, w_down, w_gate, w_up, w_down)


def _combine_kernel(x_ref, gf_ref, y_ref, o_ref):
    o_ref[0] = x_ref[0] + gf_ref[0] * y_ref[...]


def _moe_combine(xa, mod3, y, n_lat, ctx_row, n_blocks_out):
    B, NT, D = xa.shape
    TB = TOKEN_BLOCK
    nb = NT // TB
    tok = pl.BlockSpec((1, TB, D), lambda b, t: (b, t, 0))
    return pl.pallas_call(
        _combine_kernel,
        grid=(B, n_blocks_out),
        in_specs=[tok, _mod_spec(5, n_lat, ctx_row),
                  pl.BlockSpec((TB, D), lambda b, t: (b * nb + t, 0))],
        out_specs=tok,
        out_shape=jax.ShapeDtypeStruct((B, n_blocks_out * TB, D), F32),
        compiler_params=_cparams(("parallel", "arbitrary")),
        name="moe_combine",
    )(xa, mod3, y)


def _hgrn_proj_kernel(x_ref, y_ref, gf_ref, g_ref, sc_ref, sh_ref, w_ref, lb_ref,
                      xo_ref, q_ref, v_ref, kf_ref, lff_ref, kb_ref, lfb_ref, gate_ref):
    HK = HG_HEADS * HG_DIM
    x = x_ref[0] + gf_ref[0] * y_ref[...]
    xo_ref[0] = x
    h = _norm_mod(x, g_ref[...], sc_ref[0], sh_ref[0]).astype(BF16)

    W = HGRN_PROJ_PIECE

    def proj(part, c):
        return jnp.dot(h, w_ref[:, part * HK + c:part * HK + c + W], preferred_element_type=F32)

    def forget(z, lbd, k_ref, lf_ref, cols):
        e = jnp.exp(-jnp.abs(z))
        t = 1.0 + e
        k_ref[0, :, cols] = (1.0 - lbd) * (jnp.where(z >= 0.0, e, 1.0) / t)
        a = jnp.log(lbd)
        b = jnp.log1p(-lbd) + (jnp.minimum(z, 0.0) - jnp.log(t))
        lf_ref[0, :, cols] = jnp.maximum(a, b) + jnp.log(1.0 + jnp.exp(-jnp.abs(a - b)))

    def finish(part, c, y):
        cols = slice(c, c + W)
        if part == 0:
            q_ref[0, :, cols] = _silu(y)
        elif part == 1:
            v_ref[0, :, cols] = y.astype(BF16)
        elif part == 2:
            forget(y, lb_ref[0:1, cols], kf_ref, lff_ref, cols)
        elif part == 3:
            forget(y, lb_ref[1:2, cols], kb_ref, lfb_ref, cols)
        else:
            gate_ref[0, :, cols] = y

    pieces = [(part, c) for c in range(0, HK, W) for part in (2, 0, 3, 1, 4)]
    nxt = proj(*pieces[0])
    for k, piece in enumerate(pieces):
        cur = nxt
        if k + 1 < len(pieces):
            nxt = proj(*pieces[k + 1])
        finish(*piece, cur)


def _hgrn_project(xa, y, mod_prev, g, mod3, w_in, lb, n_lat, ctx_row):
    B, NT, D = xa.shape
    TB = TOKEN_BLOCK
    nb = NT // TB
    HK = HG_HEADS * HG_DIM
    tok = pl.BlockSpec((1, TB, D), lambda b, t: (b, t, 0))
    tok_o = pl.BlockSpec((1, TB, HK), lambda b, t: (b, t, 0))
    const2 = lambda shape: pl.BlockSpec(shape, lambda b, t: (0, 0))
    out = jax.ShapeDtypeStruct((B, NT, HK), F32)
    return pl.pallas_call(
        _hgrn_proj_kernel,
        grid=(B, nb),
        in_specs=[
            tok, pl.BlockSpec((TB, D), lambda b, t: (b * nb + t, 0)), _mod_spec(5, n_lat, ctx_row),
            const2((1, D)), _mod_spec(1, n_lat, ctx_row), _mod_spec(0, n_lat, ctx_row),
            pl.BlockSpec((D, 5 * HK), lambda b, t: (0, 0), pipeline_mode=pl.Buffered(1)),
            const2((2, HK)),
        ],
        out_specs=[tok] + [tok_o] * 7,
        out_shape=[jax.ShapeDtypeStruct((B, NT, D), F32), out,
                   jax.ShapeDtypeStruct((B, NT, HK), BF16)] + [out] * 5,
        compiler_params=_cparams(("parallel", "arbitrary")),
        name="hgrn_proj",
    )(xa, y, mod_prev, g, mod3, mod3, w_in, lb)


def _scan_chunk_scores(q, k, vb, a, forward):
    C, SB = SCAN_CHUNK, SCAN_SUB
    row = lax.broadcasted_iota(jnp.int32, a.shape, 0)
    first = row < SB
    mid_row = SB // 2
    m = jnp.where(first, a[mid_row:mid_row + 1], a[SB + mid_row:SB + mid_row + 1])
    qd = q * jnp.exp(jnp.minimum(a - m, EXP_CLAMP))
    kd = k * jnp.exp(jnp.minimum(m - a, EXP_CLAMP))
    edge = a[SB - 1:SB] if forward else a[SB:SB + 1]
    e_x = jnp.exp(-jnp.abs(a - edge))
    nt = (((1,), (1,)), ((), ()))
    s_d = lax.dot_general(qd.astype(BF16), kd.astype(BF16), nt, preferred_element_type=F32)
    s_x = lax.dot_general((q * e_x).astype(BF16), (k * e_x).astype(BF16), nt,
                          preferred_element_type=F32)
    a_out = a[C - 1:C] if forward else a[0:1]
    k_out = k * jnp.exp(a_out - a)
    upd = lax.dot_general(vb, k_out.astype(BF16), (((0,), (0,)), ((), ())),
                          preferred_element_type=F32)
    return s_d, s_x, q * jnp.exp(a), upd, jnp.exp(a_out)


def _scan_chunk_intra(s_d, s_x, vb, forward):
    SB = SCAN_SUB
    t_i = lax.broadcasted_iota(jnp.int32, s_d.shape, 0)
    s_i = lax.broadcasted_iota(jnp.int32, s_d.shape, 1)
    same = (t_i < SB) == (s_i < SB)
    if forward:
        causal, cross = s_i <= t_i, (t_i >= SB) & (s_i < SB)
    else:
        causal, cross = s_i >= t_i, (t_i < SB) & (s_i >= SB)
    scores = jnp.where(same & causal, s_d, jnp.where(cross, s_x, 0.0))
    return jnp.dot(scores.astype(BF16), vb, preferred_element_type=F32)


def _hgrn_scan_kernel(q_ref, v_ref, kf_ref, lff_ref, kb_ref, lfb_ref, gate_ref, gain_ref,
                      o_ref, acc_ref, qe_ref, upd_ref, dec_ref, inter_ref, *, n_chunks, n_lat_chunks):
    C, G = SCAN_CHUNK, SCAN_GROUP
    R = C * G
    pos = lax.rem(lax.broadcasted_iota(jnp.int32, (R, HG_DIM), 0), C)

    def cumsum(lf, forward):
        x = lf
        step = 1
        while step < C:
            if forward:
                x = x + jnp.where(pos >= step, pltpu.roll(x, step, 0), 0.0)
            else:
                x = x + jnp.where(pos < C - step, pltpu.roll(x, R - step, 0), 0.0)
            step *= 2
        return x

    def group_scores(g):
        rows = pl.ds(pl.multiple_of(g * R, R), R)
        q = q_ref[0, rows, :]
        vb = v_ref[0, rows, :].astype(BF16)
        kf, kb = kf_ref[0, rows, :], kb_ref[0, rows, :]
        a_f = cumsum(lff_ref[0, rows, :], True)
        a_b = cumsum(lfb_ref[0, rows, :], False)
        res = []
        for ci in range(G):
            sl = slice(ci * C, (ci + 1) * C)
            res.append((vb[sl], _scan_chunk_scores(q[sl], kf[sl], vb[sl], a_f[sl], True),
                        _scan_chunk_scores(q[sl], kb[sl], vb[sl], a_b[sl], False)))
        return res

    def group_finish(g, res, st_f):
        o_sum = [_scan_chunk_intra(fw[0], fw[1], vb, True) + _scan_chunk_intra(bw[0], bw[1], vb, False)
                 for vb, fw, bw in res]
        rows = pl.ds(pl.multiple_of(g * R, R), R)
        for d in range(2):
            p = l = None
            scaled = [None] * G
            for ci in (range(G) if d == 0 else reversed(range(G))):
                _, _, qe, upd, dec = res[ci][1 + d]
                if p is None:
                    scaled[ci] = qe.astype(BF16)
                    p, l = dec, upd
                else:
                    scaled[ci] = (qe * p).astype(BF16)
                    o_sum[ci] = o_sum[ci] + lax.dot_general(qe.astype(BF16), l.astype(BF16), nt,
                                                            preferred_element_type=F32)
                    p, l = p * dec, l * dec + upd
            qe_group = jnp.concatenate(scaled, axis=0)
            if d == 0:
                inter_ref[0, rows, :] = lax.dot_general(qe_group, st_f.astype(BF16), nt,
                                                        preferred_element_type=F32)
                st_f = st_f * p + l
            else:
                qe_ref[rows, :] = qe_group
                upd_ref[g] = l
                dec_ref[pl.ds(g, 1), :] = p
        for ci in range(G):
            acc_ref[pl.ds(pl.multiple_of((g * G + ci) * C, C), C), :] = o_sum[ci]
        return st_f

    nt = (((1,), (1,)), ((), ()))
    n_groups = n_chunks // G
    n_trip = math.gcd(SCAN_TRIP, n_groups)
    zero = jnp.zeros((HG_DIM, HG_DIM), F32)

    def forward_group(k):
        return lax.rem(k + n_lat_chunks // G, n_groups)

    def local(i, st_f):
        nxt = group_scores(forward_group(i * n_trip))
        for j in range(n_trip):
            cur = nxt
            if j + 1 < n_trip:
                nxt = group_scores(forward_group(i * n_trip + j + 1))
            st_f = group_finish(forward_group(i * n_trip + j), cur, st_f)
        return st_f

    lax.fori_loop(0, n_groups // n_trip, local, zero)

    st_b = zero
    for gb in reversed(range(n_groups)):
        inter_ref[1, gb * R:(gb + 1) * R, :] = lax.dot_general(
            qe_ref[gb * R:(gb + 1) * R, :], st_b.astype(BF16), nt, preferred_element_type=F32)
        st_b = st_b * dec_ref[gb:gb + 1, :] + upd_ref[gb]
    o = acc_ref[...] + inter_ref[0] + inter_ref[1]
    o = o * lax.rsqrt(jnp.mean(o * o, axis=-1, keepdims=True) + EPS) * gain_ref[...]
    o_ref[0] = (o * _silu(gate_ref[0])).astype(BF16)


def _hgrn_scan(q, v, kf, lff, kb, lfb, gate, gain, t_lat):
    B, NT, HK = q.shape
    blk = pl.BlockSpec((1, NT, HG_DIM), lambda b, h: (b, 0, h))
    return pl.pallas_call(
        functools.partial(_hgrn_scan_kernel, n_chunks=NT // SCAN_CHUNK,
                          n_lat_chunks=t_lat // SCAN_CHUNK),
        grid=(B, HG_HEADS),
        in_specs=[blk] * 7 + [pl.BlockSpec((1, HG_DIM), lambda b, h: (0, 0))],
        out_specs=blk,
        out_shape=jax.ShapeDtypeStruct((B, NT, HK), BF16),
        scratch_shapes=[pltpu.VMEM((NT, HG_DIM), F32),
                        pltpu.VMEM((NT, HG_DIM), BF16),
                        pltpu.VMEM((NT // (SCAN_CHUNK * SCAN_GROUP), HG_DIM, HG_DIM), F32),
                        pltpu.VMEM((NT // (SCAN_CHUNK * SCAN_GROUP), HG_DIM), F32),
                        pltpu.VMEM((2, NT, HG_DIM), F32)],
        compiler_params=_cparams(("parallel", "arbitrary")),
        name="hgrn_scan",
    )(q, v, kf, lff, kb, lfb, gate, gain)


def _rope_tables(t_lat, t_ctx):
    rows = t_lat // GRID_W
    r = jnp.repeat(jnp.arange(rows, dtype=F32), GRID_W)
    col = jnp.tile(jnp.arange(GRID_W, dtype=F32), rows)
    n_pairs = DA_QK_DIM // 4
    inv = ROPE_BASE ** (-jnp.arange(n_pairs, dtype=F32) / n_pairs)
    ang = jnp.concatenate([r[:, None] * inv, col[:, None] * inv], axis=-1)
    cos, sin = jnp.cos(ang), jnp.sin(ang)
    cos_l = jnp.tile(cos, (1, LANES // cos.shape[1]))
    sin_l = jnp.tile(jnp.concatenate([-sin, sin], axis=-1), (1, LANES // (2 * sin.shape[1])))
    cos_l = jnp.concatenate([cos_l, jnp.ones((t_ctx, LANES), F32)], axis=0)
    sin_l = jnp.concatenate([sin_l, jnp.zeros((t_ctx, LANES), F32)], axis=0)
    return cos_l, sin_l


def kernel(x, c, ctx, c_ctx, ada_w, ada_b, norm_mix, norm_ffn, attn_w_qkv, attn_w_o, attn_q_norm,
           attn_k_norm, attn_sub_norm, attn_lambda, hgrn_w_in, hgrn_w_o, hgrn_out_norm,
           hgrn_lb_gamma, router_w, router_bias, moe_w_gate, moe_w_up, moe_w_down):
    B, T, D = x.shape
    Tc = ctx.shape[1]
    TB = TOKEN_BLOCK
    assert D == D_MODEL and T % TB == 0 and Tc % TB == 0 and T % GRID_W == 0
    assert ada_w.shape[0] == DEPTH == 2
    assert T % (SCAN_CHUNK * SCAN_GROUP) == 0 and Tc % (SCAN_CHUNK * SCAN_GROUP) == 0
    assert T % ATTN_Q_BLOCK == 0
    NT = T + Tc
    n_lat = T // TB
    n_tokens = B * NT
    ctx_row = B

    n_rows = -(-(B + 1) // 8) * 8
    cvec = jnp.concatenate([c, c_ctx[None, :], jnp.zeros((n_rows - B - 1, D), F32)], axis=0)
    mod = _modulation(cvec, ada_w, ada_b)

    wg_all, wu_all, wd_all = (w.astype(BF16) for w in (moe_w_gate, moe_w_up, moe_w_down))
    rwt = jnp.pad(router_w.astype(F32), ((0, 0), (0, LANES - N_EXPERTS)))
    rb = router_bias.reshape(N_EXPERTS, 1)

    p = jax.nn.softmax(hgrn_lb_gamma.astype(F32), axis=1)
    cum = jnp.cumsum(p, axis=1)
    lb_all = cum - cum[:, :1]

    cos_t, sin_t = _rope_tables(T, Tc)
    lane = jnp.arange(LANES)
    bd = (lane[:, None] // DA_QK_DIM == lane[None, :] // DA_QK_DIM).astype(BF16)

    xa = None
    pending = None
    for i in range(DEPTH):
        mod3 = mod[i].reshape(n_rows, 1, 6 * D)
        j = i // 2
        g_mix = norm_mix[i].reshape(1, D)
        if i % 2 == 0:
            assert i == 0
            lam_init = 0.8 - 0.6 * math.exp(-0.3 * i)
            q, k, v = _attn_project(
                x, ctx, g_mix, mod3, attn_w_qkv[j].astype(BF16),
                jnp.tile(attn_q_norm[j], LANES // DA_QK_DIM).reshape(1, LANES),
                jnp.tile(attn_k_norm[j], LANES // DA_QK_DIM).reshape(1, LANES),
                cos_t, sin_t, bd, n_lat, ctx_row)
            o = _attention(attn_lambda[j], q, k, v, attn_sub_norm[j].reshape(1, DA_V_DIM),
                           T, lam_init)
            w_o = attn_w_o[j]
        else:
            xa, *parts = _hgrn_project(xa, *pending, g_mix, mod3, hgrn_w_in[j].astype(BF16),
                                       lb_all[:, i], n_lat, ctx_row)
            o = _hgrn_scan(*parts, hgrn_out_norm[j].reshape(1, HG_DIM), T)
            w_o = hgrn_w_o[j]
        stream = (x, ctx, 0) if xa is None else (xa, xa, n_lat)
        xa, f_ext, bucket = _out_router(
            o, *stream, w_o.astype(BF16), mod3, norm_ffn[i].reshape(1, D), rwt, rb, n_lat, ctx_row)
        tables = _routing_tables(bucket.reshape(n_tokens), n_tokens)
        y = _expert_ffn(tables, f_ext.reshape(n_tokens, D + LANES), i, wg_all, wu_all, wd_all)
        pending = (y, mod3)
    assert DEPTH % 2 == 0
    y, mod3 = pending
    return _moe_combine(xa, mod3, y, n_lat, ctx_row, n_lat)
```
